```python
import jax, jax.numpy as jnp
from jax import lax
import numpy as np

D_MODEL = 1024
BATCH = 4
SEQ = 4096
DEPTH = 1

GRID_W = 64
EPS = 1e-6
GLA_HEADS = 4
GLA_DK = D_MODEL // (2 * GLA_HEADS)
GLA_DV = D_MODEL // GLA_HEADS
GLA_QK = GLA_HEADS * GLA_DK
GLA_V = GLA_HEADS * GLA_DV
GLA_RANK = 16
GLA_GATE_NORM = 16.0
GLA_CHUNK = 64
ATT_HD = 128
ATT_HEADS = D_MODEL // ATT_HD
ATT_KV_HEADS = 2
ATT_Q = ATT_HEADS * ATT_HD
ATT_KV = ATT_KV_HEADS * ATT_HD
ATT_BLOCK = 128
ROPE_THETA = 10000.0
ROPE_AXIS_DIM = ATT_HD // 2
D_FF = -(-8 * D_MODEL // (3 * 256)) * 256
IN_SPLITS = (GLA_QK, GLA_QK, GLA_V, GLA_V, GLA_RANK, GLA_RANK,
             ATT_Q, ATT_KV, ATT_KV, D_MODEL, D_MODEL)
D_IN = sum(IN_SPLITS)

kernel_name = "hybrid_gla_axial_gqa_gated_encoder"


def rmsnorm(x, g):
    xf = x.astype(jnp.float32)
    y = xf * lax.rsqrt(jnp.mean(xf * xf, axis=-1, keepdims=True) + EPS)
    return (y * g.astype(jnp.float32)).astype(x.dtype)


def split_points():
    return [int(v) for v in np.cumsum(IN_SPLITS)[:-1]]


def gla_scan(q, k, v, log_a, strict):
    B, H, S, dk = q.shape
    dv = v.shape[-1]
    n = S // GLA_CHUNK

    def to_chunks(t):
        return jnp.moveaxis(t.astype(jnp.float32).reshape(B, H, n, GLA_CHUNK, t.shape[-1]), 2, 0)

    qc, kc, vc, ac = to_chunks(q), to_chunks(k), to_chunks(v), to_chunks(log_a)
    idx = jnp.arange(GLA_CHUNK)
    mask = (idx[:, None] > idx[None, :]) if strict else (idx[:, None] >= idx[None, :])

    def step(state, inp):
        qi, ki, vi, ai = inp
        cum = jnp.cumsum(ai, axis=2)
        diff = cum[:, :, :, None, :] - cum[:, :, None, :, :]
        decay = jnp.exp(jnp.where(mask[None, None, :, :, None], diff, -jnp.inf))
        scores = jnp.einsum('bhid,bhijd,bhjd->bhij', qi, decay, ki)
        o_intra = jnp.einsum('bhij,bhjv->bhiv', scores, vi)
        o_inter = jnp.einsum('bhid,bhdv->bhiv', qi * jnp.exp(cum), state)
        last = cum[:, :, -1:, :]
        k_dec = ki * jnp.exp(last - cum)
        state = state * jnp.exp(last[:, :, 0, :])[..., None] + jnp.einsum('bhjd,bhjv->bhdv', k_dec, vi)
        return state, o_intra + o_inter

    state0 = jnp.zeros((B, H, dk, dv), jnp.float32)
    _, out = lax.scan(step, state0, (qc, kc, vc, ac))
    return jnp.moveaxis(out, 0, 2).reshape(B, H, S, dv)


def axial_rope_tables(seq_len):
    rows = seq_len // GRID_W
    row = jnp.repeat(jnp.arange(rows), GRID_W).astype(jnp.float32)
    col = jnp.tile(jnp.arange(GRID_W), rows).astype(jnp.float32)
    freqs = ROPE_THETA ** (-jnp.arange(0, ROPE_AXIS_DIM, 2, dtype=jnp.float32) / ROPE_AXIS_DIM)
    ang_r = row[:, None] * freqs[None, :]
    ang_c = col[:, None] * freqs[None, :]
    return jnp.cos(ang_r), jnp.sin(ang_r), jnp.cos(ang_c), jnp.sin(ang_c)


def rope_half(x, cos, sin):
    c, s = cos[:, None, :], sin[:, None, :]
    x1, x2 = jnp.split(x, 2, axis=-1)
    return jnp.concatenate([x1 * c - x2 * s, x2 * c + x1 * s], axis=-1)


def apply_axial_rope(x, tables):
    cr, sr, cc, sc = tables
    xf = x.astype(jnp.float32)
    xr, xc = xf[..., :ROPE_AXIS_DIM], xf[..., ROPE_AXIS_DIM:]
    return jnp.concatenate([rope_half(xr, cr, sr), rope_half(xc, cc, sc)], axis=-1)


def block_attention(q, k, v):
    B, Hkv, G, S, hd = q.shape
    nb = S // ATT_BLOCK
    qb = jnp.moveaxis(q.reshape(B, Hkv, G, nb, ATT_BLOCK, hd), 3, 0)
    scale = hd ** -0.5

    def one(qi):
        s = jnp.einsum('bkgqd,bksd->bkgqs', qi, k) * scale
        p = jax.nn.softmax(s, axis=-1)
        return jnp.einsum('bkgqs,bksd->bkgqd', p, v)

    o = lax.map(one, qb)
    return jnp.moveaxis(o, 0, 3).reshape(B, Hkv, G, S, hd)


def setup_inputs(seed: int = 0) -> dict:
    key = jax.random.key(seed)
    ks = jax.random.split(key, 20)
    f32 = jnp.float32
    nrm = lambda k, shape, s: (jax.random.normal(k, shape, f32) * s)
    gain = lambda k, shape: 1.0 + 0.05 * jax.random.normal(k, shape, f32)
    return {
        "x": jax.random.normal(ks[0], (BATCH, SEQ, D_MODEL), f32),
        "mix_norm": gain(ks[1], (DEPTH, D_MODEL)),
        "w_in": nrm(ks[2], (DEPTH, D_MODEL, D_IN), D_MODEL ** -0.5),
        "w_gk_fwd": nrm(ks[3], (DEPTH, GLA_RANK, GLA_QK), GLA_RANK ** -0.5),
        "b_gk_fwd": nrm(ks[4], (DEPTH, GLA_QK), 0.1),
        "w_gk_bwd": nrm(ks[5], (DEPTH, GLA_RANK, GLA_QK), GLA_RANK ** -0.5),
        "b_gk_bwd": nrm(ks[6], (DEPTH, GLA_QK), 0.1),
        "gla_norm": gain(ks[7], (DEPTH, GLA_DV)),
        "q_norm": gain(ks[8], (DEPTH, ATT_HD)),
        "k_norm": gain(ks[9], (DEPTH, ATT_HD)),
        "b_gate": nrm(ks[10], (DEPTH, 2, D_MODEL), 0.1),
        "w_o": nrm(ks[11], (DEPTH, D_MODEL, D_MODEL), D_MODEL ** -0.5),
        "ffn_norm": gain(ks[12], (DEPTH, D_MODEL)),
        "w_gate_up": nrm(ks[13], (DEPTH, D_MODEL, 2 * D_FF), D_MODEL ** -0.5),
        "w_down": nrm(ks[14], (DEPTH, D_FF, D_MODEL), D_FF ** -0.5),
        "final_norm": gain(ks[15], (D_MODEL,)),
    }


def reference(x, mix_norm, w_in, w_gk_fwd, b_gk_fwd, w_gk_bwd, b_gk_bwd, gla_norm,
              q_norm, k_norm, b_gate, w_o, ffn_norm, w_gate_up, w_down, final_norm):
    B, S, _ = x.shape
    dt = x.dtype
    rope = axial_rope_tables(S)
    points = split_points()

    def heads(t, n):
        return t.reshape(B, S, n, -1).transpose(0, 2, 1, 3)

    for l in range(DEPTH):
        h = rmsnorm(x, mix_norm[l])
        proj = h @ w_in[l]
        (g_q, g_k, g_v, g_out, g_rf, g_rb,
         a_q, a_k, a_v, gate_a, gate_b) = jnp.split(proj, points, axis=-1)

        qg = heads(g_q, GLA_HEADS) * (GLA_DK ** -0.5)
        kg = heads(g_k, GLA_HEADS)
        vg = heads(g_v, GLA_HEADS)
        la_f = heads(jax.nn.log_sigmoid((g_rf @ w_gk_fwd[l] + b_gk_fwd[l]).astype(jnp.float32)) / GLA_GATE_NORM, GLA_HEADS)
        la_b = heads(jax.nn.log_sigmoid((g_rb @ w_gk_bwd[l] + b_gk_bwd[l]).astype(jnp.float32)) / GLA_GATE_NORM, GLA_HEADS)
        o_f = gla_scan(qg, kg, vg, la_f, strict=False)
        flip = lambda t: jnp.flip(t, axis=2)
        o_b = flip(gla_scan(flip(qg), flip(kg), flip(vg), flip(la_b), strict=True))
        o_a = rmsnorm(o_f + o_b, gla_norm[l])
        y_a = (o_a.transpose(0, 2, 1, 3).reshape(B, S, GLA_V) * jax.nn.silu(g_out.astype(jnp.float32))).astype(dt)

        qa = rmsnorm(a_q.reshape(B, S, ATT_HEADS, ATT_HD), q_norm[l])
        ka = rmsnorm(a_k.reshape(B, S, ATT_KV_HEADS, ATT_HD), k_norm[l])
        qa = apply_axial_rope(qa, rope)
        ka = apply_axial_rope(ka, rope)
        G = ATT_HEADS // ATT_KV_HEADS
        qa = qa.reshape(B, S, ATT_KV_HEADS, G, ATT_HD).transpose(0, 2, 3, 1, 4)
        ka = ka.transpose(0, 2, 1, 3)
        va = heads(a_v, ATT_KV_HEADS).astype(jnp.float32)
        o_att = block_attention(qa, ka, va)
        y_b = o_att.transpose(0, 3, 1, 2, 4).reshape(B, S, ATT_Q).astype(dt)

        sg_a = jax.nn.sigmoid(gate_a + b_gate[l, 0])
        sg_b = jax.nn.sigmoid(gate_b + b_gate[l, 1])
        mixed = sg_a * y_a + sg_b * y_b
        x = x + mixed @ w_o[l]

        h2 = rmsnorm(x, ffn_norm[l])
        gu = h2 @ w_gate_up[l]
        gt, up = jnp.split(gu, 2, axis=-1)
        x = x + (jax.nn.silu(gt) * up) @ w_down[l]

    return rmsnorm(x, final_norm)
```

```python
import functools
import math

import jax
import jax.numpy as jnp
import numpy as np
from jax import lax
from jax.experimental import pallas as pl
from jax.experimental.pallas import tpu as pltpu

F32 = jnp.float32
BF16 = jnp.bfloat16

EPS = 1e-6
GRID_W = 64
ROPE_THETA = 10000.0
GLA_HEADS = 4
GLA_DK = 128
GLA_DV = 256
GLA_RANK = 16
GLA_GATE_NORM = 16.0
ATT_HD = 128
ATT_HEADS = 8
ATT_KV_HEADS = 2
ATT_GROUP = ATT_HEADS // ATT_KV_HEADS

LANES = 128
VMEM_LIMIT = 56 * 1024 * 1024

IN_TM = 256
GLA_C = 128
ATT_TQ = 1024
ATT_KVB = 1024
POST_TM = 256


def _dot(a, b):
    return jnp.dot(a, b, preferred_element_type=F32)


def _dot_nt(a, b):
    return lax.dot_general(a, b, (((1,), (1,)), ((), ())), preferred_element_type=F32)


def _sigmoid(x):
    return 1.0 / (1.0 + jnp.exp(-x))


def _resident(shape):
    nd = len(shape)
    return pl.BlockSpec(shape, lambda *_: (0,) * nd, pipeline_mode=pl.Buffered(1))


C_GQ, C_GK, C_GV, C_GO = 0, 512, 1024, 2048
C_AQ, C_AK, C_AV = 3072, 4096, 4352
C_GA, C_GB, C_R = 4608, 5632, 6656
W_IN_COLS = 6784


def _in_proj_kernel(x_ref, mixg_ref, w_ref, wgk_ref, bgk_ref, qn_ref, kn_ref, bg_ref,
                    rc_ref, rs1_ref, rs2_ref,
                    gq_ref, gk_ref, gv_ref, laf_ref, lab_ref, sgo_ref, sga_ref, sgb_ref,
                    aq_ref, ak_ref, avt_ref):
    x = x_ref[0]
    ms = jnp.mean(x * x, axis=-1, keepdims=True)
    h = (x * lax.rsqrt(ms + EPS) * mixg_ref[...]).astype(BF16)

    def mm(lo, hi):
        return _dot(h, w_ref[:, lo:hi])

    gq = mm(C_GQ, C_GK) * (GLA_DK ** -0.5)
    gk = mm(C_GK, C_GV)
    for hh in range(GLA_HEADS):
        sl = slice(hh * GLA_DK, (hh + 1) * GLA_DK)
        gq_ref[0, hh] = gq[:, sl].astype(BF16)
        gk_ref[0, hh] = gk[:, sl].astype(BF16)
    gv = mm(C_GV, C_GO)
    for hh in range(GLA_HEADS):
        gv_ref[0, hh] = gv[:, hh * GLA_DV:(hh + 1) * GLA_DV].astype(BF16)

    go = mm(C_GO, C_AQ)
    sgo_ref[0] = (go * _sigmoid(go)).astype(BF16)

    r = mm(C_R, W_IN_COLS).astype(BF16)
    z = _dot(r, wgk_ref[...]) + bgk_ref[...]
    la = -(jnp.maximum(-z, 0.0) + jnp.log1p(jnp.exp(-jnp.abs(z)))) * (1.0 / GLA_GATE_NORM)
    for hh in range(GLA_HEADS):
        laf_ref[0, hh] = la[:, hh * GLA_DK:(hh + 1) * GLA_DK]
        lab_ref[0, hh] = la[:, 512 + hh * GLA_DK:512 + (hh + 1) * GLA_DK]

    rc, rs1, rs2 = rc_ref[...], rs1_ref[...], rs2_ref[...]

    def norm_rope(a, g):
        ss = jnp.mean(a * a, axis=-1, keepdims=True)
        a = a * lax.rsqrt(ss + EPS) * g
        return a * rc + pltpu.roll(a, 96, 1) * rs1 + pltpu.roll(a, 32, 1) * rs2

    aq = mm(C_AQ, C_AK)
    qscale = (ATT_HD ** -0.5) * math.log2(math.e)
    for hh in range(ATT_HEADS):
        a = norm_rope(aq[:, hh * ATT_HD:(hh + 1) * ATT_HD], qn_ref[...])
        aq_ref[0, hh] = (a * qscale).astype(BF16)
    akv = mm(C_AK, C_GA)
    for hh in range(ATT_KV_HEADS):
        a = norm_rope(akv[:, hh * ATT_HD:(hh + 1) * ATT_HD], kn_ref[...])
        ak_ref[0, hh] = a.astype(BF16)
        v = akv[:, 256 + hh * ATT_HD:256 + (hh + 1) * ATT_HD]
        avt_ref[0, hh] = v.T.astype(BF16)

    ga = mm(C_GA, C_GB) + bg_ref[0:1, :]
    sga_ref[0] = _sigmoid(ga).astype(BF16)
    gb = mm(C_GB, C_R) + bg_ref[1:2, :]
    sgb_ref[0] = _sigmoid(gb).astype(BF16)


def _in_proj(x, mixg, w, wgk, bgk, qn, kn, bg, rc, rs1, rs2):
    B, S, D = x.shape
    tm = IN_TM
    grid = (B, S // tm)

    def hb(nh, width):
        return pl.BlockSpec((1, nh, tm, width), lambda b, i: (b, 0, i, 0))

    rowb = pl.BlockSpec((1, tm, D), lambda b, i: (b, i, 0))
    tab = pl.BlockSpec((tm, LANES), lambda b, i: (i, 0))
    out_shape = (
        jax.ShapeDtypeStruct((B, GLA_HEADS, S, GLA_DK), BF16),
        jax.ShapeDtypeStruct((B, GLA_HEADS, S, GLA_DK), BF16),
        jax.ShapeDtypeStruct((B, GLA_HEADS, S, GLA_DV), BF16),
        jax.ShapeDtypeStruct((B, GLA_HEADS, S, GLA_DK), F32),
        jax.ShapeDtypeStruct((B, GLA_HEADS, S, GLA_DK), F32),
        jax.ShapeDtypeStruct((B, S, D), BF16),
        jax.ShapeDtypeStruct((B, S, D), BF16),
        jax.ShapeDtypeStruct((B, S, D), BF16),
        jax.ShapeDtypeStruct((B, ATT_HEADS, S, ATT_HD), BF16),
        jax.ShapeDtypeStruct((B, ATT_KV_HEADS, S, ATT_HD), BF16),
        jax.ShapeDtypeStruct((B, ATT_KV_HEADS, ATT_HD, S), BF16),
    )
    out_specs = (
        hb(GLA_HEADS, GLA_DK), hb(GLA_HEADS, GLA_DK), hb(GLA_HEADS, GLA_DV),
        hb(GLA_HEADS, GLA_DK), hb(GLA_HEADS, GLA_DK),
        rowb, rowb, rowb,
        hb(ATT_HEADS, ATT_HD), hb(ATT_KV_HEADS, ATT_HD),
        pl.BlockSpec((1, ATT_KV_HEADS, ATT_HD, tm), lambda b, i: (b, 0, 0, i)),
    )
    in_specs = [
        rowb, _resident(mixg.shape), _resident(w.shape), _resident(wgk.shape), _resident(bgk.shape),
        _resident(qn.shape), _resident(kn.shape), _resident(bg.shape), tab, tab, tab,
    ]
    return pl.pallas_call(
        _in_proj_kernel, grid=grid, in_specs=in_specs, out_specs=out_specs, out_shape=out_shape,
        compiler_params=pltpu.CompilerParams(
            dimension_semantics=("parallel", "parallel"), vmem_limit_bytes=VMEM_LIMIT),
        name="in_proj",
    )(x, mixg, w, wgk, bgk, qn, kn, bg, rc, rs1, rs2)


def _gla_kernel(q_ref, k_ref, v_ref, laf_ref, lab_ref, sgo_ref, sga_ref, gn_ref, o_ref,
                stf_ref, stb_ref, osum_ref):
    C = GLA_C
    S = q_ref.shape[2]
    n = S // C
    stf_ref[...] = jnp.zeros_like(stf_ref)
    stb_ref[...] = jnp.zeros_like(stb_ref)

    row = lax.broadcasted_iota(jnp.int32, (C, C), 0)
    col = lax.broadcasted_iota(jnp.int32, (C, C), 1)
    tri_f = (row >= col).astype(BF16)
    tri_b = (row <= col).astype(BF16)
    mask_f = row >= col
    mask_b = col > row

    def one_dir(c, la_ref, tri, mask, st_ref, mid_row, tot_row):
        rows = pl.ds(pl.multiple_of(c * C, C), C)
        la = la_ref[0, 0, rows, :]
        hi = la.astype(BF16)
        lo = (la - hi.astype(F32)).astype(BF16)
        dist = _dot(tri, hi) + _dot(tri, lo)
        mid = dist[mid_row:mid_row + 1, :]
        tot = dist[tot_row:tot_row + 1, :]
        q = q_ref[0, 0, rows, :].astype(F32)
        k = k_ref[0, 0, rows, :].astype(F32)
        qd = (q * jnp.exp(dist - mid)).astype(BF16)
        kd = (k * jnp.exp(mid - dist)).astype(BF16)
        qi = (q * jnp.exp(dist)).astype(BF16)
        ks_t = (k * jnp.exp(tot - dist)).T.astype(BF16)
        s = jnp.where(mask, _dot_nt(qd, kd), 0.0).astype(BF16)
        v = v_ref[0, 0, rows, :]
        st = st_ref[...]
        o = _dot(s, v) + _dot(qi, st.astype(BF16))
        e_col = jnp.exp(jnp.broadcast_to(tot, (C, GLA_DK)).T)
        st_ref[...] = st * jnp.concatenate([e_col, e_col], axis=1) + _dot(ks_t, v)
        return rows, o

    def finalize(rows):
        o = osum_ref[rows, :]
        ms = jnp.mean(o * o, axis=-1, keepdims=True)
        y = o * lax.rsqrt(ms + EPS) * gn_ref[...]
        y = y * sgo_ref[0, rows, :].astype(F32) * sga_ref[0, rows, :].astype(F32)
        o_ref[0, rows, :] = y.astype(BF16)

    def first_half(i, carry):
        rf, of = one_dir(i, laf_ref, tri_f, mask_f, stf_ref, C // 2 - 1, C - 1)
        osum_ref[rf, :] = of
        rb, ob = one_dir(n - 1 - i, lab_ref, tri_b, mask_b, stb_ref, C // 2, 0)
        osum_ref[rb, :] = ob
        return carry

    def second_half(i, carry):
        rf, of = one_dir(i, laf_ref, tri_f, mask_f, stf_ref, C // 2 - 1, C - 1)
        osum_ref[rf, :] += of
        finalize(rf)
        rb, ob = one_dir(n - 1 - i, lab_ref, tri_b, mask_b, stb_ref, C // 2, 0)
        osum_ref[rb, :] += ob
        finalize(rb)
        return carry

    lax.fori_loop(0, n // 2, first_half, 0)
    lax.fori_loop(n // 2, n, second_half, 0)


def _gla(gq, gk, gv, laf, lab, sgo, sga, gn):
    B, H, S, dk = gq.shape
    dv = gv.shape[-1]
    hs = lambda w: pl.BlockSpec((1, 1, S, w), lambda b, h: (b, h, 0, 0))
    cs = pl.BlockSpec((1, S, dv), lambda b, h: (b, 0, h))
    return pl.pallas_call(
        _gla_kernel, grid=(B, H),
        in_specs=[hs(dk), hs(dk), hs(dv), hs(dk), hs(dk), cs, cs, _resident(gn.shape)],
        out_specs=cs,
        out_shape=jax.ShapeDtypeStruct((B, S, H * dv), BF16),
        scratch_shapes=[pltpu.VMEM((dk, dv), F32), pltpu.VMEM((dk, dv), F32), pltpu.VMEM((S, dv), F32)],
        compiler_params=pltpu.CompilerParams(
            dimension_semantics=("parallel", "parallel"), vmem_limit_bytes=VMEM_LIMIT),
        name="gla",
    )(gq, gk, gv, laf, lab, sgo, sga, gn)


def _attn_kernel(q_ref, k_ref, vt_ref, o_ref):
    q = q_ref[0]
    tq = q.shape[0]
    skv = k_ref.shape[1]
    m = jnp.full((1, tq), -jnp.inf, F32)
    l = jnp.zeros((1, tq), F32)
    acc = jnp.zeros((ATT_HD, tq), F32)
    for j in range(skv // ATT_KVB):
        kv = slice(j * ATT_KVB, (j + 1) * ATT_KVB)
        s = _dot_nt(k_ref[0, kv, :], q)
        m_new = jnp.maximum(m, jnp.max(s, axis=0, keepdims=True))
        alpha = jnp.exp2(m - m_new)
        p = jnp.exp2(s - m_new)
        l = l * alpha + jnp.sum(p, axis=0, keepdims=True)
        acc = acc * alpha + _dot(vt_ref[0, :, kv], p.astype(BF16))
        m = m_new
    o_ref[0] = (acc / l).T.astype(BF16)


def _attn(q, k, vt):
    G, NQ, hd = q.shape
    S = k.shape[1]
    return pl.pallas_call(
        _attn_kernel, grid=(G, NQ // ATT_TQ),
        in_specs=[pl.BlockSpec((1, ATT_TQ, hd), lambda g, i: (g, i, 0)),
                  pl.BlockSpec((1, S, hd), lambda g, i: (g, 0, 0)),
                  pl.BlockSpec((1, hd, S), lambda g, i: (g, 0, 0))],
        out_specs=pl.BlockSpec((1, ATT_TQ, hd), lambda g, i: (g, i, 0)),
        out_shape=jax.ShapeDtypeStruct((G, NQ, hd), BF16),
        compiler_params=pltpu.CompilerParams(
            dimension_semantics=("parallel", "parallel"), vmem_limit_bytes=VMEM_LIMIT),
        name="attn",
    )(q, k, vt)


def _post_kernel(x_ref, ma_ref, yb_ref, sgb_ref, wo_ref, fng_ref, wgu_ref, wd_ref, fin_ref, o_ref):
    dff = wd_ref.shape[0]
    yb = jnp.concatenate([yb_ref[0, hh] for hh in range(ATT_HEADS)], axis=-1)
    mixed = ma_ref[0].astype(F32) + sgb_ref[0].astype(F32) * yb.astype(F32)
    x1 = x_ref[0] + _dot(mixed.astype(BF16), wo_ref[...])
    ms = jnp.mean(x1 * x1, axis=-1, keepdims=True)
    h2 = (x1 * lax.rsqrt(ms + EPS) * fng_ref[...]).astype(BF16)
    gt = _dot(h2, wgu_ref[:, :dff])
    up = _dot(h2, wgu_ref[:, dff:])
    act = (gt * _sigmoid(gt) * up).astype(BF16)
    x2 = x1 + _dot(act, wd_ref[...])
    ms2 = jnp.mean(x2 * x2, axis=-1, keepdims=True)
    o_ref[0] = x2 * lax.rsqrt(ms2 + EPS) * fin_ref[...]


def _post(x, ma, yb, sgb, wo, fng, wgu, wd, fin):
    B, S, D = x.shape
    tm = POST_TM
    rowb = pl.BlockSpec((1, tm, D), lambda b, i: (b, i, 0))
    return pl.pallas_call(
        _post_kernel, grid=(B, S // tm),
        in_specs=[rowb, rowb,
                  pl.BlockSpec((1, ATT_HEADS, tm, ATT_HD), lambda b, i: (b, 0, i, 0)),
                  rowb, _resident(wo.shape), _resident(fng.shape), _resident(wgu.shape),
                  _resident(wd.shape), _resident(fin.shape)],
        out_specs=rowb,
        out_shape=jax.ShapeDtypeStruct((B, S, D), F32),
        compiler_params=pltpu.CompilerParams(
            dimension_semantics=("parallel", "parallel"), vmem_limit_bytes=VMEM_LIMIT),
        name="post",
    )(x, ma, yb, sgb, wo, fng, wgu, wd, fin)


def _rope_tables(seq_len):
    pos = np.arange(seq_len)
    freqs = ROPE_THETA ** (-np.arange(0, 64, 2, dtype=np.float32) / 64.0)
    ang_r = (pos // GRID_W).astype(np.float32)[:, None] * freqs[None, :]
    ang_c = (pos % GRID_W).astype(np.float32)[:, None] * freqs[None, :]
    zero = np.zeros_like(ang_r)
    rc = np.concatenate([np.cos(ang_r), np.cos(ang_r), np.cos(ang_c), np.cos(ang_c)], axis=1)
    rs1 = np.concatenate([-np.sin(ang_r), zero, -np.sin(ang_c), zero], axis=1)
    rs2 = np.concatenate([zero, np.sin(ang_r), zero, np.sin(ang_c)], axis=1)
    return (jnp.asarray(rc, F32), jnp.asarray(rs1, F32), jnp.asarray(rs2, F32))


def kernel(x, mix_norm, w_in, w_gk_fwd, b_gk_fwd, w_gk_bwd, b_gk_bwd, gla_norm, q_norm, k_norm,
           b_gate, w_o, ffn_norm, w_gate_up, w_down, final_norm):
    B, S, D = x.shape
    assert w_in.shape[0] == 1, "single-layer block"
    assert S % max(IN_TM, GLA_C, POST_TM, ATT_KVB) == 0 and (ATT_GROUP * S) % ATT_TQ == 0

    wi = w_in[0]
    pts = np.cumsum([512, 512, 1024, 1024, 16, 16, 1024, 256, 256, 1024, 1024])[:-1].tolist()
    (w_gq, w_gk, w_gv, w_go, w_rf, w_rb, w_aq, w_ak, w_av, w_ga, w_gb) = jnp.split(wi, pts, axis=1)
    pad = jnp.zeros((D, LANES - 2 * GLA_RANK), wi.dtype)
    w = jnp.concatenate([w_gq, w_gk, w_gv, w_go, w_aq, w_ak, w_av, w_ga, w_gb, w_rf, w_rb, pad],
                        axis=1).astype(BF16)
    nqk = GLA_HEADS * GLA_DK
    wgk = jnp.zeros((LANES, 2 * nqk), F32)
    wgk = wgk.at[:GLA_RANK, :nqk].set(w_gk_fwd[0]).at[GLA_RANK:2 * GLA_RANK, nqk:].set(w_gk_bwd[0])
    wgk = wgk.astype(BF16)
    bgk = jnp.concatenate([b_gk_fwd[0], b_gk_bwd[0]])[None, :]
    rc, rs1, rs2 = _rope_tables(S)

    (gq, gk, gv, laf, lab, sgo, sga, sgb, aq, ak, avt) = _in_proj(
        x, mix_norm, w, wgk, bgk, q_norm, k_norm, b_gate[0], rc, rs1, rs2)

    ma = _gla(gq, gk, gv, laf, lab, sgo, sga, gla_norm)

    yb = _attn(aq.reshape(B * ATT_KV_HEADS, ATT_GROUP * S, ATT_HD),
               ak.reshape(B * ATT_KV_HEADS, S, ATT_HD),
               avt.reshape(B * ATT_KV_HEADS, ATT_HD, S))
    yb = yb.reshape(B, ATT_HEADS, S, ATT_HD)

    return _post(x, ma, yb, sgb, w_o[0].astype(BF16), ffn_norm, w_gate_up[0].astype(BF16),
                 w_down[0].astype(BF16), final_norm[None, :])
```

```python
import functools
import math

import jax
import jax.numpy as jnp
import numpy as np
from jax import lax
from jax.experimental import pallas as pl
from jax.experimental.pallas import tpu as pltpu

F32 = jnp.float32
BF16 = jnp.bfloat16

EPS = 1e-6
GRID_W = 64
ROPE_THETA = 10000.0
GLA_HEADS = 4
GLA_DK = 128
GLA_DV = 256
GLA_RANK = 16
GLA_GATE_NORM = 16.0
ATT_HD = 128
ATT_HEADS = 8
ATT_KV_HEADS = 2
ATT_GROUP = ATT_HEADS // ATT_KV_HEADS

LANES = 128
VMEM_LIMIT = 56 * 1024 * 1024

IN_TM = 256
GLA_C = 128
ATT_TQ = 1024
ATT_KVB = 1024
POST_TM = 256


def _dot(a, b):
    return jnp.dot(a, b, preferred_element_type=F32)


def _dot_nt(a, b):
    return lax.dot_general(a, b, (((1,), (1,)), ((), ())), preferred_element_type=F32)


def _sigmoid(x):
    return 1.0 / (1.0 + jnp.exp(-x))


def _resident(shape):
    nd = len(shape)
    return pl.BlockSpec(shape, lambda *_: (0,) * nd, pipeline_mode=pl.Buffered(1))


C_GQ, C_GK, C_GV, C_GO = 0, 512, 1024, 2048
C_AQ, C_AK, C_AV = 3072, 4096, 4352
C_GA, C_GB, C_R = 4608, 5632, 6656
W_IN_COLS = 6784


def _in_proj_kernel(x_ref, mixg_ref, w_ref, wgk_ref, bgk_ref, qn_ref, kn_ref, bg_ref,
                    rc_ref, rs1_ref, rs2_ref,
                    gq_ref, gk_ref, gv_ref, gvt_ref, la_ref, gga_ref, sgb_ref,
                    aq_ref, ak_ref, avt_ref):
    x = x_ref[0]
    ms = jnp.mean(x * x, axis=-1, keepdims=True)
    h = (x * lax.rsqrt(ms + EPS) * mixg_ref[...]).astype(BF16)

    def mm(lo, hi):
        return _dot(h, w_ref[:, lo:hi])

    gq = mm(C_GQ, C_GK) * (GLA_DK ** -0.5)
    gk = mm(C_GK, C_GV)
    for hh in range(GLA_HEADS):
        sl = slice(hh * GLA_DK, (hh + 1) * GLA_DK)
        gq_ref[0, hh] = gq[:, sl].astype(BF16)
        gk_ref[0, hh] = gk[:, sl].astype(BF16)
    gv = mm(C_GV, C_GO)
    for hh in range(GLA_HEADS):
        vh = gv[:, hh * GLA_DV:(hh + 1) * GLA_DV]
        gv_ref[0, hh] = vh.astype(BF16)
        for j in range(vh.shape[0] // GLA_C):
            gvt_ref[0, hh, j] = vh[j * GLA_C:(j + 1) * GLA_C].T.astype(BF16)

    go = mm(C_GO, C_AQ)
    ga = mm(C_GA, C_GB) + bg_ref[0:1, :]
    gga_ref[0] = (go * _sigmoid(go) * _sigmoid(ga)).astype(BF16)

    r = mm(C_R, W_IN_COLS).astype(BF16)
    z = _dot(r, wgk_ref[...]) + bgk_ref[...]
    la = -(jnp.maximum(-z, 0.0) + jnp.log1p(jnp.exp(-jnp.abs(z)))) * (1.0 / GLA_GATE_NORM)
    nqk = GLA_HEADS * GLA_DK
    for hh in range(GLA_HEADS):
        la_ref[0, hh] = jnp.concatenate(
            [la[:, hh * GLA_DK:(hh + 1) * GLA_DK], la[:, nqk + hh * GLA_DK:nqk + (hh + 1) * GLA_DK]], axis=1)

    rc, rs1, rs2 = rc_ref[...], rs1_ref[...], rs2_ref[...]

    def norm_rope(a, g):
        ss = jnp.mean(a * a, axis=-1, keepdims=True)
        a = a * lax.rsqrt(ss + EPS) * g
        return a * rc + pltpu.roll(a, 96, 1) * rs1 + pltpu.roll(a, 32, 1) * rs2

    aq = mm(C_AQ, C_AK)
    qscale = (ATT_HD ** -0.5) * math.log2(math.e)
    for hh in range(ATT_HEADS):
        a = norm_rope(aq[:, hh * ATT_HD:(hh + 1) * ATT_HD], qn_ref[...])
        aq_ref[0, hh] = (a * qscale).astype(BF16)
    akv = mm(C_AK, C_GA)
    for hh in range(ATT_KV_HEADS):
        a = norm_rope(akv[:, hh * ATT_HD:(hh + 1) * ATT_HD], kn_ref[...])
        ak_ref[0, hh] = a.astype(BF16)
        v = akv[:, 256 + hh * ATT_HD:256 + (hh + 1) * ATT_HD]
        avt_ref[0, hh] = v.T.astype(BF16)

    gb = mm(C_GB, C_R) + bg_ref[1:2, :]
    sgb_ref[0] = _sigmoid(gb).astype(BF16)


def _in_proj(x, mixg, w, wgk, bgk, qn, kn, bg, rc, rs1, rs2):
    B, S, D = x.shape
    tm = IN_TM
    grid = (B, S // tm)

    def hb(nh, width):
        return pl.BlockSpec((1, nh, tm, width), lambda b, i: (b, 0, i, 0))

    rowb = pl.BlockSpec((1, tm, D), lambda b, i: (b, i, 0))
    tab = pl.BlockSpec((tm, LANES), lambda b, i: (i, 0))
    out_shape = (
        jax.ShapeDtypeStruct((B, GLA_HEADS, S, GLA_DK), BF16),
        jax.ShapeDtypeStruct((B, GLA_HEADS, S, GLA_DK), BF16),
        jax.ShapeDtypeStruct((B, GLA_HEADS, S, GLA_DV), BF16),
        jax.ShapeDtypeStruct((B, GLA_HEADS, S // GLA_C, GLA_DV, GLA_C), BF16),
        jax.ShapeDtypeStruct((B, GLA_HEADS, S, 2 * GLA_DK), F32),
        jax.ShapeDtypeStruct((B, S, D), BF16),
        jax.ShapeDtypeStruct((B, S, D), BF16),
        jax.ShapeDtypeStruct((B, ATT_HEADS, S, ATT_HD), BF16),
        jax.ShapeDtypeStruct((B, ATT_KV_HEADS, S, ATT_HD), BF16),
        jax.ShapeDtypeStruct((B, ATT_KV_HEADS, ATT_HD, S), BF16),
    )
    out_specs = (
        hb(GLA_HEADS, GLA_DK), hb(GLA_HEADS, GLA_DK), hb(GLA_HEADS, GLA_DV),
        pl.BlockSpec((1, GLA_HEADS, tm // GLA_C, GLA_DV, GLA_C), lambda b, i: (b, 0, i, 0, 0)),
        hb(GLA_HEADS, 2 * GLA_DK),
        rowb, rowb,
        hb(ATT_HEADS, ATT_HD), hb(ATT_KV_HEADS, ATT_HD),
        pl.BlockSpec((1, ATT_KV_HEADS, ATT_HD, tm), lambda b, i: (b, 0, 0, i)),
    )
    in_specs = [
        rowb, _resident(mixg.shape), _resident(w.shape), _resident(wgk.shape), _resident(bgk.shape),
        _resident(qn.shape), _resident(kn.shape), _resident(bg.shape), tab, tab, tab,
    ]
    return pl.pallas_call(
        _in_proj_kernel, grid=grid, in_specs=in_specs, out_specs=out_specs, out_shape=out_shape,
        compiler_params=pltpu.CompilerParams(
            dimension_semantics=("parallel", "parallel"), vmem_limit_bytes=VMEM_LIMIT),
        name="in_proj",
    )(x, mixg, w, wgk, bgk, qn, kn, bg, rc, rs1, rs2)


def _gla_kernel(q_ref, k_ref, v_ref, vt_ref, la_ref, gg_ref, gn_ref, o_ref,
                osum_ref, qi_ref, s_ref, ut_ref, st_ref, e_ref, carry_ref):
    C, dk = GLA_C, GLA_DK
    S = q_ref.shape[2]
    n = S // C

    row = lax.broadcasted_iota(jnp.int32, (C, C), 0)
    col = lax.broadcasted_iota(jnp.int32, (C, C), 1)
    tri = (row >= col).astype(BF16)
    fwd_pair = row >= col

    def rows_of(c):
        return pl.ds(pl.multiple_of(c * C, C), C)

    def decay_factors(c, carry):
        rows = rows_of(c)
        la = la_ref[0, 0, rows, :]
        hi = la.astype(BF16)
        lo = (la - hi.astype(F32)).astype(BF16)
        pre = _dot(tri, hi) + _dot(tri, lo)
        tot = pre[C - 1:C, :]
        dist_f = pre[:, :dk]
        dist_b = tot[:, dk:] - pre[:, dk:] + la[:, dk:]
        tot_f, tot_b = tot[:, :dk], tot[:, dk:]
        q = q_ref[0, 0, rows, :].astype(F32)
        k = k_ref[0, 0, rows, :].astype(F32)

        def factors(dist, mid, tt):
            qd = (q * jnp.exp(dist - mid)).astype(BF16)
            kd = (k * jnp.exp(mid - dist)).astype(BF16)
            qi = (q * jnp.exp(dist)).astype(BF16)
            ks = (k * jnp.exp(tt - dist)).astype(BF16)
            return qd, kd, qi, ks

        qd_f, kd_f, qi_f, ks_f = factors(dist_f, dist_f[C // 2 - 1:C // 2, :], tot_f)
        qd_b, kd_b, qi_b, ks_b = factors(dist_b, dist_b[C // 2:C // 2 + 1, :], tot_b)
        st_ref[c, :C, :] = jnp.concatenate([qd_f, qd_b], axis=1)
        st_ref[c, C:, :] = jnp.concatenate([kd_f, kd_b], axis=1)
        o_ref[0, rows, :] = jnp.concatenate([ks_f, ks_b], axis=1)
        qi_ref[rows, :] = jnp.concatenate([qi_f, qi_b], axis=1)
        e_ref[c] = jnp.broadcast_to(jnp.exp(tot), e_ref.shape[1:])
        return carry

    def intra_scores(c, carry):
        s_f = _dot_nt(st_ref[c, :C, :dk], st_ref[c, C:, :dk])
        s_b = _dot_nt(st_ref[c, :C, dk:], st_ref[c, C:, dk:])
        s_ref[rows_of(c), :] = jnp.where(fwd_pair, s_f, s_b).astype(BF16)
        return carry

    def intra_out(c, carry):
        rows = rows_of(c)
        osum_ref[rows, :] = _dot(s_ref[rows, :], v_ref[0, 0, rows, :])
        ut_ref[c] = _dot(vt_ref[0, 0, c], o_ref[0, rows, :])
        return carry

    lax.fori_loop(0, n, decay_factors, 0, unroll=4)
    lax.fori_loop(0, n, intra_scores, 0, unroll=4)
    lax.fori_loop(0, n, intra_out, 0, unroll=4)

    carry_ref[...] = jnp.zeros_like(carry_ref)

    def phase_b(i, carry):
        cf, cb = i, n - 1 - i
        st_f = carry_ref[:, :dk]
        st_b = carry_ref[:, dk:]
        st_ref[cf, :, :dk] = st_f.astype(BF16)
        st_ref[cb, :, dk:] = st_b.astype(BF16)
        carry_ref[:, :dk] = st_f * e_ref[cf, 0:1, :dk] + ut_ref[cf, :, :dk]
        carry_ref[:, dk:] = st_b * e_ref[cb, 0:1, dk:] + ut_ref[cb, :, dk:]
        return carry

    lax.fori_loop(0, n, phase_b, 0, unroll=2)

    def phase_c(c, carry):
        rows = rows_of(c)
        o = osum_ref[rows, :] + _dot_nt(qi_ref[rows, :], st_ref[c])
        ms = jnp.mean(o * o, axis=-1, keepdims=True)
        y = o * lax.rsqrt(ms + EPS) * gn_ref[...] * gg_ref[0, rows, :].astype(F32)
        o_ref[0, rows, :] = y.astype(BF16)
        return carry

    lax.fori_loop(0, n, phase_c, 0, unroll=4)


def _gla(gq, gk, gv, gvt, la, gg, gn):
    B, H, S, dk = gq.shape
    dv = gv.shape[-1]
    n = S // GLA_C
    assert dv == 2 * GLA_C, "the per-chunk state buffer doubles as the [qd; kd] staging buffer"
    hs = lambda w: pl.BlockSpec((1, 1, S, w), lambda b, h: (b, h, 0, 0))
    cs = pl.BlockSpec((1, S, dv), lambda b, h: (b, 0, h))
    return pl.pallas_call(
        _gla_kernel, grid=(B, H),
        in_specs=[hs(dk), hs(dk), hs(dv), pl.BlockSpec((1, 1, n, dv, GLA_C), lambda b, h: (b, h, 0, 0, 0)),
                  hs(2 * dk), cs, _resident(gn.shape)],
        out_specs=cs,
        out_shape=jax.ShapeDtypeStruct((B, S, H * dv), BF16),
        scratch_shapes=[
            pltpu.VMEM((S, dv), F32),
            pltpu.VMEM((S, 2 * dk), BF16),
            pltpu.VMEM((S, GLA_C), BF16),
            pltpu.VMEM((n, dv, 2 * dk), F32),
            pltpu.VMEM((n, dv, 2 * dk), BF16),
            pltpu.VMEM((n, 8, 2 * dk), F32),
            pltpu.VMEM((dv, 2 * dk), F32),
        ],
        compiler_params=pltpu.CompilerParams(
            dimension_semantics=("parallel", "parallel"), vmem_limit_bytes=VMEM_LIMIT),
        name="gla",
    )(gq, gk, gv, gvt, la, gg, gn)


def _attn_kernel(q_ref, k_ref, vt_ref, o_ref):
    q = q_ref[0]
    tq = q.shape[0]
    skv = k_ref.shape[1]
    m = jnp.full((1, tq), -jnp.inf, F32)
    l = jnp.zeros((1, tq), F32)
    acc = jnp.zeros((ATT_HD, tq), F32)
    nblk = skv // ATT_KVB

    def scores(j):
        return _dot_nt(k_ref[0, j * ATT_KVB:(j + 1) * ATT_KVB, :], q)

    s_next = scores(0)
    for j in range(nblk):
        kv = slice(j * ATT_KVB, (j + 1) * ATT_KVB)
        s = s_next
        if j + 1 < nblk:
            s_next = scores(j + 1)
        m_new = jnp.maximum(m, jnp.max(s, axis=0, keepdims=True))
        alpha = jnp.exp2(m - m_new)
        p = jnp.exp2(s - m_new)
        l = l * alpha + jnp.sum(p, axis=0, keepdims=True)
        acc = acc * alpha + _dot(vt_ref[0, :, kv], p.astype(BF16))
        m = m_new
    o_ref[0] = (acc / l).T.astype(BF16)


def _attn(q, k, vt):
    G, NQ, hd = q.shape
    S = k.shape[1]
    return pl.pallas_call(
        _attn_kernel, grid=(G, NQ // ATT_TQ),
        in_specs=[pl.BlockSpec((1, ATT_TQ, hd), lambda g, i: (g, i, 0)),
                  pl.BlockSpec((1, S, hd), lambda g, i: (g, 0, 0)),
                  pl.BlockSpec((1, hd, S), lambda g, i: (g, 0, 0))],
        out_specs=pl.BlockSpec((1, ATT_TQ, hd), lambda g, i: (g, i, 0)),
        out_shape=jax.ShapeDtypeStruct((G, NQ, hd), BF16),
        compiler_params=pltpu.CompilerParams(
            dimension_semantics=("parallel", "parallel"), vmem_limit_bytes=VMEM_LIMIT),
        name="attn",
    )(q, k, vt)


def _post_kernel(x_ref, ma_ref, yb_ref, sgb_ref, wo_ref, fng_ref, wgu_ref, wd_ref, fin_ref, o_ref):
    dff = wd_ref.shape[0]
    yb = jnp.concatenate([yb_ref[0, hh] for hh in range(ATT_HEADS)], axis=-1)
    mixed = ma_ref[0].astype(F32) + sgb_ref[0].astype(F32) * yb.astype(F32)
    x1 = x_ref[0] + _dot(mixed.astype(BF16), wo_ref[...])
    ms = jnp.mean(x1 * x1, axis=-1, keepdims=True)
    h2 = (x1 * lax.rsqrt(ms + EPS) * fng_ref[...]).astype(BF16)
    gt = _dot(h2, wgu_ref[:, :dff])
    up = _dot(h2, wgu_ref[:, dff:])
    act = (gt * _sigmoid(gt) * up).astype(BF16)
    x2 = x1 + _dot(act, wd_ref[...])
    ms2 = jnp.mean(x2 * x2, axis=-1, keepdims=True)
    o_ref[0] = x2 * lax.rsqrt(ms2 + EPS) * fin_ref[...]


def _post(x, ma, yb, sgb, wo, fng, wgu, wd, fin):
    B, S, D = x.shape
    tm = POST_TM
    rowb = pl.BlockSpec((1, tm, D), lambda b, i: (b, i, 0))
    return pl.pallas_call(
        _post_kernel, grid=(B, S // tm),
        in_specs=[rowb, rowb,
                  pl.BlockSpec((1, ATT_HEADS, tm, ATT_HD), lambda b, i: (b, 0, i, 0)),
                  rowb, _resident(wo.shape), _resident(fng.shape), _resident(wgu.shape),
                  _resident(wd.shape), _resident(fin.shape)],
        out_specs=rowb,
        out_shape=jax.ShapeDtypeStruct((B, S, D), F32),
        compiler_params=pltpu.CompilerParams(
            dimension_semantics=("parallel", "parallel"), vmem_limit_bytes=VMEM_LIMIT),
        name="post",
    )(x, ma, yb, sgb, wo, fng, wgu, wd, fin)


def _rope_tables(seq_len):
    pos = np.arange(seq_len)
    freqs = ROPE_THETA ** (-np.arange(0, 64, 2, dtype=np.float32) / 64.0)
    ang_r = (pos // GRID_W).astype(np.float32)[:, None] * freqs[None, :]
    ang_c = (pos % GRID_W).astype(np.float32)[:, None] * freqs[None, :]
    zero = np.zeros_like(ang_r)
    rc = np.concatenate([np.cos(ang_r), np.cos(ang_r), np.cos(ang_c), np.cos(ang_c)], axis=1)
    rs1 = np.concatenate([-np.sin(ang_r), zero, -np.sin(ang_c), zero], axis=1)
    rs2 = np.concatenate([zero, np.sin(ang_r), zero, np.sin(ang_c)], axis=1)
    return (jnp.asarray(rc, F32), jnp.asarray(rs1, F32), jnp.asarray(rs2, F32))


def kernel(x, mix_norm, w_in, w_gk_fwd, b_gk_fwd, w_gk_bwd, b_gk_bwd, gla_norm, q_norm, k_norm,
           b_gate, w_o, ffn_norm, w_gate_up, w_down, final_norm):
    B, S, D = x.shape
    assert w_in.shape[0] == 1, "single-layer block"
    assert S % max(IN_TM, GLA_C, POST_TM, ATT_KVB) == 0 and (ATT_GROUP * S) % ATT_TQ == 0

    wi = w_in[0]
    pts = np.cumsum([512, 512, 1024, 1024, 16, 16, 1024, 256, 256, 1024, 1024])[:-1].tolist()
    (w_gq, w_gk, w_gv, w_go, w_rf, w_rb, w_aq, w_ak, w_av, w_ga, w_gb) = jnp.split(wi, pts, axis=1)
    pad = jnp.zeros((D, LANES - 2 * GLA_RANK), wi.dtype)
    w = jnp.concatenate([w_gq, w_gk, w_gv, w_go, w_aq, w_ak, w_av, w_ga, w_gb, w_rf, w_rb, pad],
                        axis=1).astype(BF16)
    nqk = GLA_HEADS * GLA_DK
    wgk = jnp.zeros((LANES, 2 * nqk), F32)
    wgk = wgk.at[:GLA_RANK, :nqk].set(w_gk_fwd[0]).at[GLA_RANK:2 * GLA_RANK, nqk:].set(w_gk_bwd[0])
    wgk = wgk.astype(BF16)
    bgk = jnp.concatenate([b_gk_fwd[0], b_gk_bwd[0]])[None, :]
    rc, rs1, rs2 = _rope_tables(S)

    (gq, gk, gv, gvt, la, gga, sgb, aq, ak, avt) = _in_proj(
        x, mix_norm, w, wgk, bgk, q_norm, k_norm, b_gate[0], rc, rs1, rs2)

    ma = _gla(gq, gk, gv, gvt, la, gga, gla_norm)

    yb = _attn(aq.reshape(B * ATT_KV_HEADS, ATT_GROUP * S, ATT_HD),
               ak.reshape(B * ATT_KV_HEADS, S, ATT_HD),
               avt.reshape(B * ATT_KV_HEADS, ATT_HD, S))
    yb = yb.reshape(B, ATT_HEADS, S, ATT_HD)

    return _post(x, ma, yb, sgb, w_o[0].astype(BF16), ffn_norm, w_gate_up[0].astype(BF16),
                 w_down[0].astype(BF16), final_norm[None, :])
```

```python
import functools
import math

import jax
import jax.numpy as jnp
import numpy as np
from jax import lax
from jax.experimental import pallas as pl
from jax.experimental.pallas import tpu as pltpu

F32 = jnp.float32
BF16 = jnp.bfloat16

EPS = 1e-6
GRID_W = 64
ROPE_THETA = 10000.0
GLA_HEADS = 4
GLA_DK = 128
GLA_DV = 256
GLA_RANK = 16
GLA_GATE_NORM = 16.0
ATT_HD = 128
ATT_HEADS = 8
ATT_KV_HEADS = 2
ATT_GROUP = ATT_HEADS // ATT_KV_HEADS
BF16_SUBLANES = 16
ATT_VROWS = ATT_HD + BF16_SUBLANES

LANES = 128
VMEM_LIMIT = 56 * 1024 * 1024

IN_TM = 256
GLA_C = 128
ATT_TQ = 1024
ATT_KVB = 1024
POST_TM = 256


def _dot(a, b):
    return jnp.dot(a, b, preferred_element_type=F32)


def _dot_nt(a, b):
    return lax.dot_general(a, b, (((1,), (1,)), ((), ())), preferred_element_type=F32)


def _sigmoid(x):
    return 1.0 / (1.0 + jnp.exp(-x))


def _resident(shape):
    nd = len(shape)
    return pl.BlockSpec(shape, lambda *_: (0,) * nd, pipeline_mode=pl.Buffered(1))


C_GQ, C_GK, C_GV, C_GO = 0, 512, 1024, 2048
C_AQ, C_AK, C_AV = 3072, 4096, 4352
C_GA, C_GB, C_R = 4608, 5632, 6656
W_IN_COLS = 6784


def _in_proj_kernel(x_ref, mixg_ref, w_ref, wgk_ref, bgk_ref, qn_ref, kn_ref, bg_ref,
                    rc_ref, rs1_ref, rs2_ref,
                    gq_ref, gk_ref, gv_ref, gvt_ref, la_ref, gga_ref, sgb_ref,
                    aq_ref, ak_ref, avt_ref):
    x = x_ref[0]
    ms = jnp.mean(x * x, axis=-1, keepdims=True)
    h = (x * lax.rsqrt(ms + EPS) * mixg_ref[...]).astype(BF16)

    def mm(lo, hi):
        return _dot(h, w_ref[:, lo:hi])

    gq = mm(C_GQ, C_GK) * (GLA_DK ** -0.5)
    gk = mm(C_GK, C_GV)
    for hh in range(GLA_HEADS):
        sl = slice(hh * GLA_DK, (hh + 1) * GLA_DK)
        gq_ref[0, hh] = gq[:, sl].astype(BF16)
        gk_ref[0, hh] = gk[:, sl].astype(BF16)
    gv = mm(C_GV, C_GO)
    for hh in range(GLA_HEADS):
        vh = gv[:, hh * GLA_DV:(hh + 1) * GLA_DV]
        gv_ref[0, hh] = vh.astype(BF16)
        for j in range(vh.shape[0] // GLA_C):
            gvt_ref[0, hh, j] = vh[j * GLA_C:(j + 1) * GLA_C].T.astype(BF16)

    go = mm(C_GO, C_AQ)
    ga = mm(C_GA, C_GB) + bg_ref[0:1, :]
    gga_ref[0] = (go * _sigmoid(go) * _sigmoid(ga)).astype(BF16)

    r = mm(C_R, W_IN_COLS).astype(BF16)
    z = _dot(r, wgk_ref[...]) + bgk_ref[...]
    la = -(jnp.maximum(-z, 0.0) + jnp.log1p(jnp.exp(-jnp.abs(z)))) * (1.0 / GLA_GATE_NORM)
    nqk = GLA_HEADS * GLA_DK
    for hh in range(GLA_HEADS):
        la_ref[0, hh] = jnp.concatenate(
            [la[:, hh * GLA_DK:(hh + 1) * GLA_DK], la[:, nqk + hh * GLA_DK:nqk + (hh + 1) * GLA_DK]], axis=1)

    rc, rs1, rs2 = rc_ref[...], rs1_ref[...], rs2_ref[...]

    def norm_rope(a, g):
        ss = jnp.mean(a * a, axis=-1, keepdims=True)
        a = a * lax.rsqrt(ss + EPS) * g
        return a * rc + pltpu.roll(a, 96, 1) * rs1 + pltpu.roll(a, 32, 1) * rs2

    aq = mm(C_AQ, C_AK)
    qscale = (ATT_HD ** -0.5) * math.log2(math.e)
    for hh in range(ATT_HEADS):
        a = norm_rope(aq[:, hh * ATT_HD:(hh + 1) * ATT_HD], qn_ref[...])
        aq_ref[0, hh] = (a * qscale).astype(BF16)
    akv = mm(C_AK, C_GA)
    for hh in range(ATT_KV_HEADS):
        a = norm_rope(akv[:, hh * ATT_HD:(hh + 1) * ATT_HD], kn_ref[...])
        ak_ref[0, hh] = a.astype(BF16)
        v = akv[:, 256 + hh * ATT_HD:256 + (hh + 1) * ATT_HD]
        avt_ref[0, hh, :ATT_HD, :] = v.T.astype(BF16)
        avt_ref[0, hh, ATT_HD:, :] = jnp.ones((BF16_SUBLANES, v.shape[0]), BF16)

    gb = mm(C_GB, C_R) + bg_ref[1:2, :]
    sgb_ref[0] = _sigmoid(gb).astype(BF16)


def _in_proj(x, mixg, w, wgk, bgk, qn, kn, bg, rc, rs1, rs2):
    B, S, D = x.shape
    tm = IN_TM
    grid = (B, S // tm)

    def hb(nh, width):
        return pl.BlockSpec((1, nh, tm, width), lambda b, i: (b, 0, i, 0))

    rowb = pl.BlockSpec((1, tm, D), lambda b, i: (b, i, 0))
    tab = pl.BlockSpec((tm, LANES), lambda b, i: (i, 0))
    out_shape = (
        jax.ShapeDtypeStruct((B, GLA_HEADS, S, GLA_DK), BF16),
        jax.ShapeDtypeStruct((B, GLA_HEADS, S, GLA_DK), BF16),
        jax.ShapeDtypeStruct((B, GLA_HEADS, S, GLA_DV), BF16),
        jax.ShapeDtypeStruct((B, GLA_HEADS, S // GLA_C, GLA_DV, GLA_C), BF16),
        jax.ShapeDtypeStruct((B, GLA_HEADS, S, 2 * GLA_DK), F32),
        jax.ShapeDtypeStruct((B, S, D), BF16),
        jax.ShapeDtypeStruct((B, S, D), BF16),
        jax.ShapeDtypeStruct((B, ATT_HEADS, S, ATT_HD), BF16),
        jax.ShapeDtypeStruct((B, ATT_KV_HEADS, S, ATT_HD), BF16),
        jax.ShapeDtypeStruct((B, ATT_KV_HEADS, ATT_VROWS, S), BF16),
    )
    out_specs = (
        hb(GLA_HEADS, GLA_DK), hb(GLA_HEADS, GLA_DK), hb(GLA_HEADS, GLA_DV),
        pl.BlockSpec((1, GLA_HEADS, tm // GLA_C, GLA_DV, GLA_C), lambda b, i: (b, 0, i, 0, 0)),
        hb(GLA_HEADS, 2 * GLA_DK),
        rowb, rowb,
        hb(ATT_HEADS, ATT_HD), hb(ATT_KV_HEADS, ATT_HD),
        pl.BlockSpec((1, ATT_KV_HEADS, ATT_VROWS, tm), lambda b, i: (b, 0, 0, i)),
    )
    in_specs = [
        rowb, _resident(mixg.shape), _resident(w.shape), _resident(wgk.shape), _resident(bgk.shape),
        _resident(qn.shape), _resident(kn.shape), _resident(bg.shape), tab, tab, tab,
    ]
    return pl.pallas_call(
        _in_proj_kernel, grid=grid, in_specs=in_specs, out_specs=out_specs, out_shape=out_shape,
        compiler_params=pltpu.CompilerParams(
            dimension_semantics=("parallel", "parallel"), vmem_limit_bytes=VMEM_LIMIT),
        name="in_proj",
    )(x, mixg, w, wgk, bgk, qn, kn, bg, rc, rs1, rs2)


def _gla_kernel(q_ref, k_ref, v_ref, vt_ref, la_ref, gg_ref, gn_ref, o_ref,
                osum_ref, qi_ref, s_ref, ut_ref, st_ref, e_ref, carry_ref):
    C, dk = GLA_C, GLA_DK
    S = q_ref.shape[2]
    n = S // C

    row = lax.broadcasted_iota(jnp.int32, (C, C), 0)
    col = lax.broadcasted_iota(jnp.int32, (C, C), 1)
    tri = (row >= col).astype(BF16)
    fwd_pair = row >= col

    def rows_of(c):
        return pl.ds(pl.multiple_of(c * C, C), C)

    def decay_factors(c, carry):
        rows = rows_of(c)
        la = la_ref[0, 0, rows, :]
        hi = la.astype(BF16)
        lo = (la - hi.astype(F32)).astype(BF16)
        pre = _dot(tri, hi) + _dot(tri, lo)
        tot = pre[C - 1:C, :]
        dist_f = pre[:, :dk]
        dist_b = tot[:, dk:] - pre[:, dk:] + la[:, dk:]
        tot_f, tot_b = tot[:, :dk], tot[:, dk:]
        q = q_ref[0, 0, rows, :].astype(F32)
        k = k_ref[0, 0, rows, :].astype(F32)

        def factors(dist, mid, tt):
            qd = (q * jnp.exp(dist - mid)).astype(BF16)
            kd = (k * jnp.exp(mid - dist)).astype(BF16)
            qi = (q * jnp.exp(dist)).astype(BF16)
            ks = (k * jnp.exp(tt - dist)).astype(BF16)
            return qd, kd, qi, ks

        qd_f, kd_f, qi_f, ks_f = factors(dist_f, dist_f[C // 2 - 1:C // 2, :], tot_f)
        qd_b, kd_b, qi_b, ks_b = factors(dist_b, dist_b[C // 2:C // 2 + 1, :], tot_b)
        st_ref[c, :C, :] = jnp.concatenate([qd_f, qd_b], axis=1)
        st_ref[c, C:, :] = jnp.concatenate([kd_f, kd_b], axis=1)
        o_ref[0, rows, :] = jnp.concatenate([ks_f, ks_b], axis=1)
        qi_ref[rows, :] = jnp.concatenate([qi_f, qi_b], axis=1)
        e_ref[c] = jnp.broadcast_to(jnp.exp(tot), e_ref.shape[1:])
        return carry

    def intra_scores(c, carry):
        s_f = _dot_nt(st_ref[c, :C, :dk], st_ref[c, C:, :dk])
        s_b = _dot_nt(st_ref[c, :C, dk:], st_ref[c, C:, dk:])
        s_ref[rows_of(c), :] = jnp.where(fwd_pair, s_f, s_b).astype(BF16)
        return carry

    def intra_out(c, carry):
        rows = rows_of(c)
        osum_ref[rows, :] = _dot(s_ref[rows, :], v_ref[0, 0, rows, :])
        ut_ref[c] = _dot(vt_ref[0, 0, c], o_ref[0, rows, :])
        return carry

    lax.fori_loop(0, n, decay_factors, 0, unroll=4)
    lax.fori_loop(0, n, intra_scores, 0, unroll=4)
    lax.fori_loop(0, n, intra_out, 0, unroll=4)

    carry_ref[...] = jnp.zeros_like(carry_ref)

    def phase_b(i, carry):
        cf, cb = i, n - 1 - i
        st_f = carry_ref[:, :dk]
        st_b = carry_ref[:, dk:]
        st_ref[cf, :, :dk] = st_f.astype(BF16)
        st_ref[cb, :, dk:] = st_b.astype(BF16)
        carry_ref[:, :dk] = st_f * e_ref[cf, 0:1, :dk] + ut_ref[cf, :, :dk]
        carry_ref[:, dk:] = st_b * e_ref[cb, 0:1, dk:] + ut_ref[cb, :, dk:]
        return carry

    lax.fori_loop(0, n, phase_b, 0, unroll=2)

    def phase_c(c, carry):
        rows = rows_of(c)
        o = osum_ref[rows, :] + _dot_nt(qi_ref[rows, :], st_ref[c])
        ms = jnp.mean(o * o, axis=-1, keepdims=True)
        y = o * lax.rsqrt(ms + EPS) * gn_ref[...] * gg_ref[0, rows, :].astype(F32)
        o_ref[0, rows, :] = y.astype(BF16)
        return carry

    lax.fori_loop(0, n, phase_c, 0, unroll=4)


def _gla(gq, gk, gv, gvt, la, gg, gn):
    B, H, S, dk = gq.shape
    dv = gv.shape[-1]
    n = S // GLA_C
    assert dv == 2 * GLA_C, "the per-chunk state buffer doubles as the [qd; kd] staging buffer"
    hs = lambda w: pl.BlockSpec((1, 1, S, w), lambda b, h: (b, h, 0, 0))
    cs = pl.BlockSpec((1, S, dv), lambda b, h: (b, 0, h))
    return pl.pallas_call(
        _gla_kernel, grid=(B, H),
        in_specs=[hs(dk), hs(dk), hs(dv), pl.BlockSpec((1, 1, n, dv, GLA_C), lambda b, h: (b, h, 0, 0, 0)),
                  hs(2 * dk), cs, _resident(gn.shape)],
        out_specs=cs,
        out_shape=jax.ShapeDtypeStruct((B, S, H * dv), BF16),
        scratch_shapes=[
            pltpu.VMEM((S, dv), F32),
            pltpu.VMEM((S, 2 * dk), BF16),
            pltpu.VMEM((S, GLA_C), BF16),
            pltpu.VMEM((n, dv, 2 * dk), F32),
            pltpu.VMEM((n, dv, 2 * dk), BF16),
            pltpu.VMEM((n, 8, 2 * dk), F32),
            pltpu.VMEM((dv, 2 * dk), F32),
        ],
        compiler_params=pltpu.CompilerParams(
            dimension_semantics=("parallel", "parallel"), vmem_limit_bytes=VMEM_LIMIT),
        name="gla",
    )(gq, gk, gv, gvt, la, gg, gn)


def _attn_kernel(q_ref, k_ref, vt_ref, o_ref):
    q = q_ref[0]
    tq = q.shape[0]
    skv = k_ref.shape[1]
    rows_aug = vt_ref.shape[1]
    nblk = skv // ATT_KVB

    def scores(j):
        return _dot_nt(k_ref[0, j * ATT_KVB:(j + 1) * ATT_KVB, :], q)

    def emit(acc):
        o_ref[0] = (acc[:ATT_HD] / acc[ATT_HD:ATT_HD + 1]).T.astype(BF16)

    acc = jnp.zeros((rows_aug, tq), F32)
    s_next = scores(0)
    for j in range(nblk):
        s = s_next
        if j + 1 < nblk:
            s_next = scores(j + 1)
        acc = acc + _dot(vt_ref[0, :, j * ATT_KVB:(j + 1) * ATT_KVB], jnp.exp2(s).astype(BF16))
    emit(acc)
    den = acc[ATT_HD:ATT_HD + 1]
    in_range = jnp.logical_and(jnp.min(den) >= 2.0 ** -100, jnp.max(den) <= 2.0 ** 100)
    in_range = jnp.logical_and(in_range, jnp.max(jnp.abs(acc[:ATT_HD])) < jnp.inf)

    @pl.when(jnp.logical_not(in_range))
    def _():
        m = jnp.full((1, tq), -jnp.inf, F32)
        acc = jnp.zeros((rows_aug, tq), F32)
        s_next = scores(0)
        for j in range(nblk):
            s = s_next
            if j + 1 < nblk:
                s_next = scores(j + 1)
            m_new = jnp.maximum(m, jnp.max(s, axis=0, keepdims=True))
            p = jnp.exp2(s - m_new).astype(BF16)
            acc = acc * jnp.exp2(m - m_new) + _dot(vt_ref[0, :, j * ATT_KVB:(j + 1) * ATT_KVB], p)
            m = m_new
        emit(acc)


def _attn(q, k, vt):
    G, NQ, hd = q.shape
    S = k.shape[1]
    return pl.pallas_call(
        _attn_kernel, grid=(G, NQ // ATT_TQ),
        in_specs=[pl.BlockSpec((1, ATT_TQ, hd), lambda g, i: (g, i, 0)),
                  pl.BlockSpec((1, S, hd), lambda g, i: (g, 0, 0)),
                  pl.BlockSpec((1, vt.shape[1], S), lambda g, i: (g, 0, 0))],
        out_specs=pl.BlockSpec((1, ATT_TQ, hd), lambda g, i: (g, i, 0)),
        out_shape=jax.ShapeDtypeStruct((G, NQ, hd), BF16),
        compiler_params=pltpu.CompilerParams(
            dimension_semantics=("parallel", "parallel"), vmem_limit_bytes=VMEM_LIMIT),
        name="attn",
    )(q, k, vt)


def _post_kernel(x_ref, ma_ref, yb_ref, sgb_ref, wo_ref, fng_ref, wgu_ref, wd_ref, fin_ref, o_ref):
    dff = wd_ref.shape[0]
    yb = jnp.concatenate([yb_ref[0, hh] for hh in range(ATT_HEADS)], axis=-1)
    mixed = ma_ref[0].astype(F32) + sgb_ref[0].astype(F32) * yb.astype(F32)
    x1 = x_ref[0] + _dot(mixed.astype(BF16), wo_ref[...])
    ms = jnp.mean(x1 * x1, axis=-1, keepdims=True)
    h2 = (x1 * lax.rsqrt(ms + EPS) * fng_ref[...]).astype(BF16)
    gt = _dot(h2, wgu_ref[:, :dff])
    up = _dot(h2, wgu_ref[:, dff:])
    act = (gt * _sigmoid(gt) * up).astype(BF16)
    x2 = x1 + _dot(act, wd_ref[...])
    ms2 = jnp.mean(x2 * x2, axis=-1, keepdims=True)
    o_ref[0] = x2 * lax.rsqrt(ms2 + EPS) * fin_ref[...]


def _post(x, ma, yb, sgb, wo, fng, wgu, wd, fin):
    B, S, D = x.shape
    tm = POST_TM
    rowb = pl.BlockSpec((1, tm, D), lambda b, i: (b, i, 0))
    return pl.pallas_call(
        _post_kernel, grid=(B, S // tm),
        in_specs=[rowb, rowb,
                  pl.BlockSpec((1, ATT_HEADS, tm, ATT_HD), lambda b, i: (b, 0, i, 0)),
                  rowb, _resident(wo.shape), _resident(fng.shape), _resident(wgu.shape),
                  _resident(wd.shape), _resident(fin.shape)],
        out_specs=rowb,
        out_shape=jax.ShapeDtypeStruct((B, S, D), F32),
        compiler_params=pltpu.CompilerParams(
            dimension_semantics=("parallel", "parallel"), vmem_limit_bytes=VMEM_LIMIT),
        name="post",
    )(x, ma, yb, sgb, wo, fng, wgu, wd, fin)


def _rope_tables(seq_len):
    pos = np.arange(seq_len)
    freqs = ROPE_THETA ** (-np.arange(0, 64, 2, dtype=np.float32) / 64.0)
    ang_r = (pos // GRID_W).astype(np.float32)[:, None] * freqs[None, :]
    ang_c = (pos % GRID_W).astype(np.float32)[:, None] * freqs[None, :]
    zero = np.zeros_like(ang_r)
    rc = np.concatenate([np.cos(ang_r), np.cos(ang_r), np.cos(ang_c), np.cos(ang_c)], axis=1)
    rs1 = np.concatenate([-np.sin(ang_r), zero, -np.sin(ang_c), zero], axis=1)
    rs2 = np.concatenate([zero, np.sin(ang_r), zero, np.sin(ang_c)], axis=1)
    return (jnp.asarray(rc, F32), jnp.asarray(rs1, F32), jnp.asarray(rs2, F32))


def kernel(x, mix_norm, w_in, w_gk_fwd, b_gk_fwd, w_gk_bwd, b_gk_bwd, gla_norm, q_norm, k_norm,
           b_gate, w_o, ffn_norm, w_gate_up, w_down, final_norm):
    B, S, D = x.shape
    assert w_in.shape[0] == 1, "single-layer block"
    assert S % max(IN_TM, GLA_C, POST_TM, ATT_KVB) == 0 and (ATT_GROUP * S) % ATT_TQ == 0

    wi = w_in[0]
    pts = np.cumsum([512, 512, 1024, 1024, 16, 16, 1024, 256, 256, 1024, 1024])[:-1].tolist()
    (w_gq, w_gk, w_gv, w_go, w_rf, w_rb, w_aq, w_ak, w_av, w_ga, w_gb) = jnp.split(wi, pts, axis=1)
    pad = jnp.zeros((D, LANES - 2 * GLA_RANK), wi.dtype)
    w = jnp.concatenate([w_gq, w_gk, w_gv, w_go, w_aq, w_ak, w_av, w_ga, w_gb, w_rf, w_rb, pad],
                        axis=1).astype(BF16)
    nqk = GLA_HEADS * GLA_DK
    wgk = jnp.zeros((LANES, 2 * nqk), F32)
    wgk = wgk.at[:GLA_RANK, :nqk].set(w_gk_fwd[0]).at[GLA_RANK:2 * GLA_RANK, nqk:].set(w_gk_bwd[0])
    wgk = wgk.astype(BF16)
    bgk = jnp.concatenate([b_gk_fwd[0], b_gk_bwd[0]])[None, :]
    rc, rs1, rs2 = _rope_tables(S)

    (gq, gk, gv, gvt, la, gga, sgb, aq, ak, avt) = _in_proj(
        x, mix_norm, w, wgk, bgk, q_norm, k_norm, b_gate[0], rc, rs1, rs2)

    ma = _gla(gq, gk, gv, gvt, la, gga, gla_norm)

    yb = _attn(aq.reshape(B * ATT_KV_HEADS, ATT_GROUP * S, ATT_HD),
               ak.reshape(B * ATT_KV_HEADS, S, ATT_HD),
               avt.reshape(B * ATT_KV_HEADS, ATT_VROWS, S))
    yb = yb.reshape(B, ATT_HEADS, S, ATT_HD)

    return _post(x, ma, yb, sgb, w_o[0].astype(BF16), ffn_norm, w_gate_up[0].astype(BF16),
                 w_down[0].astype(BF16), final_norm[None, :])
```

```python
import functools
import math

import jax
import jax.numpy as jnp
import numpy as np
from jax import lax
from jax.experimental import pallas as pl
from jax.experimental.pallas import tpu as pltpu

F32 = jnp.float32
BF16 = jnp.bfloat16

EPS = 1e-6
GRID_W = 64
ROPE_THETA = 10000.0
GLA_HEADS = 4
GLA_DK = 128
GLA_DV = 256
GLA_RANK = 16
GLA_GATE_NORM = 16.0
ATT_HD = 128
ATT_HEADS = 8
ATT_KV_HEADS = 2
ATT_GROUP = ATT_HEADS // ATT_KV_HEADS
BF16_SUBLANES = 16
ATT_VROWS = ATT_HD + BF16_SUBLANES

LANES = 128
VMEM_LIMIT = 56 * 1024 * 1024

IN_TM = 512
GLA_C = 128
ATT_TQ = 1024
ATT_KVB = 1024
POST_TM = 512
POST_SUB = 256


def _dot(a, b):
    return jnp.dot(a, b, preferred_element_type=F32)


def _dot_nt(a, b):
    return lax.dot_general(a, b, (((1,), (1,)), ((), ())), preferred_element_type=F32)


def _sigmoid(x):
    return 1.0 / (1.0 + jnp.exp(-x))


def _resident(shape):
    nd = len(shape)
    return pl.BlockSpec(shape, lambda *_: (0,) * nd, pipeline_mode=pl.Buffered(1))


C_GQ, C_GK, C_GV, C_GO = 0, 512, 1024, 2048
C_AQ, C_AK, C_AV = 3072, 4096, 4352
C_GA, C_GB, C_R = 4608, 5632, 6656
W_IN_COLS = 6784


def _in_proj_kernel(x_ref, mixg_ref, w_ref, wgk_ref, bgk_ref, qn_ref, kn_ref, bg_ref,
                    rc_ref, rs1_ref, rs2_ref,
                    gq_ref, gk_ref, gv_ref, gvt_ref, la_ref, gga_ref, sgb_ref,
                    aq_ref, ak_ref, avt_ref):
    x = x_ref[0]
    ms = jnp.mean(x * x, axis=-1, keepdims=True)
    h = (x * lax.rsqrt(ms + EPS) * mixg_ref[...]).astype(BF16)

    def mm(lo, hi):
        return _dot(h, w_ref[:, lo:hi])


    rc, rs1, rs2 = rc_ref[...], rs1_ref[...], rs2_ref[...]

    def rope_tables(g, scale):
        g = jnp.broadcast_to(g, (8, ATT_HD)) * scale
        return rc * g[0:1], rs1 * pltpu.roll(g, 96, 1)[0:1], rs2 * pltpu.roll(g, 32, 1)[0:1]

    def norm_rope(a, tabs):
        r = lax.rsqrt(jnp.mean(a * a, axis=-1, keepdims=True) + EPS)
        return (a * tabs[0] + pltpu.roll(a, 96, 1) * tabs[1] + pltpu.roll(a, 32, 1) * tabs[2]) * r

    aq = mm(C_AQ, C_AK)
    q_tabs = rope_tables(qn_ref[...], (ATT_HD ** -0.5) * math.log2(math.e))
    for hh in range(ATT_HEADS):
        aq_ref[0, hh] = norm_rope(aq[:, hh * ATT_HD:(hh + 1) * ATT_HD], q_tabs).astype(BF16)
    akv = mm(C_AK, C_GA)
    k_tabs = rope_tables(kn_ref[...], 1.0)
    for hh in range(ATT_KV_HEADS):
        ak_ref[0, hh] = norm_rope(akv[:, hh * ATT_HD:(hh + 1) * ATT_HD], k_tabs).astype(BF16)
        v = akv[:, 256 + hh * ATT_HD:256 + (hh + 1) * ATT_HD]
        avt_ref[0, hh, :ATT_HD, :] = v.T.astype(BF16)
        avt_ref[0, hh, ATT_HD:, :] = jnp.ones((BF16_SUBLANES, v.shape[0]), BF16)

    r = mm(C_R, W_IN_COLS).astype(BF16)
    z = _dot(r, wgk_ref[...]) + bgk_ref[...]
    la = (jnp.minimum(z, 0.0) - jnp.log(1.0 + jnp.exp(-jnp.abs(z)))) * (1.0 / GLA_GATE_NORM)
    nqk = GLA_HEADS * GLA_DK
    for hh in range(GLA_HEADS):
        la_ref[0, hh] = jnp.concatenate(
            [la[:, hh * GLA_DK:(hh + 1) * GLA_DK], la[:, nqk + hh * GLA_DK:nqk + (hh + 1) * GLA_DK]], axis=1)

    go = mm(C_GO, C_AQ)
    ga = mm(C_GA, C_GB) + bg_ref[0:1, :]
    gga_ref[0] = (go / ((1.0 + jnp.exp(-go)) * (1.0 + jnp.exp(-ga)))).astype(BF16)

    gb = mm(C_GB, C_R) + bg_ref[1:2, :]
    sgb_ref[0] = _sigmoid(gb).astype(BF16)

    gv = mm(C_GV, C_GO)
    for hh in range(GLA_HEADS):
        vh = gv[:, hh * GLA_DV:(hh + 1) * GLA_DV]
        gv_ref[0, hh] = vh.astype(BF16)
        for j in range(vh.shape[0] // GLA_C):
            gvt_ref[0, hh, j] = vh[j * GLA_C:(j + 1) * GLA_C].T.astype(BF16)
    gq = mm(C_GQ, C_GK) * (GLA_DK ** -0.5)
    gk = mm(C_GK, C_GV)
    for hh in range(GLA_HEADS):
        sl = slice(hh * GLA_DK, (hh + 1) * GLA_DK)
        gq_ref[0, hh] = gq[:, sl].astype(BF16)
        gk_ref[0, hh] = gk[:, sl].astype(BF16)


def _in_proj(x, mixg, w, wgk, bgk, qn, kn, bg, rc, rs1, rs2):
    B, S, D = x.shape
    tm = IN_TM
    grid = (B, S // tm)

    def hb(nh, width):
        return pl.BlockSpec((1, nh, tm, width), lambda b, i: (b, 0, i, 0))

    rowb = pl.BlockSpec((1, tm, D), lambda b, i: (b, i, 0))
    tab = pl.BlockSpec((tm, LANES), lambda b, i: (i, 0))
    out_shape = (
        jax.ShapeDtypeStruct((B, GLA_HEADS, S, GLA_DK), BF16),
        jax.ShapeDtypeStruct((B, GLA_HEADS, S, GLA_DK), BF16),
        jax.ShapeDtypeStruct((B, GLA_HEADS, S, GLA_DV), BF16),
        jax.ShapeDtypeStruct((B, GLA_HEADS, S // GLA_C, GLA_DV, GLA_C), BF16),
        jax.ShapeDtypeStruct((B, GLA_HEADS, S, 2 * GLA_DK), F32),
        jax.ShapeDtypeStruct((B, S, D), BF16),
        jax.ShapeDtypeStruct((B, S, D), BF16),
        jax.ShapeDtypeStruct((B, ATT_HEADS, S, ATT_HD), BF16),
        jax.ShapeDtypeStruct((B, ATT_KV_HEADS, S, ATT_HD), BF16),
        jax.ShapeDtypeStruct((B, ATT_KV_HEADS, ATT_VROWS, S), BF16),
    )
    out_specs = (
        hb(GLA_HEADS, GLA_DK), hb(GLA_HEADS, GLA_DK), hb(GLA_HEADS, GLA_DV),
        pl.BlockSpec((1, GLA_HEADS, tm // GLA_C, GLA_DV, GLA_C), lambda b, i: (b, 0, i, 0, 0)),
        hb(GLA_HEADS, 2 * GLA_DK),
        rowb, rowb,
        hb(ATT_HEADS, ATT_HD), hb(ATT_KV_HEADS, ATT_HD),
        pl.BlockSpec((1, ATT_KV_HEADS, ATT_VROWS, tm), lambda b, i: (b, 0, 0, i)),
    )
    in_specs = [
        rowb, _resident(mixg.shape), _resident(w.shape), _resident(wgk.shape), _resident(bgk.shape),
        _resident(qn.shape), _resident(kn.shape), _resident(bg.shape), tab, tab, tab,
    ]
    return pl.pallas_call(
        _in_proj_kernel, grid=grid, in_specs=in_specs, out_specs=out_specs, out_shape=out_shape,
        compiler_params=pltpu.CompilerParams(
            dimension_semantics=("parallel", "parallel"), vmem_limit_bytes=VMEM_LIMIT),
        name="in_proj",
    )(x, mixg, w, wgk, bgk, qn, kn, bg, rc, rs1, rs2)


def _gla_kernel(q_ref, k_ref, v_ref, vt_ref, la_ref, gg_ref, gn_ref, o_ref,
                osum_ref, qi_ref, s_ref, ut_ref, st_ref, e_ref, carry_ref):
    C, dk = GLA_C, GLA_DK
    S = q_ref.shape[2]
    n = S // C

    row = lax.broadcasted_iota(jnp.int32, (C, C), 0)
    col = lax.broadcasted_iota(jnp.int32, (C, C), 1)
    tri = (row >= col).astype(BF16)
    fwd_pair = row >= col

    def rows_of(c):
        return pl.ds(pl.multiple_of(c * C, C), C)

    def decay_factors(c, carry):
        rows = rows_of(c)
        la = la_ref[0, 0, rows, :]
        hi = la.astype(BF16)
        lo = (la - hi.astype(F32)).astype(BF16)
        pre = _dot(tri, hi) + _dot(tri, lo)
        tot = pre[C - 1:C, :]
        dist_f = pre[:, :dk]
        dist_b = tot[:, dk:] - pre[:, dk:] + la[:, dk:]
        tot_f, tot_b = tot[:, :dk], tot[:, dk:]
        q = q_ref[0, 0, rows, :].astype(F32)
        k = k_ref[0, 0, rows, :].astype(F32)

        def factors(dist, mid, tt):
            qd = (q * jnp.exp(dist - mid)).astype(BF16)
            kd = (k * jnp.exp(mid - dist)).astype(BF16)
            qi = (q * jnp.exp(dist)).astype(BF16)
            ks = (k * jnp.exp(tt - dist)).astype(BF16)
            return qd, kd, qi, ks

        qd_f, kd_f, qi_f, ks_f = factors(dist_f, dist_f[C // 2 - 1:C // 2, :], tot_f)
        qd_b, kd_b, qi_b, ks_b = factors(dist_b, dist_b[C // 2:C // 2 + 1, :], tot_b)
        st_ref[c, :C, :] = jnp.concatenate([qd_f, qd_b], axis=1)
        st_ref[c, C:, :] = jnp.concatenate([kd_f, kd_b], axis=1)
        o_ref[0, rows, :] = jnp.concatenate([ks_f, ks_b], axis=1)
        qi_ref[rows, :] = jnp.concatenate([qi_f, qi_b], axis=1)
        e_ref[c] = jnp.broadcast_to(jnp.exp(tot), e_ref.shape[1:])
        return carry

    def intra_scores(c, carry):
        s_f = _dot_nt(st_ref[c, :C, :dk], st_ref[c, C:, :dk])
        s_b = _dot_nt(st_ref[c, :C, dk:], st_ref[c, C:, dk:])
        s_ref[rows_of(c), :] = jnp.where(fwd_pair, s_f, s_b).astype(BF16)
        return carry

    def intra_out(c, carry):
        rows = rows_of(c)
        osum_ref[rows, :] = _dot(s_ref[rows, :], v_ref[0, 0, rows, :])
        ut_ref[c] = _dot(vt_ref[0, 0, c], o_ref[0, rows, :])
        return carry

    lax.fori_loop(0, n, decay_factors, 0, unroll=4)
    lax.fori_loop(0, n, intra_scores, 0, unroll=4)
    lax.fori_loop(0, n, intra_out, 0, unroll=4)

    carry_ref[...] = jnp.zeros_like(carry_ref)

    def phase_b(i, carry):
        cf, cb = i, n - 1 - i
        st_f = carry_ref[:, :dk]
        st_b = carry_ref[:, dk:]
        st_ref[cf, :, :dk] = st_f.astype(BF16)
        st_ref[cb, :, dk:] = st_b.astype(BF16)
        carry_ref[:, :dk] = st_f * e_ref[cf, 0:1, :dk] + ut_ref[cf, :, :dk]
        carry_ref[:, dk:] = st_b * e_ref[cb, 0:1, dk:] + ut_ref[cb, :, dk:]
        return carry

    lax.fori_loop(0, n, phase_b, 0, unroll=2)

    def phase_c(c, carry):
        rows = rows_of(c)
        o = osum_ref[rows, :] + _dot_nt(qi_ref[rows, :], st_ref[c])
        ms = jnp.mean(o * o, axis=-1, keepdims=True)
        y = o * lax.rsqrt(ms + EPS) * gn_ref[...] * gg_ref[0, rows, :].astype(F32)
        o_ref[0, rows, :] = y.astype(BF16)
        return carry

    lax.fori_loop(0, n, phase_c, 0, unroll=4)


def _gla(gq, gk, gv, gvt, la, gg, gn):
    B, H, S, dk = gq.shape
    dv = gv.shape[-1]
    n = S // GLA_C
    assert dv == 2 * GLA_C, "the per-chunk state buffer doubles as the [qd; kd] staging buffer"
    hs = lambda w: pl.BlockSpec((1, 1, S, w), lambda b, h: (b, h, 0, 0))
    cs = pl.BlockSpec((1, S, dv), lambda b, h: (b, 0, h))
    return pl.pallas_call(
        _gla_kernel, grid=(B, H),
        in_specs=[hs(dk), hs(dk), hs(dv), pl.BlockSpec((1, 1, n, dv, GLA_C), lambda b, h: (b, h, 0, 0, 0)),
                  hs(2 * dk), cs, _resident(gn.shape)],
        out_specs=cs,
        out_shape=jax.ShapeDtypeStruct((B, S, H * dv), BF16),
        scratch_shapes=[
            pltpu.VMEM((S, dv), F32),
            pltpu.VMEM((S, 2 * dk), BF16),
            pltpu.VMEM((S, GLA_C), BF16),
            pltpu.VMEM((n, dv, 2 * dk), F32),
            pltpu.VMEM((n, dv, 2 * dk), BF16),
            pltpu.VMEM((n, 8, 2 * dk), F32),
            pltpu.VMEM((dv, 2 * dk), F32),
        ],
        compiler_params=pltpu.CompilerParams(
            dimension_semantics=("parallel", "parallel"), vmem_limit_bytes=VMEM_LIMIT),
        name="gla",
    )(gq, gk, gv, gvt, la, gg, gn)


def _attn_kernel(q_ref, k_ref, vt_ref, o_ref):
    q = q_ref[0]
    tq = q.shape[0]
    skv = k_ref.shape[1]
    rows_aug = vt_ref.shape[1]
    nblk = skv // ATT_KVB

    def scores(j):
        return _dot_nt(k_ref[0, j * ATT_KVB:(j + 1) * ATT_KVB, :], q)

    def emit(acc):
        o_ref[0] = (acc[:ATT_HD] / acc[ATT_HD:ATT_HD + 1]).T.astype(BF16)

    acc = jnp.zeros((rows_aug, tq), F32)
    s_next = scores(0)
    for j in range(nblk):
        s = s_next
        if j + 1 < nblk:
            s_next = scores(j + 1)
        acc = acc + _dot(vt_ref[0, :, j * ATT_KVB:(j + 1) * ATT_KVB], jnp.exp2(s).astype(BF16))
    emit(acc)
    den = acc[ATT_HD:ATT_HD + 1]
    in_range = jnp.logical_and(jnp.min(den) >= 2.0 ** -100, jnp.max(den) <= 2.0 ** 100)
    in_range = jnp.logical_and(in_range, jnp.max(jnp.abs(acc[:ATT_HD])) < jnp.inf)

    @pl.when(jnp.logical_not(in_range))
    def _():
        m = jnp.full((1, tq), -jnp.inf, F32)
        acc = jnp.zeros((rows_aug, tq), F32)
        s_next = scores(0)
        for j in range(nblk):
            s = s_next
            if j + 1 < nblk:
                s_next = scores(j + 1)
            m_new = jnp.maximum(m, jnp.max(s, axis=0, keepdims=True))
            p = jnp.exp2(s - m_new).astype(BF16)
            acc = acc * jnp.exp2(m - m_new) + _dot(vt_ref[0, :, j * ATT_KVB:(j + 1) * ATT_KVB], p)
            m = m_new
        emit(acc)


def _attn(q, k, vt):
    G, NQ, hd = q.shape
    S = k.shape[1]
    return pl.pallas_call(
        _attn_kernel, grid=(G, NQ // ATT_TQ),
        in_specs=[pl.BlockSpec((1, ATT_TQ, hd), lambda g, i: (g, i, 0)),
                  pl.BlockSpec((1, S, hd), lambda g, i: (g, 0, 0)),
                  pl.BlockSpec((1, vt.shape[1], S), lambda g, i: (g, 0, 0))],
        out_specs=pl.BlockSpec((1, ATT_TQ, hd), lambda g, i: (g, i, 0)),
        out_shape=jax.ShapeDtypeStruct((G, NQ, hd), BF16),
        compiler_params=pltpu.CompilerParams(
            dimension_semantics=("parallel", "parallel"), vmem_limit_bytes=VMEM_LIMIT),
        name="attn",
    )(q, k, vt)


def _post_kernel(x_ref, ma_ref, yb_ref, sgb_ref, wo_ref, fng_ref, wgu_ref, wd_ref, fin_ref, o_ref):
    dff = wd_ref.shape[0]
    tm = x_ref.shape[1]
    groups = [slice(r, r + POST_SUB) for r in range(0, tm, POST_SUB)]

    def rms(v, g_ref):
        return v * lax.rsqrt(jnp.mean(v * v, axis=-1, keepdims=True) + EPS) * g_ref[...]

    x1 = []
    for rs in groups:
        yb = jnp.concatenate([yb_ref[0, hh, rs, :] for hh in range(ATT_HEADS)], axis=-1)
        mixed = ma_ref[0, rs, :].astype(F32) + sgb_ref[0, rs, :].astype(F32) * yb.astype(F32)
        x1.append(x_ref[0, rs, :] + _dot(mixed.astype(BF16), wo_ref[...]))
    act = []
    for v in x1:
        h2 = rms(v, fng_ref).astype(BF16)
        gt = _dot(h2, wgu_ref[:, :dff])
        up = _dot(h2, wgu_ref[:, dff:])
        act.append((gt / (1.0 + jnp.exp(-gt)) * up).astype(BF16))
    for rs, v, a in zip(groups, x1, act):
        o_ref[0, rs, :] = rms(v + _dot(a, wd_ref[...]), fin_ref)


def _post(x, ma, yb, sgb, wo, fng, wgu, wd, fin):
    B, S, D = x.shape
    tm = POST_TM
    rowb = pl.BlockSpec((1, tm, D), lambda b, i: (b, i, 0))
    return pl.pallas_call(
        _post_kernel, grid=(B, S // tm),
        in_specs=[rowb, rowb,
                  pl.BlockSpec((1, ATT_HEADS, tm, ATT_HD), lambda b, i: (b, 0, i, 0)),
                  rowb, _resident(wo.shape), _resident(fng.shape), _resident(wgu.shape),
                  _resident(wd.shape), _resident(fin.shape)],
        out_specs=rowb,
        out_shape=jax.ShapeDtypeStruct((B, S, D), F32),
        compiler_params=pltpu.CompilerParams(
            dimension_semantics=("parallel", "parallel"), vmem_limit_bytes=VMEM_LIMIT),
        name="post",
    )(x, ma, yb, sgb, wo, fng, wgu, wd, fin)


def _rope_tables(seq_len):
    pos = np.arange(seq_len)
    freqs = ROPE_THETA ** (-np.arange(0, 64, 2, dtype=np.float32) / 64.0)
    ang_r = (pos // GRID_W).astype(np.float32)[:, None] * freqs[None, :]
    ang_c = (pos % GRID_W).astype(np.float32)[:, None] * freqs[None, :]
    zero = np.zeros_like(ang_r)
    rc = np.concatenate([np.cos(ang_r), np.cos(ang_r), np.cos(ang_c), np.cos(ang_c)], axis=1)
    rs1 = np.concatenate([-np.sin(ang_r), zero, -np.sin(ang_c), zero], axis=1)
    rs2 = np.concatenate([zero, np.sin(ang_r), zero, np.sin(ang_c)], axis=1)
    return (jnp.asarray(rc, F32), jnp.asarray(rs1, F32), jnp.asarray(rs2, F32))


def kernel(x, mix_norm, w_in, w_gk_fwd, b_gk_fwd, w_gk_bwd, b_gk_bwd, gla_norm, q_norm, k_norm,
           b_gate, w_o, ffn_norm, w_gate_up, w_down, final_norm):
    B, S, D = x.shape
    assert w_in.shape[0] == 1, "single-layer block"
    assert S % max(IN_TM, GLA_C, POST_TM, ATT_KVB) == 0 and (ATT_GROUP * S) % ATT_TQ == 0

    wi = w_in[0]
    pts = np.cumsum([512, 512, 1024, 1024, 16, 16, 1024, 256, 256, 1024, 1024])[:-1].tolist()
    (w_gq, w_gk, w_gv, w_go, w_rf, w_rb, w_aq, w_ak, w_av, w_ga, w_gb) = jnp.split(wi, pts, axis=1)
    pad = jnp.zeros((D, LANES - 2 * GLA_RANK), wi.dtype)
    w = jnp.concatenate([w_gq, w_gk, w_gv, w_go, w_aq, w_ak, w_av, w_ga, w_gb, w_rf, w_rb, pad],
                        axis=1).astype(BF16)
    nqk = GLA_HEADS * GLA_DK
    wgk = jnp.zeros((LANES, 2 * nqk), F32)
    wgk = wgk.at[:GLA_RANK, :nqk].set(w_gk_fwd[0]).at[GLA_RANK:2 * GLA_RANK, nqk:].set(w_gk_bwd[0])
    wgk = wgk.astype(BF16)
    bgk = jnp.concatenate([b_gk_fwd[0], b_gk_bwd[0]])[None, :]
    rc, rs1, rs2 = _rope_tables(S)

    (gq, gk, gv, gvt, la, gga, sgb, aq, ak, avt) = _in_proj(
        x, mix_norm, w, wgk, bgk, q_norm, k_norm, b_gate[0], rc, rs1, rs2)

    ma = _gla(gq, gk, gv, gvt, la, gga, gla_norm)

    yb = _attn(aq.reshape(B * ATT_KV_HEADS, ATT_GROUP * S, ATT_HD),
               ak.reshape(B * ATT_KV_HEADS, S, ATT_HD),
               avt.reshape(B * ATT_KV_HEADS, ATT_VROWS, S))
    yb = yb.reshape(B, ATT_HEADS, S, ATT_HD)

    return _post(x, ma, yb, sgb, w_o[0].astype(BF16), ffn_norm, w_gate_up[0].astype(BF16),
                 w_down[0].astype(BF16), final_norm[None, :])
```

```python
import math

import jax
import jax.numpy as jnp
import numpy as np
from jax import lax
from jax.experimental import pallas as pl
from jax.experimental.pallas import tpu as pltpu

F32 = jnp.float32
BF16 = jnp.bfloat16

EPS = 1e-6
GRID_W = 64
ROPE_THETA = 10000.0
GLA_HEADS = 4
GLA_DK = 128
GLA_DV = 256
GLA_RANK = 16
GLA_GATE_NORM = 16.0
ATT_HD = 128
ATT_HEADS = 8
ATT_KV_HEADS = 2
ATT_GROUP = ATT_HEADS // ATT_KV_HEADS
BF16_SUBLANES = 16
ATT_VROWS = ATT_HD + BF16_SUBLANES

LANES = 128
VMEM_LIMIT = 56 * 1024 * 1024

IN_TM = 512
GLA_C = 128
ATT_TQ = 1024
ATT_KVB = 1024
POST_TM = 512
POST_SUB = 256


def _dot(a, b):
    return jnp.dot(a, b, preferred_element_type=F32)


def _dot_nt(a, b):
    return lax.dot_general(a, b, (((1,), (1,)), ((), ())), preferred_element_type=F32)


def _sigmoid(x):
    return 1.0 / (1.0 + jnp.exp(-x))


def _resident(shape):
    nd = len(shape)
    return pl.BlockSpec(shape, lambda *_: (0,) * nd, pipeline_mode=pl.Buffered(1))


SEG = {
    "gq": ("a", 0, 512), "gk": ("a", 512, 1024), "gv": ("a", 1024, 2048), "go": ("a", 2048, 3072),
    "r": ("r", 0, LANES),
    "aq": ("b", 0, 1024), "akv": ("b", 1024, 1536), "ga": ("b", 1536, 2560), "gb": ("b", 2560, 3584),
}
W_IN_A = 3072


def _in_proj_kernel(x_ref, mixg_ref, wa_ref, wr_ref, wb_ref, wgk_ref, bgk_ref, qn_ref, kn_ref, bg_ref,
                    rc_ref, rs1_ref, rs2_ref,
                    gq_ref, gk_ref, gv_ref, gvt_ref, la_ref, gga_ref, sgb_ref,
                    aq_ref, ak_ref, avt_ref):
    x = x_ref[0]
    ms = jnp.mean(x * x, axis=-1, keepdims=True)
    h = (x * lax.rsqrt(ms + EPS) * mixg_ref[...]).astype(BF16)

    w_refs = {"a": wa_ref, "r": wr_ref, "b": wb_ref}

    def mm(name):
        group, lo, hi = SEG[name]
        return _dot(h, w_refs[group][:, lo:hi])


    rc, rs1, rs2 = rc_ref[...], rs1_ref[...], rs2_ref[...]

    def rope_tables(g, scale):
        g = jnp.broadcast_to(g, (8, ATT_HD)) * scale
        return rc * g[0:1], rs1 * pltpu.roll(g, 96, 1)[0:1], rs2 * pltpu.roll(g, 32, 1)[0:1]

    def norm_rope(a, tabs):
        r = lax.rsqrt(jnp.mean(a * a, axis=-1, keepdims=True) + EPS)
        return (a * tabs[0] + pltpu.roll(a, 96, 1) * tabs[1] + pltpu.roll(a, 32, 1) * tabs[2]) * r

    aq = mm("aq")
    q_tabs = rope_tables(qn_ref[...], (ATT_HD ** -0.5) * math.log2(math.e))
    for hh in range(ATT_HEADS):
        aq_ref[0, hh] = norm_rope(aq[:, hh * ATT_HD:(hh + 1) * ATT_HD], q_tabs).astype(BF16)
    akv = mm("akv")
    k_tabs = rope_tables(kn_ref[...], 1.0)
    for hh in range(ATT_KV_HEADS):
        ak_ref[0, hh] = norm_rope(akv[:, hh * ATT_HD:(hh + 1) * ATT_HD], k_tabs).astype(BF16)
        v = akv[:, 256 + hh * ATT_HD:256 + (hh + 1) * ATT_HD]
        avt_ref[0, hh, :ATT_HD, :] = v.T.astype(BF16)
        avt_ref[0, hh, ATT_HD:, :] = jnp.ones((BF16_SUBLANES, v.shape[0]), BF16)

    r = mm("r").astype(BF16)
    z = _dot(r, wgk_ref[...]) + bgk_ref[...]
    la = (jnp.minimum(z, 0.0) - jnp.log(1.0 + jnp.exp(-jnp.abs(z)))) * (1.0 / GLA_GATE_NORM)
    nqk = GLA_HEADS * GLA_DK
    for hh in range(GLA_HEADS):
        la_ref[0, hh] = jnp.concatenate(
            [la[:, hh * GLA_DK:(hh + 1) * GLA_DK], la[:, nqk + hh * GLA_DK:nqk + (hh + 1) * GLA_DK]], axis=1)

    go = mm("go")
    ga = mm("ga") + bg_ref[0:1, :]
    gga_ref[0] = (go / ((1.0 + jnp.exp(-go)) * (1.0 + jnp.exp(-ga)))).astype(BF16)

    gb = mm("gb") + bg_ref[1:2, :]
    sgb_ref[0] = _sigmoid(gb).astype(BF16)

    gv = mm("gv")
    for hh in range(GLA_HEADS):
        vh = gv[:, hh * GLA_DV:(hh + 1) * GLA_DV]
        gv_ref[0, hh] = vh.astype(BF16)
        for j in range(vh.shape[0] // GLA_C):
            gvt_ref[0, hh, j] = vh[j * GLA_C:(j + 1) * GLA_C].T.astype(BF16)
    gq = mm("gq") * (GLA_DK ** -0.5)
    gk = mm("gk")
    for hh in range(GLA_HEADS):
        sl = slice(hh * GLA_DK, (hh + 1) * GLA_DK)
        gq_ref[0, hh] = gq[:, sl].astype(BF16)
        gk_ref[0, hh] = gk[:, sl].astype(BF16)


def _in_proj(x, mixg, wa, wr, wb, wgk, bgk, qn, kn, bg, rc, rs1, rs2):
    B, S, D = x.shape
    tm = IN_TM
    grid = (B, S // tm)

    def hb(nh, width):
        return pl.BlockSpec((1, nh, tm, width), lambda b, i: (b, 0, i, 0))

    rowb = pl.BlockSpec((1, tm, D), lambda b, i: (b, i, 0))
    tab = pl.BlockSpec((tm, LANES), lambda b, i: (i, 0))
    out_shape = (
        jax.ShapeDtypeStruct((B, GLA_HEADS, S, GLA_DK), BF16),
        jax.ShapeDtypeStruct((B, GLA_HEADS, S, GLA_DK), BF16),
        jax.ShapeDtypeStruct((B, GLA_HEADS, S, GLA_DV), BF16),
        jax.ShapeDtypeStruct((B, GLA_HEADS, S // GLA_C, GLA_DV, GLA_C), BF16),
        jax.ShapeDtypeStruct((B, GLA_HEADS, S, 2 * GLA_DK), F32),
        jax.ShapeDtypeStruct((B, S, D), BF16),
        jax.ShapeDtypeStruct((B, S, D), BF16),
        jax.ShapeDtypeStruct((B, ATT_HEADS, S, ATT_HD), BF16),
        jax.ShapeDtypeStruct((B, ATT_KV_HEADS, S, ATT_HD), BF16),
        jax.ShapeDtypeStruct((B, ATT_KV_HEADS, ATT_VROWS, S), BF16),
    )
    out_specs = (
        hb(GLA_HEADS, GLA_DK), hb(GLA_HEADS, GLA_DK), hb(GLA_HEADS, GLA_DV),
        pl.BlockSpec((1, GLA_HEADS, tm // GLA_C, GLA_DV, GLA_C), lambda b, i: (b, 0, i, 0, 0)),
        hb(GLA_HEADS, 2 * GLA_DK),
        rowb, rowb,
        hb(ATT_HEADS, ATT_HD), hb(ATT_KV_HEADS, ATT_HD),
        pl.BlockSpec((1, ATT_KV_HEADS, ATT_VROWS, tm), lambda b, i: (b, 0, 0, i)),
    )
    in_specs = [
        rowb, _resident(mixg.shape), _resident(wa.shape), _resident(wr.shape), _resident(wb.shape),
        _resident(wgk.shape), _resident(bgk.shape),
        _resident(qn.shape), _resident(kn.shape), _resident(bg.shape), tab, tab, tab,
    ]
    return pl.pallas_call(
        _in_proj_kernel, grid=grid, in_specs=in_specs, out_specs=out_specs, out_shape=out_shape,
        compiler_params=pltpu.CompilerParams(
            dimension_semantics=("parallel", "parallel"), vmem_limit_bytes=VMEM_LIMIT),
        name="in_proj",
    )(x, mixg, wa, wr, wb, wgk, bgk, qn, kn, bg, rc, rs1, rs2)


def _gla_kernel(q_ref, k_ref, v_ref, vt_ref, la_ref, gg_ref, gn_ref, o_ref,
                osum_ref, qi_ref, s_ref, ut_ref, st_ref, e_ref, carry_ref):
    C, dk = GLA_C, GLA_DK
    S = q_ref.shape[2]
    n = S // C

    row = lax.broadcasted_iota(jnp.int32, (C, C), 0)
    col = lax.broadcasted_iota(jnp.int32, (C, C), 1)
    tri = (row >= col).astype(BF16)
    fwd_pair = row >= col

    def rows_of(c):
        return pl.ds(c * C if isinstance(c, int) else pl.multiple_of(c * C, C), C)

    def prefix_sums(c):
        la = la_ref[0, 0, rows_of(c), :]
        hi = la.astype(BF16)
        lo = (la - hi.astype(F32)).astype(BF16)
        return la, _dot(tri, hi) + _dot(tri, lo)

    def decay_factors(c, la, pre):
        rows = rows_of(c)
        tot = pre[C - 1:C, :]
        dist_f = pre[:, :dk]
        dist_b = tot[:, dk:] - pre[:, dk:] + la[:, dk:]
        tot_f, tot_b = tot[:, :dk], tot[:, dk:]
        q = q_ref[0, 0, rows, :].astype(F32)
        k = k_ref[0, 0, rows, :].astype(F32)

        def factors(dist, mid, tt):
            qd = (q * jnp.exp(dist - mid)).astype(BF16)
            kd = (k * jnp.exp(mid - dist)).astype(BF16)
            qi = (q * jnp.exp(dist)).astype(BF16)
            ks = (k * jnp.exp(tt - dist)).astype(BF16)
            return qd, kd, qi, ks

        qd_f, kd_f, qi_f, ks_f = factors(dist_f, dist_f[C // 2 - 1:C // 2, :], tot_f)
        qd_b, kd_b, qi_b, ks_b = factors(dist_b, dist_b[C // 2:C // 2 + 1, :], tot_b)
        st_ref[c, :C, :] = jnp.concatenate([qd_f, qd_b], axis=1)
        st_ref[c, C:, :] = jnp.concatenate([kd_f, kd_b], axis=1)
        o_ref[0, rows, :] = jnp.concatenate([ks_f, ks_b], axis=1)
        qi_ref[rows, :] = jnp.concatenate([qi_f, qi_b], axis=1)
        e_ref[c] = jnp.broadcast_to(jnp.exp(tot), e_ref.shape[1:])

    def intra_scores(c):
        s_f = _dot_nt(st_ref[c, :C, :dk], st_ref[c, C:, :dk])
        s_b = _dot_nt(st_ref[c, :C, dk:], st_ref[c, C:, dk:])
        s_ref[rows_of(c), :] = jnp.where(fwd_pair, s_f, s_b).astype(BF16)

    def intra_out(c):
        rows = rows_of(c)
        osum_ref[rows, :] = _dot(s_ref[rows, :], v_ref[0, 0, rows, :])
        ut_ref[c] = _dot(vt_ref[0, 0, c], o_ref[0, rows, :])

    def phase_a(i, carry):
        la, pre = prefix_sums(i)
        intra_out(i - 2)
        intra_scores(i - 1)
        decay_factors(i, la, pre)
        return carry

    decay_factors(0, *prefix_sums(0))
    la1, pre1 = prefix_sums(1)
    intra_scores(0)
    decay_factors(1, la1, pre1)
    lax.fori_loop(2, n, phase_a, 0, unroll=5)
    intra_out(n - 2)
    intra_scores(n - 1)
    intra_out(n - 1)

    carry_ref[...] = jnp.zeros_like(carry_ref)

    def phase_b(i, carry):
        cf, cb = i, n - 1 - i
        st_f = carry_ref[:, :dk]
        st_b = carry_ref[:, dk:]
        st_ref[cf, :, :dk] = st_f.astype(BF16)
        st_ref[cb, :, dk:] = st_b.astype(BF16)
        carry_ref[:, :dk] = st_f * e_ref[cf, 0:1, :dk] + ut_ref[cf, :, :dk]
        carry_ref[:, dk:] = st_b * e_ref[cb, 0:1, dk:] + ut_ref[cb, :, dk:]
        return carry

    lax.fori_loop(0, n, phase_b, 0, unroll=2)

    def phase_c(c, carry):
        rows = rows_of(c)
        o = osum_ref[rows, :] + _dot_nt(qi_ref[rows, :], st_ref[c])
        ms = jnp.mean(o * o, axis=-1, keepdims=True)
        y = o * lax.rsqrt(ms + EPS) * gn_ref[...] * gg_ref[0, rows, :].astype(F32)
        o_ref[0, rows, :] = y.astype(BF16)
        return carry

    lax.fori_loop(0, n, phase_c, 0, unroll=8)


def _gla(gq, gk, gv, gvt, la, gg, gn):
    B, H, S, dk = gq.shape
    dv = gv.shape[-1]
    n = S // GLA_C
    assert dv == 2 * GLA_C, "the per-chunk state buffer doubles as the [qd; kd] staging buffer"
    hs = lambda w: pl.BlockSpec((1, 1, S, w), lambda b, h: (b, h, 0, 0))
    cs = pl.BlockSpec((1, S, dv), lambda b, h: (b, 0, h))
    return pl.pallas_call(
        _gla_kernel, grid=(B, H),
        in_specs=[hs(dk), hs(dk), hs(dv), pl.BlockSpec((1, 1, n, dv, GLA_C), lambda b, h: (b, h, 0, 0, 0)),
                  hs(2 * dk), cs, _resident(gn.shape)],
        out_specs=cs,
        out_shape=jax.ShapeDtypeStruct((B, S, H * dv), BF16),
        scratch_shapes=[
            pltpu.VMEM((S, dv), F32),
            pltpu.VMEM((S, 2 * dk), BF16),
            pltpu.VMEM((S, GLA_C), BF16),
            pltpu.VMEM((n, dv, 2 * dk), F32),
            pltpu.VMEM((n, dv, 2 * dk), BF16),
            pltpu.VMEM((n, 8, 2 * dk), F32),
            pltpu.VMEM((dv, 2 * dk), F32),
        ],
        compiler_params=pltpu.CompilerParams(
            dimension_semantics=("parallel", "parallel"), vmem_limit_bytes=VMEM_LIMIT),
        name="gla",
    )(gq, gk, gv, gvt, la, gg, gn)


def _attn_kernel(q_ref, k_ref, vt_ref, o_ref):
    q = q_ref[0]
    tq = q.shape[0]
    skv = k_ref.shape[1]
    rows_aug = vt_ref.shape[1]
    nblk = skv // ATT_KVB

    def scores(j):
        return _dot_nt(k_ref[0, j * ATT_KVB:(j + 1) * ATT_KVB, :], q)

    def emit(acc):
        o_ref[0] = (acc[:ATT_HD] / acc[ATT_HD:ATT_HD + 1]).T.astype(BF16)

    acc = jnp.zeros((rows_aug, tq), F32)
    s_next = scores(0)
    for j in range(nblk):
        s = s_next
        if j + 1 < nblk:
            s_next = scores(j + 1)
        acc = acc + _dot(vt_ref[0, :, j * ATT_KVB:(j + 1) * ATT_KVB], jnp.exp2(s).astype(BF16))
    emit(acc)
    den = acc[ATT_HD:ATT_HD + 1]
    in_range = jnp.logical_and(jnp.min(den) >= 2.0 ** -100, jnp.max(den) <= 2.0 ** 100)
    in_range = jnp.logical_and(in_range, jnp.max(jnp.abs(acc[:ATT_HD])) < jnp.inf)

    @pl.when(jnp.logical_not(in_range))
    def _():
        m = jnp.full((1, tq), -jnp.inf, F32)
        acc = jnp.zeros((rows_aug, tq), F32)
        s_next = scores(0)
        for j in range(nblk):
            s = s_next
            if j + 1 < nblk:
                s_next = scores(j + 1)
            m_new = jnp.maximum(m, jnp.max(s, axis=0, keepdims=True))
            p = jnp.exp2(s - m_new).astype(BF16)
            acc = acc * jnp.exp2(m - m_new) + _dot(vt_ref[0, :, j * ATT_KVB:(j + 1) * ATT_KVB], p)
            m = m_new
        emit(acc)


def _attn(q, k, vt):
    G, NQ, hd = q.shape
    S = k.shape[1]
    return pl.pallas_call(
        _attn_kernel, grid=(G, NQ // ATT_TQ),
        in_specs=[pl.BlockSpec((1, ATT_TQ, hd), lambda g, i: (g, i, 0)),
                  pl.BlockSpec((1, S, hd), lambda g, i: (g, 0, 0)),
                  pl.BlockSpec((1, vt.shape[1], S), lambda g, i: (g, 0, 0))],
        out_specs=pl.BlockSpec((1, ATT_TQ, hd), lambda g, i: (g, i, 0)),
        out_shape=jax.ShapeDtypeStruct((G, NQ, hd), BF16),
        compiler_params=pltpu.CompilerParams(
            dimension_semantics=("parallel", "parallel"), vmem_limit_bytes=VMEM_LIMIT),
        name="attn",
    )(q, k, vt)


def _post_kernel(x_ref, ma_ref, yb_ref, sgb_ref, wo_ref, fng_ref, wgu_ref, wd_ref, fin_ref, o_ref):
    dff = wd_ref.shape[0]
    tm = x_ref.shape[1]
    groups = [slice(r, r + POST_SUB) for r in range(0, tm, POST_SUB)]

    def rms(v, g_ref):
        return v * lax.rsqrt(jnp.mean(v * v, axis=-1, keepdims=True) + EPS) * g_ref[...]

    x1 = []
    for rs in groups:
        yb = jnp.concatenate([yb_ref[0, hh, rs, :] for hh in range(ATT_HEADS)], axis=-1)
        mixed = ma_ref[0, rs, :].astype(F32) + sgb_ref[0, rs, :].astype(F32) * yb.astype(F32)
        x1.append(x_ref[0, rs, :] + _dot(mixed.astype(BF16), wo_ref[...]))
    act = []
    for v in x1:
        h2 = rms(v, fng_ref).astype(BF16)
        gt = _dot(h2, wgu_ref[:, :dff])
        up = _dot(h2, wgu_ref[:, dff:])
        act.append((gt / (1.0 + jnp.exp(-gt)) * up).astype(BF16))
    for rs, v, a in zip(groups, x1, act):
        o_ref[0, rs, :] = rms(v + _dot(a, wd_ref[...]), fin_ref)


def _post(x, ma, yb, sgb, wo, fng, wgu, wd, fin):
    B, S, D = x.shape
    tm = POST_TM
    rowb = pl.BlockSpec((1, tm, D), lambda b, i: (b, i, 0))
    return pl.pallas_call(
        _post_kernel, grid=(B, S // tm),
        in_specs=[rowb, rowb,
                  pl.BlockSpec((1, ATT_HEADS, tm, ATT_HD), lambda b, i: (b, 0, i, 0)),
                  rowb, _resident(wo.shape), _resident(fng.shape), _resident(wgu.shape),
                  _resident(wd.shape), _resident(fin.shape)],
        out_specs=rowb,
        out_shape=jax.ShapeDtypeStruct((B, S, D), F32),
        compiler_params=pltpu.CompilerParams(
            dimension_semantics=("parallel", "parallel"), vmem_limit_bytes=VMEM_LIMIT),
        name="post",
    )(x, ma, yb, sgb, wo, fng, wgu, wd, fin)


def _rope_tables(seq_len):
    pos = np.arange(seq_len)
    freqs = ROPE_THETA ** (-np.arange(0, 64, 2, dtype=np.float32) / 64.0)
    ang_r = (pos // GRID_W).astype(np.float32)[:, None] * freqs[None, :]
    ang_c = (pos % GRID_W).astype(np.float32)[:, None] * freqs[None, :]
    zero = np.zeros_like(ang_r)
    rc = np.concatenate([np.cos(ang_r), np.cos(ang_r), np.cos(ang_c), np.cos(ang_c)], axis=1)
    rs1 = np.concatenate([-np.sin(ang_r), zero, -np.sin(ang_c), zero], axis=1)
    rs2 = np.concatenate([zero, np.sin(ang_r), zero, np.sin(ang_c)], axis=1)
    return (jnp.asarray(rc, F32), jnp.asarray(rs1, F32), jnp.asarray(rs2, F32))


def kernel(x, mix_norm, w_in, w_gk_fwd, b_gk_fwd, w_gk_bwd, b_gk_bwd, gla_norm, q_norm, k_norm,
           b_gate, w_o, ffn_norm, w_gate_up, w_down, final_norm):
    B, S, D = x.shape
    assert w_in.shape[0] == 1, "single-layer block"
    assert S % max(IN_TM, GLA_C, POST_TM, ATT_KVB) == 0 and (ATT_GROUP * S) % ATT_TQ == 0

    wi = w_in[0]
    wa = wi[:, :W_IN_A].astype(BF16)
    wr = jnp.pad(wi[:, W_IN_A:W_IN_A + 2 * GLA_RANK], ((0, 0), (0, LANES - 2 * GLA_RANK))).astype(BF16)
    wb = wi[:, W_IN_A + 2 * GLA_RANK:].astype(BF16)
    nqk = GLA_HEADS * GLA_DK
    wgk = jnp.zeros((LANES, 2 * nqk), F32)
    wgk = wgk.at[:GLA_RANK, :nqk].set(w_gk_fwd[0]).at[GLA_RANK:2 * GLA_RANK, nqk:].set(w_gk_bwd[0])
    wgk = wgk.astype(BF16)
    bgk = jnp.concatenate([b_gk_fwd[0], b_gk_bwd[0]])[None, :]
    rc, rs1, rs2 = _rope_tables(S)

    (gq, gk, gv, gvt, la, gga, sgb, aq, ak, avt) = _in_proj(
        x, mix_norm, wa, wr, wb, wgk, bgk, q_norm, k_norm, b_gate[0], rc, rs1, rs2)

    ma = _gla(gq, gk, gv, gvt, la, gga, gla_norm)

    yb = _attn(aq.reshape(B * ATT_KV_HEADS, ATT_GROUP * S, ATT_HD),
               ak.reshape(B * ATT_KV_HEADS, S, ATT_HD),
               avt.reshape(B * ATT_KV_HEADS, ATT_VROWS, S))
    yb = yb.reshape(B, ATT_HEADS, S, ATT_HD)

    return _post(x, ma, yb, sgb, w_o[0].astype(BF16), ffn_norm, w_gate_up[0].astype(BF16),
                 w_down[0].astype(BF16), final_norm[None, :])
```

```python
import math

import jax
import jax.numpy as jnp
import numpy as np
from jax import lax
from jax.experimental import pallas as pl
from jax.experimental.pallas import tpu as pltpu

F32 = jnp.float32
BF16 = jnp.bfloat16

EPS = 1e-6
GRID_W = 64
ROPE_THETA = 10000.0
GLA_HEADS = 4
GLA_DK = 128
GLA_DV = 256
GLA_RANK = 16
GLA_GATE_NORM = 16.0
ATT_HD = 128
ATT_HEADS = 8
ATT_KV_HEADS = 2
ATT_GROUP = ATT_HEADS // ATT_KV_HEADS

LANES = 128
VMEM_LIMIT = 56 * 1024 * 1024

IN_TM = 512
GLA_C = 128
ATT_TQ = 2048
ATT_SUB = 1024
ATT_KVB = 1024
POST_TM = 512
POST_SUB = 256


def _dot(a, b):
    return jnp.dot(a, b, preferred_element_type=F32)


def _dot_nt(a, b):
    return lax.dot_general(a, b, (((1,), (1,)), ((), ())), preferred_element_type=F32)


def _sigmoid(x):
    return 1.0 / (1.0 + jnp.exp(-x))


def _resident(shape):
    nd = len(shape)
    return pl.BlockSpec(shape, lambda *_: (0,) * nd, pipeline_mode=pl.Buffered(1))


SEG = {
    "gq": ("a", 0, 512), "gk": ("a", 512, 1024), "gv": ("a", 1024, 2048), "go": ("a", 2048, 3072),
    "r": ("r", 0, LANES),
    "aq": ("b", 0, 1024), "akv": ("b", 1024, 1536), "ga": ("b", 1536, 2560), "gb": ("b", 2560, 3584),
}
W_IN_A = 3072


def _in_proj_kernel(x_ref, mixg_ref, wa_ref, wr_ref, wb_ref, wgk_ref, bgk_ref, qn_ref, kn_ref, bg_ref,
                    rc_ref, rs1_ref, rs2_ref,
                    gq_ref, gk_ref, gv_ref, gvt_ref, la_ref, gga_ref, sgb_ref,
                    aq_ref, ak_ref, avt_ref):
    x = x_ref[0]
    ms = jnp.mean(x * x, axis=-1, keepdims=True)
    h = (x * lax.rsqrt(ms + EPS) * mixg_ref[...]).astype(BF16)

    w_refs = {"a": wa_ref, "r": wr_ref, "b": wb_ref}

    def mm(name):
        group, lo, hi = SEG[name]
        return _dot(h, w_refs[group][:, lo:hi])


    rc, rs1, rs2 = rc_ref[...], rs1_ref[...], rs2_ref[...]

    def rope_tables(g, scale):
        g = jnp.broadcast_to(g, (8, ATT_HD)) * scale
        return rc * g[0:1], rs1 * pltpu.roll(g, 96, 1)[0:1], rs2 * pltpu.roll(g, 32, 1)[0:1]

    def norm_rope(a, tabs):
        r = lax.rsqrt(jnp.mean(a * a, axis=-1, keepdims=True) + EPS)
        return (a * tabs[0] + pltpu.roll(a, 96, 1) * tabs[1] + pltpu.roll(a, 32, 1) * tabs[2]) * r

    aq = mm("aq")
    q_tabs = rope_tables(qn_ref[...], (ATT_HD ** -0.5) * math.log2(math.e))
    for hh in range(ATT_HEADS):
        aq_ref[0, hh] = norm_rope(aq[:, hh * ATT_HD:(hh + 1) * ATT_HD], q_tabs).astype(BF16)
    akv = mm("akv")
    k_tabs = rope_tables(kn_ref[...], 1.0)
    for hh in range(ATT_KV_HEADS):
        ak_ref[0, hh] = norm_rope(akv[:, hh * ATT_HD:(hh + 1) * ATT_HD], k_tabs).astype(BF16)
        v = akv[:, 256 + hh * ATT_HD:256 + (hh + 1) * ATT_HD]
        avt_ref[0, hh] = v.T.astype(BF16)

    r = mm("r").astype(BF16)
    z = _dot(r, wgk_ref[...]) + bgk_ref[...]
    la = (jnp.minimum(z, 0.0) - jnp.log(1.0 + jnp.exp(-jnp.abs(z)))) * (1.0 / GLA_GATE_NORM)
    nqk = GLA_HEADS * GLA_DK
    for hh in range(GLA_HEADS):
        la_ref[0, hh] = jnp.concatenate(
            [la[:, hh * GLA_DK:(hh + 1) * GLA_DK], la[:, nqk + hh * GLA_DK:nqk + (hh + 1) * GLA_DK]], axis=1)

    go = mm("go")
    ga = mm("ga") + bg_ref[0:1, :]
    gga_ref[0] = (go / ((1.0 + jnp.exp(-go)) * (1.0 + jnp.exp(-ga)))).astype(BF16)

    gb = mm("gb") + bg_ref[1:2, :]
    sgb_ref[0] = _sigmoid(gb).astype(BF16)

    gv = mm("gv")
    for hh in range(GLA_HEADS):
        vh = gv[:, hh * GLA_DV:(hh + 1) * GLA_DV]
        gv_ref[0, hh] = vh.astype(BF16)
        for j in range(vh.shape[0] // GLA_C):
            gvt_ref[0, hh, j] = vh[j * GLA_C:(j + 1) * GLA_C].T.astype(BF16)
    gq = mm("gq") * (GLA_DK ** -0.5)
    gk = mm("gk")
    for hh in range(GLA_HEADS):
        sl = slice(hh * GLA_DK, (hh + 1) * GLA_DK)
        gq_ref[0, hh] = gq[:, sl].astype(BF16)
        gk_ref[0, hh] = gk[:, sl].astype(BF16)


def _in_proj(x, mixg, wa, wr, wb, wgk, bgk, qn, kn, bg, rc, rs1, rs2):
    B, S, D = x.shape
    tm = IN_TM
    grid = (B, S // tm)

    def hb(nh, width):
        return pl.BlockSpec((1, nh, tm, width), lambda b, i: (b, 0, i, 0))

    rowb = pl.BlockSpec((1, tm, D), lambda b, i: (b, i, 0))
    tab = pl.BlockSpec((tm, LANES), lambda b, i: (i, 0))
    out_shape = (
        jax.ShapeDtypeStruct((B, GLA_HEADS, S, GLA_DK), BF16),
        jax.ShapeDtypeStruct((B, GLA_HEADS, S, GLA_DK), BF16),
        jax.ShapeDtypeStruct((B, GLA_HEADS, S, GLA_DV), BF16),
        jax.ShapeDtypeStruct((B, GLA_HEADS, S // GLA_C, GLA_DV, GLA_C), BF16),
        jax.ShapeDtypeStruct((B, GLA_HEADS, S, 2 * GLA_DK), F32),
        jax.ShapeDtypeStruct((B, S, D), BF16),
        jax.ShapeDtypeStruct((B, S, D), BF16),
        jax.ShapeDtypeStruct((B, ATT_HEADS, S, ATT_HD), BF16),
        jax.ShapeDtypeStruct((B, ATT_KV_HEADS, S, ATT_HD), BF16),
        jax.ShapeDtypeStruct((B, ATT_KV_HEADS, ATT_HD, S), BF16),
    )
    out_specs = (
        hb(GLA_HEADS, GLA_DK), hb(GLA_HEADS, GLA_DK), hb(GLA_HEADS, GLA_DV),
        pl.BlockSpec((1, GLA_HEADS, tm // GLA_C, GLA_DV, GLA_C), lambda b, i: (b, 0, i, 0, 0)),
        hb(GLA_HEADS, 2 * GLA_DK),
        rowb, rowb,
        hb(ATT_HEADS, ATT_HD), hb(ATT_KV_HEADS, ATT_HD),
        pl.BlockSpec((1, ATT_KV_HEADS, ATT_HD, tm), lambda b, i: (b, 0, 0, i)),
    )
    in_specs = [
        rowb, _resident(mixg.shape), _resident(wa.shape), _resident(wr.shape), _resident(wb.shape),
        _resident(wgk.shape), _resident(bgk.shape),
        _resident(qn.shape), _resident(kn.shape), _resident(bg.shape), tab, tab, tab,
    ]
    return pl.pallas_call(
        _in_proj_kernel, grid=grid, in_specs=in_specs, out_specs=out_specs, out_shape=out_shape,
        compiler_params=pltpu.CompilerParams(
            dimension_semantics=("parallel", "parallel"), vmem_limit_bytes=VMEM_LIMIT),
        name="in_proj",
    )(x, mixg, wa, wr, wb, wgk, bgk, qn, kn, bg, rc, rs1, rs2)


def _gla_kernel(q_ref, k_ref, v_ref, vt_ref, la_ref, gg_ref, gn_ref, o_ref,
                osum_ref, qi_ref, s_ref, ut_ref, st_ref, e_ref, carry_ref):
    C, dk = GLA_C, GLA_DK
    S = q_ref.shape[2]
    n = S // C

    row = lax.broadcasted_iota(jnp.int32, (C, C), 0)
    col = lax.broadcasted_iota(jnp.int32, (C, C), 1)
    tri = (row >= col).astype(BF16)
    fwd_pair = row >= col

    def rows_of(c):
        return pl.ds(c * C if isinstance(c, int) else pl.multiple_of(c * C, C), C)

    def prefix_sums(c):
        la = la_ref[0, 0, rows_of(c), :]
        hi = la.astype(BF16)
        lo = (la - hi.astype(F32)).astype(BF16)
        return la, _dot(tri, hi) + _dot(tri, lo)

    def decay_factors(c, la, pre):
        rows = rows_of(c)
        tot = pre[C - 1:C, :]
        dist_f = pre[:, :dk]
        dist_b = tot[:, dk:] - pre[:, dk:] + la[:, dk:]
        tot_f, tot_b = tot[:, :dk], tot[:, dk:]
        q = q_ref[0, 0, rows, :].astype(F32)
        k = k_ref[0, 0, rows, :].astype(F32)

        def factors(dist, mid, tt):
            qd = (q * jnp.exp(dist - mid)).astype(BF16)
            kd = (k * jnp.exp(mid - dist)).astype(BF16)
            qi = (q * jnp.exp(dist)).astype(BF16)
            ks = (k * jnp.exp(tt - dist)).astype(BF16)
            return qd, kd, qi, ks

        qd_f, kd_f, qi_f, ks_f = factors(dist_f, dist_f[C // 2 - 1:C // 2, :], tot_f)
        qd_b, kd_b, qi_b, ks_b = factors(dist_b, dist_b[C // 2:C // 2 + 1, :], tot_b)
        st_ref[c, :C, :] = jnp.concatenate([qd_f, qd_b], axis=1)
        st_ref[c, C:, :] = jnp.concatenate([kd_f, kd_b], axis=1)
        o_ref[0, rows, :] = jnp.concatenate([ks_f, ks_b], axis=1)
        qi_ref[rows, :] = jnp.concatenate([qi_f, qi_b], axis=1)
        e_ref[c] = jnp.broadcast_to(jnp.exp(tot), e_ref.shape[1:])

    def intra_scores(c):
        s_f = _dot_nt(st_ref[c, :C, :dk], st_ref[c, C:, :dk])
        s_b = _dot_nt(st_ref[c, :C, dk:], st_ref[c, C:, dk:])
        s_ref[rows_of(c), :] = jnp.where(fwd_pair, s_f, s_b).astype(BF16)

    def intra_out(c):
        rows = rows_of(c)
        osum_ref[rows, :] = _dot(s_ref[rows, :], v_ref[0, 0, rows, :])
        ut_ref[c] = _dot(vt_ref[0, 0, c], o_ref[0, rows, :])

    def phase_a(i, carry):
        la, pre = prefix_sums(i)
        intra_out(i - 2)
        intra_scores(i - 1)
        decay_factors(i, la, pre)
        return carry

    decay_factors(0, *prefix_sums(0))
    la1, pre1 = prefix_sums(1)
    intra_scores(0)
    decay_factors(1, la1, pre1)
    lax.fori_loop(2, n, phase_a, 0, unroll=5)
    intra_out(n - 2)
    intra_scores(n - 1)
    intra_out(n - 1)

    carry_ref[...] = jnp.zeros_like(carry_ref)

    def phase_b(i, carry):
        cf, cb = i, n - 1 - i
        st_f = carry_ref[:, :dk]
        st_b = carry_ref[:, dk:]
        st_ref[cf, :, :dk] = st_f.astype(BF16)
        st_ref[cb, :, dk:] = st_b.astype(BF16)
        carry_ref[:, :dk] = st_f * e_ref[cf, 0:1, :dk] + ut_ref[cf, :, :dk]
        carry_ref[:, dk:] = st_b * e_ref[cb, 0:1, dk:] + ut_ref[cb, :, dk:]
        return carry

    lax.fori_loop(0, n, phase_b, 0, unroll=2)

    def phase_c(c, carry):
        rows = rows_of(c)
        o = osum_ref[rows, :] + _dot_nt(qi_ref[rows, :], st_ref[c])
        ms = jnp.mean(o * o, axis=-1, keepdims=True)
        y = o * lax.rsqrt(ms + EPS) * gn_ref[...] * gg_ref[0, rows, :].astype(F32)
        o_ref[0, rows, :] = y.astype(BF16)
        return carry

    lax.fori_loop(0, n, phase_c, 0, unroll=8)


def _gla(gq, gk, gv, gvt, la, gg, gn):
    B, H, S, dk = gq.shape
    dv = gv.shape[-1]
    n = S // GLA_C
    assert dv == 2 * GLA_C, "the per-chunk state buffer doubles as the [qd; kd] staging buffer"
    hs = lambda w: pl.BlockSpec((1, 1, S, w), lambda b, h: (b, h, 0, 0))
    cs = pl.BlockSpec((1, S, dv), lambda b, h: (b, 0, h))
    return pl.pallas_call(
        _gla_kernel, grid=(B, H),
        in_specs=[hs(dk), hs(dk), hs(dv), pl.BlockSpec((1, 1, n, dv, GLA_C), lambda b, h: (b, h, 0, 0, 0)),
                  hs(2 * dk), cs, _resident(gn.shape)],
        out_specs=cs,
        out_shape=jax.ShapeDtypeStruct((B, S, H * dv), BF16),
        scratch_shapes=[
            pltpu.VMEM((S, dv), F32),
            pltpu.VMEM((S, 2 * dk), BF16),
            pltpu.VMEM((S, GLA_C), BF16),
            pltpu.VMEM((n, dv, 2 * dk), F32),
            pltpu.VMEM((n, dv, 2 * dk), BF16),
            pltpu.VMEM((n, 8, 2 * dk), F32),
            pltpu.VMEM((dv, 2 * dk), F32),
        ],
        compiler_params=pltpu.CompilerParams(
            dimension_semantics=("parallel", "parallel"), vmem_limit_bytes=VMEM_LIMIT),
        name="gla",
    )(gq, gk, gv, gvt, la, gg, gn)


def _attn_kernel(q_ref, k_ref, vt_ref, o_ref):
    skv = k_ref.shape[1]
    nblk = skv // ATT_KVB
    nsub = q_ref.shape[1] // ATT_SUB

    def q_of(t):
        return q_ref[0, t * ATT_SUB:(t + 1) * ATT_SUB, :]

    def scores(q, j):
        return _dot_nt(k_ref[0, j * ATT_KVB:(j + 1) * ATT_KVB, :], q)

    def emit(t, acc, den):
        o_ref[0, t * ATT_SUB:(t + 1) * ATT_SUB, :] = (acc / den).T.astype(BF16)

    in_range = []
    for t in range(nsub):
        q = q_of(t)
        acc = jnp.zeros((ATT_HD, ATT_SUB), F32)
        den = jnp.zeros((1, ATT_SUB), F32)
        s_next = scores(q, 0)
        for j in range(nblk):
            s = s_next
            if j + 1 < nblk:
                s_next = scores(q, j + 1)
            p = jnp.exp2(s)
            den = den + jnp.sum(p, axis=0, keepdims=True)
            acc = acc + _dot(vt_ref[0, :, j * ATT_KVB:(j + 1) * ATT_KVB], p.astype(BF16))
        emit(t, acc, den)
        ok = jnp.logical_and(jnp.min(den) >= 2.0 ** -100, jnp.max(den) <= 2.0 ** 100)
        in_range.append(jnp.logical_and(ok, jnp.max(jnp.abs(acc)) < jnp.inf))

    for t in range(nsub):
        @pl.when(jnp.logical_not(in_range[t]))
        def _():
            q = q_of(t)
            m = jnp.full((1, ATT_SUB), -jnp.inf, F32)
            acc = jnp.zeros((ATT_HD, ATT_SUB), F32)
            den = jnp.zeros((1, ATT_SUB), F32)
            s_next = scores(q, 0)
            for j in range(nblk):
                s = s_next
                if j + 1 < nblk:
                    s_next = scores(q, j + 1)
                m_new = jnp.maximum(m, jnp.max(s, axis=0, keepdims=True))
                alpha = jnp.exp2(m - m_new)
                p = jnp.exp2(s - m_new)
                den = den * alpha + jnp.sum(p, axis=0, keepdims=True)
                acc = acc * alpha + _dot(vt_ref[0, :, j * ATT_KVB:(j + 1) * ATT_KVB], p.astype(BF16))
                m = m_new
            emit(t, acc, den)


def _attn(q, k, vt):
    G, NQ, hd = q.shape
    S = k.shape[1]
    return pl.pallas_call(
        _attn_kernel, grid=(G, NQ // ATT_TQ),
        in_specs=[pl.BlockSpec((1, ATT_TQ, hd), lambda g, i: (g, i, 0)),
                  pl.BlockSpec((1, S, hd), lambda g, i: (g, 0, 0)),
                  pl.BlockSpec((1, vt.shape[1], S), lambda g, i: (g, 0, 0))],
        out_specs=pl.BlockSpec((1, ATT_TQ, hd), lambda g, i: (g, i, 0)),
        out_shape=jax.ShapeDtypeStruct((G, NQ, hd), BF16),
        compiler_params=pltpu.CompilerParams(
            dimension_semantics=("parallel", "parallel"), vmem_limit_bytes=VMEM_LIMIT),
        name="attn",
    )(q, k, vt)


def _post_kernel(x_ref, ma_ref, yb_ref, sgb_ref, wo_ref, fng_ref, wgu_ref, wd_ref, fin_ref, o_ref):
    dff = wd_ref.shape[0]
    tm = x_ref.shape[1]
    groups = [slice(r, r + POST_SUB) for r in range(0, tm, POST_SUB)]

    def rms(v, g_ref):
        return v * lax.rsqrt(jnp.mean(v * v, axis=-1, keepdims=True) + EPS) * g_ref[...]

    x1 = []
    for rs in groups:
        yb = jnp.concatenate([yb_ref[0, hh, rs, :] for hh in range(ATT_HEADS)], axis=-1)
        mixed = ma_ref[0, rs, :].astype(F32) + sgb_ref[0, rs, :].astype(F32) * yb.astype(F32)
        x1.append(x_ref[0, rs, :] + _dot(mixed.astype(BF16), wo_ref[...]))
    act = []
    for v in x1:
        h2 = rms(v, fng_ref).astype(BF16)
        gt = _dot(h2, wgu_ref[:, :dff])
        up = _dot(h2, wgu_ref[:, dff:])
        act.append((gt / (1.0 + jnp.exp(-gt)) * up).astype(BF16))
    for rs, v, a in zip(groups, x1, act):
        o_ref[0, rs, :] = rms(v + _dot(a, wd_ref[...]), fin_ref)


def _post(x, ma, yb, sgb, wo, fng, wgu, wd, fin):
    B, S, D = x.shape
    tm = POST_TM
    rowb = pl.BlockSpec((1, tm, D), lambda b, i: (b, i, 0))
    return pl.pallas_call(
        _post_kernel, grid=(B, S // tm),
        in_specs=[rowb, rowb,
                  pl.BlockSpec((1, ATT_HEADS, tm, ATT_HD), lambda b, i: (b, 0, i, 0)),
                  rowb, _resident(wo.shape), _resident(fng.shape), _resident(wgu.shape),
                  _resident(wd.shape), _resident(fin.shape)],
        out_specs=rowb,
        out_shape=jax.ShapeDtypeStruct((B, S, D), F32),
        compiler_params=pltpu.CompilerParams(
            dimension_semantics=("parallel", "parallel"), vmem_limit_bytes=VMEM_LIMIT),
        name="post",
    )(x, ma, yb, sgb, wo, fng, wgu, wd, fin)


def _rope_tables(seq_len):
    pos = np.arange(seq_len)
    freqs = ROPE_THETA ** (-np.arange(0, 64, 2, dtype=np.float32) / 64.0)
    ang_r = (pos // GRID_W).astype(np.float32)[:, None] * freqs[None, :]
    ang_c = (pos % GRID_W).astype(np.float32)[:, None] * freqs[None, :]
    zero = np.zeros_like(ang_r)
    rc = np.concatenate([np.cos(ang_r), np.cos(ang_r), np.cos(ang_c), np.cos(ang_c)], axis=1)
    rs1 = np.concatenate([-np.sin(ang_r), zero, -np.sin(ang_c), zero], axis=1)
    rs2 = np.concatenate([zero, np.sin(ang_r), zero, np.sin(ang_c)], axis=1)
    return (jnp.asarray(rc, F32), jnp.asarray(rs1, F32), jnp.asarray(rs2, F32))


def kernel(x, mix_norm, w_in, w_gk_fwd, b_gk_fwd, w_gk_bwd, b_gk_bwd, gla_norm, q_norm, k_norm,
           b_gate, w_o, ffn_norm, w_gate_up, w_down, final_norm):
    B, S, D = x.shape
    assert w_in.shape[0] == 1, "single-layer block"
    assert S % max(IN_TM, GLA_C, POST_TM, ATT_KVB) == 0 and (ATT_GROUP * S) % ATT_TQ == 0

    wi = w_in[0]
    wa = wi[:, :W_IN_A].astype(BF16)
    wr = jnp.pad(wi[:, W_IN_A:W_IN_A + 2 * GLA_RANK], ((0, 0), (0, LANES - 2 * GLA_RANK))).astype(BF16)
    wb = wi[:, W_IN_A + 2 * GLA_RANK:].astype(BF16)
    nqk = GLA_HEADS * GLA_DK
    wgk = jnp.zeros((LANES, 2 * nqk), F32)
    wgk = wgk.at[:GLA_RANK, :nqk].set(w_gk_fwd[0]).at[GLA_RANK:2 * GLA_RANK, nqk:].set(w_gk_bwd[0])
    wgk = wgk.astype(BF16)
    bgk = jnp.concatenate([b_gk_fwd[0], b_gk_bwd[0]])[None, :]
    rc, rs1, rs2 = _rope_tables(S)

    (gq, gk, gv, gvt, la, gga, sgb, aq, ak, avt) = _in_proj(
        x, mix_norm, wa, wr, wb, wgk, bgk, q_norm, k_norm, b_gate[0], rc, rs1, rs2)

    ma = _gla(gq, gk, gv, gvt, la, gga, gla_norm)

    yb = _attn(aq.reshape(B * ATT_KV_HEADS, ATT_GROUP * S, ATT_HD),
               ak.reshape(B * ATT_KV_HEADS, S, ATT_HD),
               avt.reshape(B * ATT_KV_HEADS, ATT_HD, S))
    yb = yb.reshape(B, ATT_HEADS, S, ATT_HD)

    return _post(x, ma, yb, sgb, w_o[0].astype(BF16), ffn_norm, w_gate_up[0].astype(BF16),
                 w_down[0].astype(BF16), final_norm[None, :])
```

```python
import math

import jax
import jax.numpy as jnp
import numpy as np
from jax import lax
from jax.experimental import pallas as pl
from jax.experimental.pallas import tpu as pltpu

F32 = jnp.float32
BF16 = jnp.bfloat16

EPS = 1e-6
GRID_W = 64
ROPE_THETA = 10000.0
GLA_HEADS = 4
GLA_DK = 128
GLA_DV = 256
GLA_RANK = 16
GLA_GATE_NORM = 16.0
ATT_HD = 128
ATT_HEADS = 8
ATT_KV_HEADS = 2
ATT_GROUP = ATT_HEADS // ATT_KV_HEADS

LANES = 128
VMEM_LIMIT = 56 * 1024 * 1024

IN_TM = 512
GLA_C = 128
GLA_RING = 3
ATT_TQ = 2048
ATT_SUB = 1024
ATT_KVB = 1024
POST_TM = 512
POST_SUB = 256


def _dot(a, b):
    return jnp.dot(a, b, preferred_element_type=F32)


def _dot_nt(a, b):
    return lax.dot_general(a, b, (((1,), (1,)), ((), ())), preferred_element_type=F32)


def _sigmoid(x):
    return 1.0 / (1.0 + jnp.exp(-x))


def _resident(shape):
    nd = len(shape)
    return pl.BlockSpec(shape, lambda *_: (0,) * nd, pipeline_mode=pl.Buffered(1))


SEG = {
    "gq": ("a", 0, 512), "gk": ("a", 512, 1024), "gv": ("a", 1024, 2048), "go": ("a", 2048, 3072),
    "r": ("r", 0, LANES),
    "aq": ("b", 0, 1024), "akv": ("b", 1024, 1536), "ga": ("b", 1536, 2560), "gb": ("b", 2560, 3584),
}
W_IN_A = 3072
W_SPLIT_ROWS = 128


def _split_w_in_kernel(w_ref, wa_ref, wr_ref, wb_ref):
    n_r = 2 * GLA_RANK
    wa_ref[...] = w_ref[0, :, :W_IN_A].astype(BF16)
    t = w_ref[0, :, W_IN_A:W_IN_A + LANES]
    lane = lax.broadcasted_iota(jnp.int32, t.shape, 1)
    wr_ref[...] = jnp.where(lane < n_r, t, 0.0).astype(BF16)
    wb_ref[...] = w_ref[0, :, W_IN_A + n_r:].astype(BF16)


def _split_w_in(w_in):
    _, D, n = w_in.shape
    nb = n - W_IN_A - 2 * GLA_RANK
    rows = W_SPLIT_ROWS
    row_block = lambda width: pl.BlockSpec((rows, width), lambda i: (i, 0))
    return pl.pallas_call(
        _split_w_in_kernel, grid=(D // rows,),
        in_specs=[pl.BlockSpec((1, rows, n), lambda i: (0, i, 0))],
        out_specs=[row_block(W_IN_A), row_block(LANES), row_block(nb)],
        out_shape=[jax.ShapeDtypeStruct((D, W_IN_A), BF16), jax.ShapeDtypeStruct((D, LANES), BF16),
                   jax.ShapeDtypeStruct((D, nb), BF16)],
        compiler_params=pltpu.CompilerParams(dimension_semantics=("parallel",)),
        name="split_w_in",
    )(w_in)


def _in_proj_kernel(x_ref, mixg_ref, wa_ref, wr_ref, wb_ref, wgk_ref, bgk_ref, qn_ref, kn_ref, bg_ref,
                    rc_ref, rs1_ref, rs2_ref,
                    gq_ref, gk_ref, gv_ref, gvt_ref, la_ref, gga_ref, sgb_ref,
                    aq_ref, ak_ref, avt_ref):
    x = x_ref[0]
    ms = jnp.mean(x * x, axis=-1, keepdims=True)
    h = (x * lax.rsqrt(ms + EPS) * mixg_ref[...]).astype(BF16)

    w_refs = {"a": wa_ref, "r": wr_ref, "b": wb_ref}

    def mm(name):
        group, lo, hi = SEG[name]
        return _dot(h, w_refs[group][:, lo:hi])


    rc, rs1, rs2 = rc_ref[...], rs1_ref[...], rs2_ref[...]

    def rope_tables(g, scale):
        g = jnp.broadcast_to(g, (8, ATT_HD)) * scale
        return rc * g[0:1], rs1 * pltpu.roll(g, 96, 1)[0:1], rs2 * pltpu.roll(g, 32, 1)[0:1]

    def norm_rope(a, tabs):
        r = lax.rsqrt(jnp.mean(a * a, axis=-1, keepdims=True) + EPS)
        return (a * tabs[0] + pltpu.roll(a, 96, 1) * tabs[1] + pltpu.roll(a, 32, 1) * tabs[2]) * r

    aq = mm("aq")
    q_tabs = rope_tables(qn_ref[...], (ATT_HD ** -0.5) * math.log2(math.e))
    for hh in range(ATT_HEADS):
        aq_ref[0, hh] = norm_rope(aq[:, hh * ATT_HD:(hh + 1) * ATT_HD], q_tabs).astype(BF16)
    akv = mm("akv")
    k_tabs = rope_tables(kn_ref[...], 1.0)
    for hh in range(ATT_KV_HEADS):
        ak_ref[0, hh] = norm_rope(akv[:, hh * ATT_HD:(hh + 1) * ATT_HD], k_tabs).astype(BF16)
        v = akv[:, 256 + hh * ATT_HD:256 + (hh + 1) * ATT_HD]
        avt_ref[0, hh] = v.T.astype(BF16)

    r = mm("r").astype(BF16)
    z = _dot(r, wgk_ref[...]) + bgk_ref[...]
    la = (jnp.minimum(z, 0.0) - jnp.log(1.0 + jnp.exp(-jnp.abs(z)))) * (1.0 / GLA_GATE_NORM)
    nqk = GLA_HEADS * GLA_DK
    for hh in range(GLA_HEADS):
        la_ref[0, hh] = jnp.concatenate(
            [la[:, hh * GLA_DK:(hh + 1) * GLA_DK], la[:, nqk + hh * GLA_DK:nqk + (hh + 1) * GLA_DK]], axis=1)

    go = mm("go")
    ga = mm("ga") + bg_ref[0:1, :]
    gga_ref[0] = (go / ((1.0 + jnp.exp(-go)) * (1.0 + jnp.exp(-ga)))).astype(BF16)

    gb = mm("gb") + bg_ref[1:2, :]
    sgb_ref[0] = _sigmoid(gb).astype(BF16)

    gv = mm("gv")
    for hh in range(GLA_HEADS):
        vh = gv[:, hh * GLA_DV:(hh + 1) * GLA_DV]
        gv_ref[0, hh] = vh.astype(BF16)
        for j in range(vh.shape[0] // GLA_C):
            gvt_ref[0, hh, j] = vh[j * GLA_C:(j + 1) * GLA_C].T.astype(BF16)
    gq = mm("gq") * (GLA_DK ** -0.5)
    gk = mm("gk")
    for hh in range(GLA_HEADS):
        sl = slice(hh * GLA_DK, (hh + 1) * GLA_DK)
        gq_ref[0, hh] = gq[:, sl].astype(BF16)
        gk_ref[0, hh] = gk[:, sl].astype(BF16)


def _in_proj(x, mixg, wa, wr, wb, wgk, bgk, qn, kn, bg, rc, rs1, rs2):
    B, S, D = x.shape
    tm = IN_TM
    grid = (B, S // tm)

    def hb(nh, width):
        return pl.BlockSpec((1, nh, tm, width), lambda b, i: (b, 0, i, 0))

    rowb = pl.BlockSpec((1, tm, D), lambda b, i: (b, i, 0))
    tab = pl.BlockSpec((tm, LANES), lambda b, i: (i, 0))
    out_shape = (
        jax.ShapeDtypeStruct((B, GLA_HEADS, S, GLA_DK), BF16),
        jax.ShapeDtypeStruct((B, GLA_HEADS, S, GLA_DK), BF16),
        jax.ShapeDtypeStruct((B, GLA_HEADS, S, GLA_DV), BF16),
        jax.ShapeDtypeStruct((B, GLA_HEADS, S // GLA_C, GLA_DV, GLA_C), BF16),
        jax.ShapeDtypeStruct((B, GLA_HEADS, S, 2 * GLA_DK), F32),
        jax.ShapeDtypeStruct((B, S, D), BF16),
        jax.ShapeDtypeStruct((B, S, D), BF16),
        jax.ShapeDtypeStruct((B, ATT_HEADS, S, ATT_HD), BF16),
        jax.ShapeDtypeStruct((B, ATT_KV_HEADS, S, ATT_HD), BF16),
        jax.ShapeDtypeStruct((B, ATT_KV_HEADS, ATT_HD, S), BF16),
    )
    out_specs = (
        hb(GLA_HEADS, GLA_DK), hb(GLA_HEADS, GLA_DK), hb(GLA_HEADS, GLA_DV),
        pl.BlockSpec((1, GLA_HEADS, tm // GLA_C, GLA_DV, GLA_C), lambda b, i: (b, 0, i, 0, 0)),
        hb(GLA_HEADS, 2 * GLA_DK),
        rowb, rowb,
        hb(ATT_HEADS, ATT_HD), hb(ATT_KV_HEADS, ATT_HD),
        pl.BlockSpec((1, ATT_KV_HEADS, ATT_HD, tm), lambda b, i: (b, 0, 0, i)),
    )
    in_specs = [
        rowb, _resident(mixg.shape), _resident(wa.shape), _resident(wr.shape), _resident(wb.shape),
        _resident(wgk.shape), _resident(bgk.shape),
        _resident(qn.shape), _resident(kn.shape), _resident(bg.shape), tab, tab, tab,
    ]
    return pl.pallas_call(
        _in_proj_kernel, grid=grid, in_specs=in_specs, out_specs=out_specs, out_shape=out_shape,
        compiler_params=pltpu.CompilerParams(
            dimension_semantics=("parallel", "parallel"), vmem_limit_bytes=VMEM_LIMIT),
        name="in_proj",
    )(x, mixg, wa, wr, wb, wgk, bgk, qn, kn, bg, rc, rs1, rs2)


def _gla_kernel(q_ref, k_ref, v_ref, vt_ref, la_ref, gg_ref, gn_ref, o_ref,
                osum_ref, qi_ref, qd_ring, kd_ring, ks_ring, s_ring, ut_ref, st_ref, e_ref, carry_ref):
    C, dk = GLA_C, GLA_DK
    S = q_ref.shape[2]
    n = S // C

    row = lax.broadcasted_iota(jnp.int32, (C, C), 0)
    col = lax.broadcasted_iota(jnp.int32, (C, C), 1)
    tri = (row >= col).astype(BF16)
    fwd_pair = row >= col

    def rows_of(c):
        return pl.ds(c * C if isinstance(c, int) else pl.multiple_of(c * C, C), C)

    def prefix_sums(c):
        la = la_ref[0, 0, rows_of(c), :]
        hi = la.astype(BF16)
        lo = (la - hi.astype(F32)).astype(BF16)
        return la, _dot(tri, hi) + _dot(tri, lo)

    def decay_factors(c, slot, la, pre):
        rows = rows_of(c)
        tot = pre[C - 1:C, :]
        dist_f = pre[:, :dk]
        dist_b = tot[:, dk:] - pre[:, dk:] + la[:, dk:]
        tot_f, tot_b = tot[:, :dk], tot[:, dk:]
        q = q_ref[0, 0, rows, :].astype(F32)
        k = k_ref[0, 0, rows, :].astype(F32)

        def factors(dist, mid, tt):
            qd = (q * jnp.exp(dist - mid)).astype(BF16)
            kd = (k * jnp.exp(mid - dist)).astype(BF16)
            qi = (q * jnp.exp(dist)).astype(BF16)
            ks = (k * jnp.exp(tt - dist)).astype(BF16)
            return qd, kd, qi, ks

        qd_f, kd_f, qi_f, ks_f = factors(dist_f, dist_f[C // 2 - 1:C // 2, :], tot_f)
        qd_b, kd_b, qi_b, ks_b = factors(dist_b, dist_b[C // 2:C // 2 + 1, :], tot_b)
        qd_ring[slot] = jnp.concatenate([qd_f, qd_b], axis=1)
        kd_ring[slot] = jnp.concatenate([kd_f, kd_b], axis=1)
        ks_ring[slot] = jnp.concatenate([ks_f, ks_b], axis=1)
        qi_ref[rows, :] = jnp.concatenate([qi_f, qi_b], axis=1)
        e_ref[c] = jnp.broadcast_to(jnp.exp(tot), e_ref.shape[1:])

    def intra_scores(slot):
        s_f = _dot_nt(qd_ring[slot, :, :dk], kd_ring[slot, :, :dk])
        s_b = _dot_nt(qd_ring[slot, :, dk:], kd_ring[slot, :, dk:])
        s_ring[slot] = jnp.where(fwd_pair, s_f, s_b).astype(BF16)

    def intra_out(c, slot):
        osum_ref[rows_of(c), :] = _dot(s_ring[slot], v_ref[0, 0, rows_of(c), :])
        ut_ref[c] = _dot(vt_ref[0, 0, c], ks_ring[slot])

    ring = qd_ring.shape[0]
    unroll = 2 * ring
    assert (n - 2) % unroll == 0

    def phase_a(it, carry):
        base = 2 + it * unroll
        for u in range(unroll):
            i = base + u
            la, pre = prefix_sums(i)
            intra_out(i - 2, u % ring)
            intra_scores((u + 1) % ring)
            decay_factors(i, (u + 2) % ring, la, pre)
        return carry

    decay_factors(0, 0, *prefix_sums(0))
    la1, pre1 = prefix_sums(1)
    intra_scores(0)
    decay_factors(1, 1, la1, pre1)
    lax.fori_loop(0, (n - 2) // unroll, phase_a, 0)
    intra_out(n - 2, (n - 2) % ring)
    intra_scores((n - 1) % ring)
    intra_out(n - 1, (n - 1) % ring)

    carry_ref[...] = jnp.zeros_like(carry_ref)

    def phase_b(i, carry):
        cf, cb = i, n - 1 - i
        st_f = carry_ref[:, :dk]
        st_b = carry_ref[:, dk:]
        st_ref[cf, :, :dk] = st_f.astype(BF16)
        st_ref[cb, :, dk:] = st_b.astype(BF16)
        carry_ref[:, :dk] = st_f * e_ref[cf, 0:1, :dk] + ut_ref[cf, :, :dk]
        carry_ref[:, dk:] = st_b * e_ref[cb, 0:1, dk:] + ut_ref[cb, :, dk:]
        return carry

    lax.fori_loop(0, n, phase_b, 0, unroll=2)

    def phase_c(c, carry):
        rows = rows_of(c)
        o = osum_ref[rows, :] + _dot_nt(qi_ref[rows, :], st_ref[c])
        ms = jnp.mean(o * o, axis=-1, keepdims=True)
        y = o * lax.rsqrt(ms + EPS) * gn_ref[...] * gg_ref[0, rows, :].astype(F32)
        o_ref[0, rows, :] = y.astype(BF16)
        return carry

    lax.fori_loop(0, n, phase_c, 0, unroll=8)


def _gla(gq, gk, gv, gvt, la, gg, gn):
    B, H, S, dk = gq.shape
    dv = gv.shape[-1]
    n = S // GLA_C
    hs = lambda w: pl.BlockSpec((1, 1, S, w), lambda b, h: (b, h, 0, 0))
    cs = pl.BlockSpec((1, S, dv), lambda b, h: (b, 0, h))
    return pl.pallas_call(
        _gla_kernel, grid=(B, H),
        in_specs=[hs(dk), hs(dk), hs(dv), pl.BlockSpec((1, 1, n, dv, GLA_C), lambda b, h: (b, h, 0, 0, 0)),
                  hs(2 * dk), cs, _resident(gn.shape)],
        out_specs=cs,
        out_shape=jax.ShapeDtypeStruct((B, S, H * dv), BF16),
        scratch_shapes=[
            pltpu.VMEM((S, dv), F32),
            pltpu.VMEM((S, 2 * dk), BF16),
            pltpu.VMEM((GLA_RING, GLA_C, 2 * dk), BF16),
            pltpu.VMEM((GLA_RING, GLA_C, 2 * dk), BF16),
            pltpu.VMEM((GLA_RING, GLA_C, 2 * dk), BF16),
            pltpu.VMEM((GLA_RING, GLA_C, GLA_C), BF16),
            pltpu.VMEM((n, dv, 2 * dk), F32),
            pltpu.VMEM((n, dv, 2 * dk), BF16),
            pltpu.VMEM((n, 8, 2 * dk), F32),
            pltpu.VMEM((dv, 2 * dk), F32),
        ],
        compiler_params=pltpu.CompilerParams(
            dimension_semantics=("parallel", "parallel"), vmem_limit_bytes=VMEM_LIMIT),
        name="gla",
    )(gq, gk, gv, gvt, la, gg, gn)


def _attn_kernel(q_ref, k_ref, vt_ref, o_ref):
    skv = k_ref.shape[1]
    nblk = skv // ATT_KVB
    nsub = q_ref.shape[1] // ATT_SUB

    def q_of(t):
        return q_ref[0, t * ATT_SUB:(t + 1) * ATT_SUB, :]

    def scores(q, j):
        return _dot_nt(k_ref[0, j * ATT_KVB:(j + 1) * ATT_KVB, :], q)

    def emit(t, acc, den):
        o_ref[0, t * ATT_SUB:(t + 1) * ATT_SUB, :] = (acc / den).T.astype(BF16)

    in_range = []
    for t in range(nsub):
        q = q_of(t)
        acc = jnp.zeros((ATT_HD, ATT_SUB), F32)
        den = jnp.zeros((1, ATT_SUB), F32)
        s_next = scores(q, 0)
        for j in range(nblk):
            s = s_next
            if j + 1 < nblk:
                s_next = scores(q, j + 1)
            p = jnp.exp2(s)
            den = den + jnp.sum(p, axis=0, keepdims=True)
            acc = acc + _dot(vt_ref[0, :, j * ATT_KVB:(j + 1) * ATT_KVB], p.astype(BF16))
        emit(t, acc, den)
        ok = jnp.logical_and(jnp.min(den) >= 2.0 ** -100, jnp.max(den) <= 2.0 ** 100)
        in_range.append(jnp.logical_and(ok, jnp.max(jnp.abs(acc)) < jnp.inf))

    for t in range(nsub):
        @pl.when(jnp.logical_not(in_range[t]))
        def _():
            q = q_of(t)
            m = jnp.full((1, ATT_SUB), -jnp.inf, F32)
            acc = jnp.zeros((ATT_HD, ATT_SUB), F32)
            den = jnp.zeros((1, ATT_SUB), F32)
            s_next = scores(q, 0)
            for j in range(nblk):
                s = s_next
                if j + 1 < nblk:
                    s_next = scores(q, j + 1)
                m_new = jnp.maximum(m, jnp.max(s, axis=0, keepdims=True))
                alpha = jnp.exp2(m - m_new)
                p = jnp.exp2(s - m_new)
                den = den * alpha + jnp.sum(p, axis=0, keepdims=True)
                acc = acc * alpha + _dot(vt_ref[0, :, j * ATT_KVB:(j + 1) * ATT_KVB], p.astype(BF16))
                m = m_new
            emit(t, acc, den)


def _attn(q, k, vt):
    G, NQ, hd = q.shape
    S = k.shape[1]
    return pl.pallas_call(
        _attn_kernel, grid=(G, NQ // ATT_TQ),
        in_specs=[pl.BlockSpec((1, ATT_TQ, hd), lambda g, i: (g, i, 0)),
                  pl.BlockSpec((1, S, hd), lambda g, i: (g, 0, 0)),
                  pl.BlockSpec((1, vt.shape[1], S), lambda g, i: (g, 0, 0))],
        out_specs=pl.BlockSpec((1, ATT_TQ, hd), lambda g, i: (g, i, 0)),
        out_shape=jax.ShapeDtypeStruct((G, NQ, hd), BF16),
        compiler_params=pltpu.CompilerParams(
            dimension_semantics=("parallel", "parallel"), vmem_limit_bytes=VMEM_LIMIT),
        name="attn",
    )(q, k, vt)


def _post_kernel(x_ref, ma_ref, yb_ref, sgb_ref, wo_ref, fng_ref, wgu_ref, wd_ref, fin_ref, o_ref):
    dff = wd_ref.shape[0]
    tm = x_ref.shape[1]
    groups = [slice(r, r + POST_SUB) for r in range(0, tm, POST_SUB)]

    def rms(v, g_ref):
        return v * lax.rsqrt(jnp.mean(v * v, axis=-1, keepdims=True) + EPS) * g_ref[...]

    x1 = []
    for rs in groups:
        yb = jnp.concatenate([yb_ref[0, hh, rs, :] for hh in range(ATT_HEADS)], axis=-1)
        mixed = ma_ref[0, rs, :].astype(F32) + sgb_ref[0, rs, :].astype(F32) * yb.astype(F32)
        x1.append(x_ref[0, rs, :] + _dot(mixed.astype(BF16), wo_ref[...]))
    act = []
    for v in x1:
        h2 = rms(v, fng_ref).astype(BF16)
        gt = _dot(h2, wgu_ref[:, :dff])
        up = _dot(h2, wgu_ref[:, dff:])
        act.append((gt / (1.0 + jnp.exp(-gt)) * up).astype(BF16))
    for rs, v, a in zip(groups, x1, act):
        o_ref[0, rs, :] = rms(v + _dot(a, wd_ref[...]), fin_ref)


def _post(x, ma, yb, sgb, wo, fng, wgu, wd, fin):
    B, S, D = x.shape
    tm = POST_TM
    rowb = pl.BlockSpec((1, tm, D), lambda b, i: (b, i, 0))
    return pl.pallas_call(
        _post_kernel, grid=(B, S // tm),
        in_specs=[rowb, rowb,
                  pl.BlockSpec((1, ATT_HEADS, tm, ATT_HD), lambda b, i: (b, 0, i, 0)),
                  rowb, _resident(wo.shape), _resident(fng.shape), _resident(wgu.shape),
                  _resident(wd.shape), _resident(fin.shape)],
        out_specs=rowb,
        out_shape=jax.ShapeDtypeStruct((B, S, D), F32),
        compiler_params=pltpu.CompilerParams(
            dimension_semantics=("parallel", "parallel"), vmem_limit_bytes=VMEM_LIMIT),
        name="post",
    )(x, ma, yb, sgb, wo, fng, wgu, wd, fin)


def _rope_tables(seq_len):
    pos = np.arange(seq_len)
    freqs = ROPE_THETA ** (-np.arange(0, 64, 2, dtype=np.float32) / 64.0)
    ang_r = (pos // GRID_W).astype(np.float32)[:, None] * freqs[None, :]
    ang_c = (pos % GRID_W).astype(np.float32)[:, None] * freqs[None, :]
    zero = np.zeros_like(ang_r)
    rc = np.concatenate([np.cos(ang_r), np.cos(ang_r), np.cos(ang_c), np.cos(ang_c)], axis=1)
    rs1 = np.concatenate([-np.sin(ang_r), zero, -np.sin(ang_c), zero], axis=1)
    rs2 = np.concatenate([zero, np.sin(ang_r), zero, np.sin(ang_c)], axis=1)
    return (jnp.asarray(rc, F32), jnp.asarray(rs1, F32), jnp.asarray(rs2, F32))


def kernel(x, mix_norm, w_in, w_gk_fwd, b_gk_fwd, w_gk_bwd, b_gk_bwd, gla_norm, q_norm, k_norm,
           b_gate, w_o, ffn_norm, w_gate_up, w_down, final_norm):
    B, S, D = x.shape
    assert w_in.shape[0] == 1, "single-layer block"
    assert S % max(IN_TM, GLA_C, POST_TM, ATT_KVB) == 0 and (ATT_GROUP * S) % ATT_TQ == 0

    wa, wr, wb = _split_w_in(w_in)
    nqk = GLA_HEADS * GLA_DK
    wgk = jnp.zeros((LANES, 2 * nqk), F32)
    wgk = wgk.at[:GLA_RANK, :nqk].set(w_gk_fwd[0]).at[GLA_RANK:2 * GLA_RANK, nqk:].set(w_gk_bwd[0])
    wgk = wgk.astype(BF16)
    bgk = jnp.concatenate([b_gk_fwd[0], b_gk_bwd[0]])[None, :]
    rc, rs1, rs2 = _rope_tables(S)

    (gq, gk, gv, gvt, la, gga, sgb, aq, ak, avt) = _in_proj(
        x, mix_norm, wa, wr, wb, wgk, bgk, q_norm, k_norm, b_gate[0], rc, rs1, rs2)

    ma = _gla(gq, gk, gv, gvt, la, gga, gla_norm)

    yb = _attn(aq.reshape(B * ATT_KV_HEADS, ATT_GROUP * S, ATT_HD),
               ak.reshape(B * ATT_KV_HEADS, S, ATT_HD),
               avt.reshape(B * ATT_KV_HEADS, ATT_HD, S))
    yb = yb.reshape(B, ATT_HEADS, S, ATT_HD)

    return _post(x, ma, yb, sgb, w_o[0].astype(BF16), ffn_norm, w_gate_up[0].astype(BF16),
                 w_down[0].astype(BF16), final_norm[None, :])
```

```python
import math

import jax
import jax.numpy as jnp
import numpy as np
from jax import lax
from jax.experimental import pallas as pl
from jax.experimental.pallas import tpu as pltpu

F32 = jnp.float32
BF16 = jnp.bfloat16

EPS = 1e-6
GRID_W = 64
ROPE_THETA = 10000.0
GLA_HEADS = 4
GLA_DK = 128
GLA_DV = 256
GLA_RANK = 16
GLA_GATE_NORM = 16.0
ATT_HD = 128
ATT_HEADS = 8
ATT_KV_HEADS = 2
ATT_GROUP = ATT_HEADS // ATT_KV_HEADS

LANES = 128
VMEM_LIMIT = 56 * 1024 * 1024

IN_TM = 512
GLA_C = 128
GLA_RING = 3
ATT_TQ = 2048
ATT_SUB = 1024
ATT_KVB = 1024
POST_TM = 512
POST_SUB = 256


def _dot(a, b):
    return jnp.dot(a, b, preferred_element_type=F32)


def _dot_nt(a, b):
    return lax.dot_general(a, b, (((1,), (1,)), ((), ())), preferred_element_type=F32)


def _sigmoid(x):
    return 1.0 / (1.0 + jnp.exp(-x))


def _resident(shape):
    nd = len(shape)
    return pl.BlockSpec(shape, lambda *_: (0,) * nd, pipeline_mode=pl.Buffered(1))


SEG = {
    "gq": (0, 512), "gk": (512, 1024), "gv": (1024, 2048), "go": (2048, 3072),
    "r": (3072, 3072 + LANES),
    "aq": (3104, 4128), "akv": (4128, 4640), "ga": (4640, 5664), "gb": (5664, 6688),
}


def _in_proj_kernel(x_ref, mixg_ref, wt_ref, wgk_ref, bgk_ref, qn_ref, kn_ref, bg_ref,
                    rc_ref, rs1_ref, rs2_ref,
                    gq_ref, gk_ref, gv_ref, gvt_ref, la_ref, gga_ref, sgb_ref,
                    aq_ref, ak_ref, avt_ref):
    x = x_ref[0]
    ms = jnp.mean(x * x, axis=-1, keepdims=True)
    h = (x * lax.rsqrt(ms + EPS) * mixg_ref[...]).astype(BF16)

    def mm(name):
        lo, hi = SEG[name]
        return _dot_nt(h, wt_ref[lo:hi, :])


    rc, rs1, rs2 = rc_ref[...], rs1_ref[...], rs2_ref[...]

    def rope_tables(g, scale):
        g = jnp.broadcast_to(g, (8, ATT_HD)) * scale
        return rc * g[0:1], rs1 * pltpu.roll(g, 96, 1)[0:1], rs2 * pltpu.roll(g, 32, 1)[0:1]

    def norm_rope(a, tabs):
        r = lax.rsqrt(jnp.mean(a * a, axis=-1, keepdims=True) + EPS)
        return (a * tabs[0] + pltpu.roll(a, 96, 1) * tabs[1] + pltpu.roll(a, 32, 1) * tabs[2]) * r

    aq = mm("aq")
    q_tabs = rope_tables(qn_ref[...], (ATT_HD ** -0.5) * math.log2(math.e))
    for hh in range(ATT_HEADS):
        aq_ref[0, hh] = norm_rope(aq[:, hh * ATT_HD:(hh + 1) * ATT_HD], q_tabs).astype(BF16)
    akv = mm("akv")
    k_tabs = rope_tables(kn_ref[...], 1.0)
    for hh in range(ATT_KV_HEADS):
        ak_ref[0, hh] = norm_rope(akv[:, hh * ATT_HD:(hh + 1) * ATT_HD], k_tabs).astype(BF16)
        v = akv[:, 256 + hh * ATT_HD:256 + (hh + 1) * ATT_HD]
        avt_ref[0, hh] = v.T.astype(BF16)

    r = mm("r").astype(BF16)
    z = _dot(r, wgk_ref[...]) + bgk_ref[...]
    la = (jnp.minimum(z, 0.0) - jnp.log(1.0 + jnp.exp(-jnp.abs(z)))) * (1.0 / GLA_GATE_NORM)
    nqk = GLA_HEADS * GLA_DK
    for hh in range(GLA_HEADS):
        la_ref[0, hh] = jnp.concatenate(
            [la[:, hh * GLA_DK:(hh + 1) * GLA_DK], la[:, nqk + hh * GLA_DK:nqk + (hh + 1) * GLA_DK]], axis=1)

    go = mm("go")
    ga = mm("ga") + bg_ref[0:1, :]
    gga_ref[0] = (go / ((1.0 + jnp.exp(-go)) * (1.0 + jnp.exp(-ga)))).astype(BF16)

    gb = mm("gb") + bg_ref[1:2, :]
    sgb_ref[0] = gb.astype(BF16)

    gv = mm("gv")
    for hh in range(GLA_HEADS):
        vh = gv[:, hh * GLA_DV:(hh + 1) * GLA_DV]
        gv_ref[0, hh] = vh.astype(BF16)
        for j in range(vh.shape[0] // GLA_C):
            gvt_ref[0, hh, j] = vh[j * GLA_C:(j + 1) * GLA_C].T.astype(BF16)
    gq = mm("gq") * (GLA_DK ** -0.5)
    gk = mm("gk")
    for hh in range(GLA_HEADS):
        sl = slice(hh * GLA_DK, (hh + 1) * GLA_DK)
        gq_ref[0, hh] = gq[:, sl].astype(BF16)
        gk_ref[0, hh] = gk[:, sl].astype(BF16)


def _in_proj(x, mixg, wt, wgk, bgk, qn, kn, bg, rc, rs1, rs2):
    B, S, D = x.shape
    tm = IN_TM
    grid = (B, S // tm)

    def hb(nh, width):
        return pl.BlockSpec((1, nh, tm, width), lambda b, i: (b, 0, i, 0))

    rowb = pl.BlockSpec((1, tm, D), lambda b, i: (b, i, 0))
    tab = pl.BlockSpec((tm, LANES), lambda b, i: (i, 0))
    out_shape = (
        jax.ShapeDtypeStruct((B, GLA_HEADS, S, GLA_DK), BF16),
        jax.ShapeDtypeStruct((B, GLA_HEADS, S, GLA_DK), BF16),
        jax.ShapeDtypeStruct((B, GLA_HEADS, S, GLA_DV), BF16),
        jax.ShapeDtypeStruct((B, GLA_HEADS, S // GLA_C, GLA_DV, GLA_C), BF16),
        jax.ShapeDtypeStruct((B, GLA_HEADS, S, 2 * GLA_DK), F32),
        jax.ShapeDtypeStruct((B, S, D), BF16),
        jax.ShapeDtypeStruct((B, S, D), BF16),
        jax.ShapeDtypeStruct((B, ATT_HEADS, S, ATT_HD), BF16),
        jax.ShapeDtypeStruct((B, ATT_KV_HEADS, S, ATT_HD), BF16),
        jax.ShapeDtypeStruct((B, ATT_KV_HEADS, ATT_HD, S), BF16),
    )
    out_specs = (
        hb(GLA_HEADS, GLA_DK), hb(GLA_HEADS, GLA_DK), hb(GLA_HEADS, GLA_DV),
        pl.BlockSpec((1, GLA_HEADS, tm // GLA_C, GLA_DV, GLA_C), lambda b, i: (b, 0, i, 0, 0)),
        hb(GLA_HEADS, 2 * GLA_DK),
        rowb, rowb,
        hb(ATT_HEADS, ATT_HD), hb(ATT_KV_HEADS, ATT_HD),
        pl.BlockSpec((1, ATT_KV_HEADS, ATT_HD, tm), lambda b, i: (b, 0, 0, i)),
    )
    in_specs = [
        rowb, _resident(mixg.shape), _resident(wt.shape), _resident(wgk.shape), _resident(bgk.shape),
        _resident(qn.shape), _resident(kn.shape), _resident(bg.shape), tab, tab, tab,
    ]
    return pl.pallas_call(
        _in_proj_kernel, grid=grid, in_specs=in_specs, out_specs=out_specs, out_shape=out_shape,
        compiler_params=pltpu.CompilerParams(
            dimension_semantics=("parallel", "parallel"), vmem_limit_bytes=VMEM_LIMIT),
        name="in_proj",
    )(x, mixg, wt, wgk, bgk, qn, kn, bg, rc, rs1, rs2)


def _gla_kernel(q_ref, k_ref, v_ref, vt_ref, la_ref, gg_ref, gn_ref, o_ref,
                osum_ref, qi_ref, qd_ring, kd_ring, ks_ring, s_ring, ut_ref, st_ref, e_ref, carry_ref):
    C, dk = GLA_C, GLA_DK
    S = q_ref.shape[2]
    n = S // C

    row = lax.broadcasted_iota(jnp.int32, (C, C), 0)
    col = lax.broadcasted_iota(jnp.int32, (C, C), 1)
    tri = (row >= col).astype(BF16)
    fwd_pair = row >= col

    def rows_of(c):
        return pl.ds(c * C if isinstance(c, int) else pl.multiple_of(c * C, C), C)

    def prefix_sums(c):
        la = la_ref[0, 0, rows_of(c), :]
        hi = la.astype(BF16)
        lo = (la - hi.astype(F32)).astype(BF16)
        return la, _dot(tri, hi) + _dot(tri, lo)

    def decay_factors(c, slot, la, pre):
        rows = rows_of(c)
        tot = pre[C - 1:C, :]
        dist_f = pre[:, :dk]
        dist_b = tot[:, dk:] - pre[:, dk:] + la[:, dk:]
        tot_f, tot_b = tot[:, :dk], tot[:, dk:]
        q = q_ref[0, 0, rows, :].astype(F32)
        k = k_ref[0, 0, rows, :].astype(F32)

        def factors(dist, mid, tt):
            qd = (q * jnp.exp(dist - mid)).astype(BF16)
            kd = (k * jnp.exp(mid - dist)).astype(BF16)
            qi = (q * jnp.exp(dist)).astype(BF16)
            ks = (k * jnp.exp(tt - dist)).astype(BF16)
            return qd, kd, qi, ks

        qd_f, kd_f, qi_f, ks_f = factors(dist_f, dist_f[C // 2 - 1:C // 2, :], tot_f)
        qd_b, kd_b, qi_b, ks_b = factors(dist_b, dist_b[C // 2:C // 2 + 1, :], tot_b)
        qd_ring[slot] = jnp.concatenate([qd_f, qd_b], axis=1)
        kd_ring[slot] = jnp.concatenate([kd_f, kd_b], axis=1)
        ks_ring[slot] = jnp.concatenate([ks_f, ks_b], axis=1)
        qi_ref[rows, :] = jnp.concatenate([qi_f, qi_b], axis=1)
        e_ref[c] = jnp.broadcast_to(jnp.exp(tot), e_ref.shape[1:])

    def intra_scores(slot):
        s_f = _dot_nt(qd_ring[slot, :, :dk], kd_ring[slot, :, :dk])
        s_b = _dot_nt(qd_ring[slot, :, dk:], kd_ring[slot, :, dk:])
        s_ring[slot] = jnp.where(fwd_pair, s_f, s_b).astype(BF16)

    def intra_out(c, slot):
        osum_ref[rows_of(c), :] = _dot(s_ring[slot], v_ref[0, 0, rows_of(c), :])
        ut_ref[c] = _dot(vt_ref[0, 0, c], ks_ring[slot])

    ring = qd_ring.shape[0]
    unroll = 2 * ring
    assert (n - 2) % unroll == 0

    def phase_a(it, carry):
        base = 2 + it * unroll
        for u in range(unroll):
            i = base + u
            la, pre = prefix_sums(i)
            intra_out(i - 2, u % ring)
            intra_scores((u + 1) % ring)
            decay_factors(i, (u + 2) % ring, la, pre)
        return carry

    decay_factors(0, 0, *prefix_sums(0))
    la1, pre1 = prefix_sums(1)
    intra_scores(0)
    decay_factors(1, 1, la1, pre1)
    lax.fori_loop(0, (n - 2) // unroll, phase_a, 0)
    intra_out(n - 2, (n - 2) % ring)
    intra_scores((n - 1) % ring)
    intra_out(n - 1, (n - 1) % ring)

    carry_ref[...] = jnp.zeros_like(carry_ref)

    def phase_b(i, carry):
        cf, cb = i, n - 1 - i
        st_f = carry_ref[:, :dk]
        st_b = carry_ref[:, dk:]
        st_ref[cf, :, :dk] = st_f.astype(BF16)
        st_ref[cb, :, dk:] = st_b.astype(BF16)
        carry_ref[:, :dk] = st_f * e_ref[cf, 0:1, :dk] + ut_ref[cf, :, :dk]
        carry_ref[:, dk:] = st_b * e_ref[cb, 0:1, dk:] + ut_ref[cb, :, dk:]
        return carry

    lax.fori_loop(0, n, phase_b, 0, unroll=2)

    def phase_c(c, carry):
        rows = rows_of(c)
        o = osum_ref[rows, :] + _dot_nt(qi_ref[rows, :], st_ref[c])
        ms = jnp.mean(o * o, axis=-1, keepdims=True)
        y = o * lax.rsqrt(ms + EPS) * gn_ref[...] * gg_ref[0, rows, :].astype(F32)
        o_ref[0, rows, :] = y.astype(BF16)
        return carry

    lax.fori_loop(0, n, phase_c, 0, unroll=8)


def _gla(gq, gk, gv, gvt, la, gg, gn):
    B, H, S, dk = gq.shape
    dv = gv.shape[-1]
    n = S // GLA_C
    hs = lambda w: pl.BlockSpec((1, 1, S, w), lambda b, h: (b, h, 0, 0))
    cs = pl.BlockSpec((1, S, dv), lambda b, h: (b, 0, h))
    return pl.pallas_call(
        _gla_kernel, grid=(B, H),
        in_specs=[hs(dk), hs(dk), hs(dv), pl.BlockSpec((1, 1, n, dv, GLA_C), lambda b, h: (b, h, 0, 0, 0)),
                  hs(2 * dk), cs, _resident(gn.shape)],
        out_specs=cs,
        out_shape=jax.ShapeDtypeStruct((B, S, H * dv), BF16),
        scratch_shapes=[
            pltpu.VMEM((S, dv), F32),
            pltpu.VMEM((S, 2 * dk), BF16),
            pltpu.VMEM((GLA_RING, GLA_C, 2 * dk), BF16),
            pltpu.VMEM((GLA_RING, GLA_C, 2 * dk), BF16),
            pltpu.VMEM((GLA_RING, GLA_C, 2 * dk), BF16),
            pltpu.VMEM((GLA_RING, GLA_C, GLA_C), BF16),
            pltpu.VMEM((n, dv, 2 * dk), F32),
            pltpu.VMEM((n, dv, 2 * dk), BF16),
            pltpu.VMEM((n, 8, 2 * dk), F32),
            pltpu.VMEM((dv, 2 * dk), F32),
        ],
        compiler_params=pltpu.CompilerParams(
            dimension_semantics=("parallel", "parallel"), vmem_limit_bytes=VMEM_LIMIT),
        name="gla",
    )(gq, gk, gv, gvt, la, gg, gn)


def _attn_kernel(q_ref, k_ref, vt_ref, o_ref):
    skv = k_ref.shape[1]
    nblk = skv // ATT_KVB
    nsub = q_ref.shape[1] // ATT_SUB

    def q_of(t):
        return q_ref[0, t * ATT_SUB:(t + 1) * ATT_SUB, :]

    def scores(q, j):
        return _dot_nt(k_ref[0, j * ATT_KVB:(j + 1) * ATT_KVB, :], q)

    def emit(t, acc, den):
        o_ref[0, t * ATT_SUB:(t + 1) * ATT_SUB, :] = (acc / den).T.astype(BF16)

    in_range = []
    for t in range(nsub):
        q = q_of(t)
        acc = jnp.zeros((ATT_HD, ATT_SUB), F32)
        den = jnp.zeros((1, ATT_SUB), F32)
        s_next = scores(q, 0)
        for j in range(nblk):
            s = s_next
            if j + 1 < nblk:
                s_next = scores(q, j + 1)
            p = jnp.exp2(s)
            den = den + jnp.sum(p, axis=0, keepdims=True)
            acc = acc + _dot(vt_ref[0, :, j * ATT_KVB:(j + 1) * ATT_KVB], p.astype(BF16))
        emit(t, acc, den)
        ok = jnp.logical_and(jnp.min(den) >= 2.0 ** -100, jnp.max(den) <= 2.0 ** 100)
        in_range.append(jnp.logical_and(ok, jnp.max(jnp.abs(acc)) < jnp.inf))

    for t in range(nsub):
        @pl.when(jnp.logical_not(in_range[t]))
        def _():
            q = q_of(t)
            m = jnp.full((1, ATT_SUB), -jnp.inf, F32)
            acc = jnp.zeros((ATT_HD, ATT_SUB), F32)
            den = jnp.zeros((1, ATT_SUB), F32)
            s_next = scores(q, 0)
            for j in range(nblk):
                s = s_next
                if j + 1 < nblk:
                    s_next = scores(q, j + 1)
                m_new = jnp.maximum(m, jnp.max(s, axis=0, keepdims=True))
                alpha = jnp.exp2(m - m_new)
                p = jnp.exp2(s - m_new)
                den = den * alpha + jnp.sum(p, axis=0, keepdims=True)
                acc = acc * alpha + _dot(vt_ref[0, :, j * ATT_KVB:(j + 1) * ATT_KVB], p.astype(BF16))
                m = m_new
            emit(t, acc, den)


def _attn(q, k, vt):
    G, NQ, hd = q.shape
    S = k.shape[1]
    return pl.pallas_call(
        _attn_kernel, grid=(G, NQ // ATT_TQ),
        in_specs=[pl.BlockSpec((1, ATT_TQ, hd), lambda g, i: (g, i, 0)),
                  pl.BlockSpec((1, S, hd), lambda g, i: (g, 0, 0)),
                  pl.BlockSpec((1, vt.shape[1], S), lambda g, i: (g, 0, 0))],
        out_specs=pl.BlockSpec((1, ATT_TQ, hd), lambda g, i: (g, i, 0)),
        out_shape=jax.ShapeDtypeStruct((G, NQ, hd), BF16),
        compiler_params=pltpu.CompilerParams(
            dimension_semantics=("parallel", "parallel"), vmem_limit_bytes=VMEM_LIMIT),
        name="attn",
    )(q, k, vt)


def _post_kernel(x_ref, ma_ref, yb_ref, sgb_ref, wo_ref, fng_ref, wgu_ref, wd_ref, fin_ref, o_ref):
    dff = wd_ref.shape[0]
    tm = x_ref.shape[1]
    groups = [slice(r, r + POST_SUB) for r in range(0, tm, POST_SUB)]

    def rms(v, g_ref):
        return v * lax.rsqrt(jnp.mean(v * v, axis=-1, keepdims=True) + EPS) * g_ref[...]

    x1 = []
    for rs in groups:
        yb = jnp.concatenate([yb_ref[0, hh, rs, :] for hh in range(ATT_HEADS)], axis=-1)
        mixed = ma_ref[0, rs, :].astype(F32) + _sigmoid(sgb_ref[0, rs, :].astype(F32)) * yb.astype(F32)
        x1.append(x_ref[0, rs, :] + _dot(mixed.astype(BF16), wo_ref[...]))
    act = []
    for v in x1:
        h2 = rms(v, fng_ref).astype(BF16)
        gt = _dot(h2, wgu_ref[:, :dff])
        up = _dot(h2, wgu_ref[:, dff:])
        act.append((gt / (1.0 + jnp.exp(-gt)) * up).astype(BF16))
    for rs, v, a in zip(groups, x1, act):
        o_ref[0, rs, :] = rms(v + _dot(a, wd_ref[...]), fin_ref)


def _post(x, ma, yb, sgb, wo, fng, wgu, wd, fin):
    B, S, D = x.shape
    tm = POST_TM
    rowb = pl.BlockSpec((1, tm, D), lambda b, i: (b, i, 0))
    return pl.pallas_call(
        _post_kernel, grid=(B, S // tm),
        in_specs=[rowb, rowb,
                  pl.BlockSpec((1, ATT_HEADS, tm, ATT_HD), lambda b, i: (b, 0, i, 0)),
                  rowb, _resident(wo.shape), _resident(fng.shape), _resident(wgu.shape),
                  _resident(wd.shape), _resident(fin.shape)],
        out_specs=rowb,
        out_shape=jax.ShapeDtypeStruct((B, S, D), F32),
        compiler_params=pltpu.CompilerParams(
            dimension_semantics=("parallel", "parallel"), vmem_limit_bytes=VMEM_LIMIT),
        name="post",
    )(x, ma, yb, sgb, wo, fng, wgu, wd, fin)


def _rope_tables(seq_len):
    pos = np.arange(seq_len)
    freqs = ROPE_THETA ** (-np.arange(0, 64, 2, dtype=np.float32) / 64.0)
    ang_r = (pos // GRID_W).astype(np.float32)[:, None] * freqs[None, :]
    ang_c = (pos % GRID_W).astype(np.float32)[:, None] * freqs[None, :]
    zero = np.zeros_like(ang_r)
    rc = np.concatenate([np.cos(ang_r), np.cos(ang_r), np.cos(ang_c), np.cos(ang_c)], axis=1)
    rs1 = np.concatenate([-np.sin(ang_r), zero, -np.sin(ang_c), zero], axis=1)
    rs2 = np.concatenate([zero, np.sin(ang_r), zero, np.sin(ang_c)], axis=1)
    return (jnp.asarray(rc, F32), jnp.asarray(rs1, F32), jnp.asarray(rs2, F32))


def kernel(x, mix_norm, w_in, w_gk_fwd, b_gk_fwd, w_gk_bwd, b_gk_bwd, gla_norm, q_norm, k_norm,
           b_gate, w_o, ffn_norm, w_gate_up, w_down, final_norm):
    B, S, D = x.shape
    assert w_in.shape[0] == 1, "single-layer block"
    assert S % max(IN_TM, GLA_C, POST_TM, ATT_KVB) == 0 and (ATT_GROUP * S) % ATT_TQ == 0

    wt = jnp.swapaxes(w_in[0], 0, 1).astype(BF16)
    nqk = GLA_HEADS * GLA_DK
    wgk = jnp.zeros((LANES, 2 * nqk), F32)
    wgk = wgk.at[:GLA_RANK, :nqk].set(w_gk_fwd[0]).at[GLA_RANK:2 * GLA_RANK, nqk:].set(w_gk_bwd[0])
    wgk = wgk.astype(BF16)
    bgk = jnp.concatenate([b_gk_fwd[0], b_gk_bwd[0]])[None, :]
    rc, rs1, rs2 = _rope_tables(S)

    (gq, gk, gv, gvt, la, gga, sgb, aq, ak, avt) = _in_proj(
        x, mix_norm, wt, wgk, bgk, q_norm, k_norm, b_gate[0], rc, rs1, rs2)

    ma = _gla(gq, gk, gv, gvt, la, gga, gla_norm)

    yb = _attn(aq.reshape(B * ATT_KV_HEADS, ATT_GROUP * S, ATT_HD),
               ak.reshape(B * ATT_KV_HEADS, S, ATT_HD),
               avt.reshape(B * ATT_KV_HEADS, ATT_HD, S))
    yb = yb.reshape(B, ATT_HEADS, S, ATT_HD)

    return _post(x, ma, yb, sgb, w_o[0].astype(BF16), ffn_norm, w_gate_up[0].astype(BF16),
                 w_down[0].astype(BF16), final_norm[None, :])
```

```python
import math

import jax
import jax.numpy as jnp
import numpy as np
from jax import lax
from jax.experimental import pallas as pl
from jax.experimental.pallas import tpu as pltpu

F32 = jnp.float32
BF16 = jnp.bfloat16

EPS = 1e-6
GRID_W = 64
ROPE_THETA = 10000.0
GLA_HEADS = 4
GLA_DK = 128
GLA_DV = 256
GLA_RANK = 16
GLA_GATE_NORM = 16.0
ATT_HD = 128
ATT_HEADS = 8
ATT_KV_HEADS = 2
ATT_GROUP = ATT_HEADS // ATT_KV_HEADS

LANES = 128
VMEM_LIMIT = 56 * 1024 * 1024

IN_TM = 512
GLA_C = 128
GLA_RING = 3
ATT_TQ = 4096
ATT_SUB = 1024
ATT_KVB = 1024
POST_TM = 512
POST_SUB = 256


def _dot(a, b):
    return jnp.dot(a, b, preferred_element_type=F32)


def _dot_nt(a, b):
    return lax.dot_general(a, b, (((1,), (1,)), ((), ())), preferred_element_type=F32)


def _sigmoid(x):
    return 1.0 / (1.0 + jnp.exp(-x))


def _resident(shape):
    nd = len(shape)
    return pl.BlockSpec(shape, lambda *_: (0,) * nd, pipeline_mode=pl.Buffered(1))


SEG = {
    "gq": (0, 512), "gk": (512, 1024), "gv": (1024, 2048), "go": (2048, 3072),
    "r": (3072, 3072 + LANES),
    "aq": (3104, 4128), "akv": (4128, 4640), "ga": (4640, 5664), "gb": (5664, 6688),
}


def _in_proj_kernel(x_ref, mixg_ref, wt_ref, wgk_ref, bgk_ref, qn_ref, kn_ref, bg_ref,
                    rc_ref, rs1_ref, rs2_ref,
                    gq_ref, gk_ref, gv_ref, gvt_ref, la_ref, gga_ref, sgb_ref,
                    aq_ref, ak_ref, avt_ref):
    x = x_ref[0]
    ms = jnp.mean(x * x, axis=-1, keepdims=True)
    h = (x * lax.rsqrt(ms + EPS) * mixg_ref[...]).astype(BF16)

    def mm(name):
        lo, hi = SEG[name]
        return _dot_nt(h, wt_ref[lo:hi, :])


    rc, rs1, rs2 = rc_ref[...], rs1_ref[...], rs2_ref[...]

    def rope_tables(g, scale):
        g = jnp.broadcast_to(g, (8, ATT_HD)) * scale
        return rc * g[0:1], rs1 * pltpu.roll(g, 96, 1)[0:1], rs2 * pltpu.roll(g, 32, 1)[0:1]

    def norm_rope(a, tabs):
        r = lax.rsqrt(jnp.mean(a * a, axis=-1, keepdims=True) + EPS)
        return (a * tabs[0] + pltpu.roll(a, 96, 1) * tabs[1] + pltpu.roll(a, 32, 1) * tabs[2]) * r

    aq = mm("aq")
    q_tabs = rope_tables(qn_ref[...], (ATT_HD ** -0.5) * math.log2(math.e))
    for hh in range(ATT_HEADS):
        aq_ref[0, hh] = norm_rope(aq[:, hh * ATT_HD:(hh + 1) * ATT_HD], q_tabs).astype(BF16)
    akv = mm("akv")
    k_tabs = rope_tables(kn_ref[...], 1.0)
    for hh in range(ATT_KV_HEADS):
        ak_ref[0, hh] = norm_rope(akv[:, hh * ATT_HD:(hh + 1) * ATT_HD], k_tabs).astype(BF16)
        v = akv[:, 256 + hh * ATT_HD:256 + (hh + 1) * ATT_HD]
        avt_ref[0, hh] = v.T.astype(BF16)

    r = mm("r").astype(BF16)
    z = _dot(r, wgk_ref[...]) + bgk_ref[...]
    la = (jnp.minimum(z, 0.0) - jnp.log(1.0 + jnp.exp(-jnp.abs(z)))) * (1.0 / GLA_GATE_NORM)
    nqk = GLA_HEADS * GLA_DK
    for hh in range(GLA_HEADS):
        la_ref[0, hh] = jnp.concatenate(
            [la[:, hh * GLA_DK:(hh + 1) * GLA_DK], la[:, nqk + hh * GLA_DK:nqk + (hh + 1) * GLA_DK]], axis=1)

    go = mm("go")
    ga = mm("ga") + bg_ref[0:1, :]
    gga_ref[0] = (go / ((1.0 + jnp.exp(-go)) * (1.0 + jnp.exp(-ga)))).astype(BF16)

    gb = mm("gb") + bg_ref[1:2, :]
    sgb_ref[0] = gb.astype(BF16)

    gv = mm("gv")
    for hh in range(GLA_HEADS):
        vh = gv[:, hh * GLA_DV:(hh + 1) * GLA_DV]
        gv_ref[0, hh] = vh.astype(BF16)
        for j in range(vh.shape[0] // GLA_C):
            gvt_ref[0, hh, j] = vh[j * GLA_C:(j + 1) * GLA_C].T.astype(BF16)
    gq = mm("gq") * (GLA_DK ** -0.5)
    gk = mm("gk")
    for hh in range(GLA_HEADS):
        sl = slice(hh * GLA_DK, (hh + 1) * GLA_DK)
        gq_ref[0, hh] = gq[:, sl].astype(BF16)
        gk_ref[0, hh] = gk[:, sl].astype(BF16)


def _in_proj(x, mixg, wt, wgk, bgk, qn, kn, bg, rc, rs1, rs2):
    B, S, D = x.shape
    tm = IN_TM
    grid = (B, S // tm)

    def hb(nh, width):
        return pl.BlockSpec((1, nh, tm, width), lambda b, i: (b, 0, i, 0))

    rowb = pl.BlockSpec((1, tm, D), lambda b, i: (b, i, 0))
    tab = pl.BlockSpec((tm, LANES), lambda b, i: (i, 0))
    out_shape = (
        jax.ShapeDtypeStruct((B, GLA_HEADS, S, GLA_DK), BF16),
        jax.ShapeDtypeStruct((B, GLA_HEADS, S, GLA_DK), BF16),
        jax.ShapeDtypeStruct((B, GLA_HEADS, S, GLA_DV), BF16),
        jax.ShapeDtypeStruct((B, GLA_HEADS, S // GLA_C, GLA_DV, GLA_C), BF16),
        jax.ShapeDtypeStruct((B, GLA_HEADS, S, 2 * GLA_DK), F32),
        jax.ShapeDtypeStruct((B, S, D), BF16),
        jax.ShapeDtypeStruct((B, S, D), BF16),
        jax.ShapeDtypeStruct((B, ATT_HEADS, S, ATT_HD), BF16),
        jax.ShapeDtypeStruct((B, ATT_KV_HEADS, S, ATT_HD), BF16),
        jax.ShapeDtypeStruct((B, ATT_KV_HEADS, ATT_HD, S), BF16),
    )
    out_specs = (
        hb(GLA_HEADS, GLA_DK), hb(GLA_HEADS, GLA_DK), hb(GLA_HEADS, GLA_DV),
        pl.BlockSpec((1, GLA_HEADS, tm // GLA_C, GLA_DV, GLA_C), lambda b, i: (b, 0, i, 0, 0)),
        hb(GLA_HEADS, 2 * GLA_DK),
        rowb, rowb,
        hb(ATT_HEADS, ATT_HD), hb(ATT_KV_HEADS, ATT_HD),
        pl.BlockSpec((1, ATT_KV_HEADS, ATT_HD, tm), lambda b, i: (b, 0, 0, i)),
    )
    in_specs = [
        rowb, _resident(mixg.shape), _resident(wt.shape), _resident(wgk.shape), _resident(bgk.shape),
        _resident(qn.shape), _resident(kn.shape), _resident(bg.shape), tab, tab, tab,
    ]
    return pl.pallas_call(
        _in_proj_kernel, grid=grid, in_specs=in_specs, out_specs=out_specs, out_shape=out_shape,
        compiler_params=pltpu.CompilerParams(
            dimension_semantics=("parallel", "parallel"), vmem_limit_bytes=VMEM_LIMIT),
        name="in_proj",
    )(x, mixg, wt, wgk, bgk, qn, kn, bg, rc, rs1, rs2)


def _gla_kernel(q_ref, k_ref, v_ref, vt_ref, la_ref, gg_ref, gn_ref, o_ref,
                osum_ref, qi_ref, qd_ring, kd_ring, ks_ring, s_ring, ut_ref, st_ref, e_ref, carry_ref):
    C, dk = GLA_C, GLA_DK
    S = q_ref.shape[2]
    n = S // C

    row = lax.broadcasted_iota(jnp.int32, (C, C), 0)
    col = lax.broadcasted_iota(jnp.int32, (C, C), 1)
    tri = (row >= col).astype(BF16)
    fwd_pair = row >= col

    def rows_of(c):
        return pl.ds(c * C if isinstance(c, int) else pl.multiple_of(c * C, C), C)

    def prefix_sums(c):
        la = la_ref[0, 0, rows_of(c), :]
        hi = la.astype(BF16)
        lo = (la - hi.astype(F32)).astype(BF16)
        return la, _dot(tri, hi) + _dot(tri, lo)

    def decay_factors(c, slot, la, pre):
        rows = rows_of(c)
        tot = pre[C - 1:C, :]
        dist_f = pre[:, :dk]
        dist_b = tot[:, dk:] - pre[:, dk:] + la[:, dk:]
        tot_f, tot_b = tot[:, :dk], tot[:, dk:]
        q = q_ref[0, 0, rows, :].astype(F32)
        k = k_ref[0, 0, rows, :].astype(F32)

        def factors(dist, mid, tt):
            qd = (q * jnp.exp(dist - mid)).astype(BF16)
            kd = (k * jnp.exp(mid - dist)).astype(BF16)
            qi = (q * jnp.exp(dist)).astype(BF16)
            ks = (k * jnp.exp(tt - dist)).astype(BF16)
            return qd, kd, qi, ks

        qd_f, kd_f, qi_f, ks_f = factors(dist_f, dist_f[C // 2 - 1:C // 2, :], tot_f)
        qd_b, kd_b, qi_b, ks_b = factors(dist_b, dist_b[C // 2:C // 2 + 1, :], tot_b)
        qd_ring[slot] = jnp.concatenate([qd_f, qd_b], axis=1)
        kd_ring[slot] = jnp.concatenate([kd_f, kd_b], axis=1)
        ks_ring[slot] = jnp.concatenate([ks_f, ks_b], axis=1)
        qi_ref[rows, :] = jnp.concatenate([qi_f, qi_b], axis=1)
        e_ref[c] = jnp.broadcast_to(jnp.exp(tot), e_ref.shape[1:])

    def intra_scores(slot):
        s_f = _dot_nt(qd_ring[slot, :, :dk], kd_ring[slot, :, :dk])
        s_b = _dot_nt(qd_ring[slot, :, dk:], kd_ring[slot, :, dk:])
        s_ring[slot] = jnp.where(fwd_pair, s_f, s_b).astype(BF16)

    def intra_out(c, slot):
        osum_ref[rows_of(c), :] = _dot(s_ring[slot], v_ref[0, 0, rows_of(c), :])
        ut_ref[c] = _dot(vt_ref[0, 0, c], ks_ring[slot])

    ring = qd_ring.shape[0]
    unroll = 2 * ring
    assert (n - 2) % unroll == 0

    def phase_a(it, carry):
        base = 2 + it * unroll
        for u in range(unroll):
            i = base + u
            la, pre = prefix_sums(i)
            intra_out(i - 2, u % ring)
            intra_scores((u + 1) % ring)
            decay_factors(i, (u + 2) % ring, la, pre)
        return carry

    decay_factors(0, 0, *prefix_sums(0))
    la1, pre1 = prefix_sums(1)
    intra_scores(0)
    decay_factors(1, 1, la1, pre1)
    lax.fori_loop(0, (n - 2) // unroll, phase_a, 0)
    intra_out(n - 2, (n - 2) % ring)
    intra_scores((n - 1) % ring)
    intra_out(n - 1, (n - 1) % ring)

    carry_ref[...] = jnp.zeros_like(carry_ref)

    def phase_b(i, carry):
        cf, cb = i, n - 1 - i
        st_f = carry_ref[:, :dk]
        st_b = carry_ref[:, dk:]
        st_ref[cf, :, :dk] = st_f.astype(BF16)
        st_ref[cb, :, dk:] = st_b.astype(BF16)
        carry_ref[:, :dk] = st_f * e_ref[cf, 0:1, :dk] + ut_ref[cf, :, :dk]
        carry_ref[:, dk:] = st_b * e_ref[cb, 0:1, dk:] + ut_ref[cb, :, dk:]
        return carry

    lax.fori_loop(0, n, phase_b, 0, unroll=2)

    def phase_c(c, carry):
        rows = rows_of(c)
        o = osum_ref[rows, :] + _dot_nt(qi_ref[rows, :], st_ref[c])
        ms = jnp.mean(o * o, axis=-1, keepdims=True)
        y = o * lax.rsqrt(ms + EPS) * gn_ref[...] * gg_ref[0, rows, :].astype(F32)
        o_ref[0, rows, :] = y.astype(BF16)
        return carry

    lax.fori_loop(0, n, phase_c, 0, unroll=8)


def _gla(gq, gk, gv, gvt, la, gg, gn):
    B, H, S, dk = gq.shape
    dv = gv.shape[-1]
    n = S // GLA_C
    hs = lambda w: pl.BlockSpec((1, 1, S, w), lambda b, h: (b, h, 0, 0))
    cs = pl.BlockSpec((1, S, dv), lambda b, h: (b, 0, h))
    return pl.pallas_call(
        _gla_kernel, grid=(B, H),
        in_specs=[hs(dk), hs(dk), hs(dv), pl.BlockSpec((1, 1, n, dv, GLA_C), lambda b, h: (b, h, 0, 0, 0)),
                  hs(2 * dk), cs, _resident(gn.shape)],
        out_specs=cs,
        out_shape=jax.ShapeDtypeStruct((B, S, H * dv), BF16),
        scratch_shapes=[
            pltpu.VMEM((S, dv), F32),
            pltpu.VMEM((S, 2 * dk), BF16),
            pltpu.VMEM((GLA_RING, GLA_C, 2 * dk), BF16),
            pltpu.VMEM((GLA_RING, GLA_C, 2 * dk), BF16),
            pltpu.VMEM((GLA_RING, GLA_C, 2 * dk), BF16),
            pltpu.VMEM((GLA_RING, GLA_C, GLA_C), BF16),
            pltpu.VMEM((n, dv, 2 * dk), F32),
            pltpu.VMEM((n, dv, 2 * dk), BF16),
            pltpu.VMEM((n, 8, 2 * dk), F32),
            pltpu.VMEM((dv, 2 * dk), F32),
        ],
        compiler_params=pltpu.CompilerParams(
            dimension_semantics=("parallel", "parallel"), vmem_limit_bytes=VMEM_LIMIT),
        name="gla",
    )(gq, gk, gv, gvt, la, gg, gn)


def _attn_kernel(q_ref, k_ref, vt_ref, o_ref):
    skv = k_ref.shape[1]
    nblk = skv // ATT_KVB
    nsub = q_ref.shape[1] // ATT_SUB

    def sub_rows(t):
        return pl.ds(t * ATT_SUB if isinstance(t, int) else pl.multiple_of(t * ATT_SUB, ATT_SUB), ATT_SUB)

    def scores(q, j):
        return _dot_nt(k_ref[0, j * ATT_KVB:(j + 1) * ATT_KVB, :], q)

    def emit(t, acc, den):
        o_ref[0, sub_rows(t), :] = (acc / den).T.astype(BF16)

    in_range = None
    for t in range(nsub):
        q = q_ref[0, sub_rows(t), :]
        acc = jnp.zeros((ATT_HD, ATT_SUB), F32)
        den = jnp.zeros((1, ATT_SUB), F32)
        s_next = scores(q, 0)
        for j in range(nblk):
            s = s_next
            if j + 1 < nblk:
                s_next = scores(q, j + 1)
            p = jnp.exp2(s)
            den = den + jnp.sum(p, axis=0, keepdims=True)
            acc = acc + _dot(vt_ref[0, :, j * ATT_KVB:(j + 1) * ATT_KVB], p.astype(BF16))
        emit(t, acc, den)
        ok = jnp.logical_and(jnp.min(den) >= 2.0 ** -100, jnp.max(den) <= 2.0 ** 100)
        ok = jnp.logical_and(ok, jnp.max(jnp.abs(acc)) < jnp.inf)
        in_range = ok if in_range is None else jnp.logical_and(in_range, ok)

    @pl.when(jnp.logical_not(in_range))
    def _():
        def redo(t, carry):
            q = q_ref[0, sub_rows(t), :]
            m = jnp.full((1, ATT_SUB), -jnp.inf, F32)
            acc = jnp.zeros((ATT_HD, ATT_SUB), F32)
            den = jnp.zeros((1, ATT_SUB), F32)
            for j in range(nblk):
                s = scores(q, j)
                m_new = jnp.maximum(m, jnp.max(s, axis=0, keepdims=True))
                alpha = jnp.exp2(m - m_new)
                p = jnp.exp2(s - m_new)
                den = den * alpha + jnp.sum(p, axis=0, keepdims=True)
                acc = acc * alpha + _dot(vt_ref[0, :, j * ATT_KVB:(j + 1) * ATT_KVB], p.astype(BF16))
                m = m_new
            emit(t, acc, den)
            return carry

        lax.fori_loop(0, nsub, redo, 0)


def _attn(q, k, vt):
    G, NQ, hd = q.shape
    S = k.shape[1]
    return pl.pallas_call(
        _attn_kernel, grid=(G, NQ // ATT_TQ),
        in_specs=[pl.BlockSpec((1, ATT_TQ, hd), lambda g, i: (g, i, 0)),
                  pl.BlockSpec((1, S, hd), lambda g, i: (g, 0, 0)),
                  pl.BlockSpec((1, vt.shape[1], S), lambda g, i: (g, 0, 0))],
        out_specs=pl.BlockSpec((1, ATT_TQ, hd), lambda g, i: (g, i, 0)),
        out_shape=jax.ShapeDtypeStruct((G, NQ, hd), BF16),
        compiler_params=pltpu.CompilerParams(
            dimension_semantics=("parallel", "parallel"), vmem_limit_bytes=VMEM_LIMIT),
        name="attn",
    )(q, k, vt)


def _post_kernel(x_ref, ma_ref, yb_ref, sgb_ref, wo_ref, fng_ref, wgu_ref, wd_ref, fin_ref, o_ref):
    dff = wd_ref.shape[0]
    tm = x_ref.shape[1]
    groups = [slice(r, r + POST_SUB) for r in range(0, tm, POST_SUB)]

    def rms(v, g_ref):
        return v * lax.rsqrt(jnp.mean(v * v, axis=-1, keepdims=True) + EPS) * g_ref[...]

    x1 = []
    for rs in groups:
        yb = jnp.concatenate([yb_ref[0, hh, rs, :] for hh in range(ATT_HEADS)], axis=-1)
        mixed = ma_ref[0, rs, :].astype(F32) + _sigmoid(sgb_ref[0, rs, :].astype(F32)) * yb.astype(F32)
        x1.append(x_ref[0, rs, :] + _dot(mixed.astype(BF16), wo_ref[...]))
    act = []
    for v in x1:
        h2 = rms(v, fng_ref).astype(BF16)
        gt = _dot(h2, wgu_ref[:, :dff])
        up = _dot(h2, wgu_ref[:, dff:])
        act.append((gt / (1.0 + jnp.exp(-gt)) * up).astype(BF16))
    for rs, v, a in zip(groups, x1, act):
        o_ref[0, rs, :] = rms(v + _dot(a, wd_ref[...]), fin_ref)


def _post(x, ma, yb, sgb, wo, fng, wgu, wd, fin):
    B, S, D = x.shape
    tm = POST_TM
    rowb = pl.BlockSpec((1, tm, D), lambda b, i: (b, i, 0))
    return pl.pallas_call(
        _post_kernel, grid=(B, S // tm),
        in_specs=[rowb, rowb,
                  pl.BlockSpec((1, ATT_HEADS, tm, ATT_HD), lambda b, i: (b, 0, i, 0)),
                  rowb, _resident(wo.shape), _resident(fng.shape), _resident(wgu.shape),
                  _resident(wd.shape), _resident(fin.shape)],
        out_specs=rowb,
        out_shape=jax.ShapeDtypeStruct((B, S, D), F32),
        compiler_params=pltpu.CompilerParams(
            dimension_semantics=("parallel", "parallel"), vmem_limit_bytes=VMEM_LIMIT),
        name="post",
    )(x, ma, yb, sgb, wo, fng, wgu, wd, fin)


def _rope_tables(seq_len):
    pos = np.arange(seq_len)
    freqs = ROPE_THETA ** (-np.arange(0, 64, 2, dtype=np.float32) / 64.0)
    ang_r = (pos // GRID_W).astype(np.float32)[:, None] * freqs[None, :]
    ang_c = (pos % GRID_W).astype(np.float32)[:, None] * freqs[None, :]
    zero = np.zeros_like(ang_r)
    rc = np.concatenate([np.cos(ang_r), np.cos(ang_r), np.cos(ang_c), np.cos(ang_c)], axis=1)
    rs1 = np.concatenate([-np.sin(ang_r), zero, -np.sin(ang_c), zero], axis=1)
    rs2 = np.concatenate([zero, np.sin(ang_r), zero, np.sin(ang_c)], axis=1)
    return (jnp.asarray(rc, F32), jnp.asarray(rs1, F32), jnp.asarray(rs2, F32))


def kernel(x, mix_norm, w_in, w_gk_fwd, b_gk_fwd, w_gk_bwd, b_gk_bwd, gla_norm, q_norm, k_norm,
           b_gate, w_o, ffn_norm, w_gate_up, w_down, final_norm):
    B, S, D = x.shape
    assert w_in.shape[0] == 1, "single-layer block"
    assert S % max(IN_TM, GLA_C, POST_TM, ATT_KVB) == 0 and (ATT_GROUP * S) % ATT_TQ == 0

    wt = jnp.swapaxes(w_in[0], 0, 1).astype(BF16)
    nqk = GLA_HEADS * GLA_DK
    wgk = jnp.zeros((LANES, 2 * nqk), F32)
    wgk = wgk.at[:GLA_RANK, :nqk].set(w_gk_fwd[0]).at[GLA_RANK:2 * GLA_RANK, nqk:].set(w_gk_bwd[0])
    wgk = wgk.astype(BF16)
    bgk = jnp.concatenate([b_gk_fwd[0], b_gk_bwd[0]])[None, :]
    rc, rs1, rs2 = _rope_tables(S)

    (gq, gk, gv, gvt, la, gga, sgb, aq, ak, avt) = _in_proj(
        x, mix_norm, wt, wgk, bgk, q_norm, k_norm, b_gate[0], rc, rs1, rs2)

    ma = _gla(gq, gk, gv, gvt, la, gga, gla_norm)

    yb = _attn(aq.reshape(B * ATT_KV_HEADS, ATT_GROUP * S, ATT_HD),
               ak.reshape(B * ATT_KV_HEADS, S, ATT_HD),
               avt.reshape(B * ATT_KV_HEADS, ATT_HD, S))
    yb = yb.reshape(B, ATT_HEADS, S, ATT_HD)

    return _post(x, ma, yb, sgb, w_o[0].astype(BF16), ffn_norm, w_gate_up[0].astype(BF16),
                 w_down[0].astype(BF16), final_norm[None, :])
```

```python
import math

import jax
import jax.numpy as jnp
import numpy as np
from jax import lax
from jax.experimental import pallas as pl
from jax.experimental.pallas import tpu as pltpu

F32 = jnp.float32
BF16 = jnp.bfloat16

EPS = 1e-6
GRID_W = 64
ROPE_THETA = 10000.0
GLA_HEADS = 4
GLA_DK = 128
GLA_DV = 256
GLA_RANK = 16
GLA_GATE_NORM = 16.0
ATT_HD = 128
ATT_HEADS = 8
ATT_KV_HEADS = 2
ATT_GROUP = ATT_HEADS // ATT_KV_HEADS

LANES = 128
VMEM_LIMIT = 56 * 1024 * 1024

IN_TM = 512
GLA_C = 128
GLA_LAG_SCORES = 2
GLA_LAG_OUT = 2
GLA_RING = 7
GLA_UNROLL = 14
ATT_TQ = 4096
ATT_SUB = 1024
ATT_KVB = 1024
POST_TM = 512
POST_SUB = 256


def _dot(a, b):
    return jnp.dot(a, b, preferred_element_type=F32)


def _dot_nt(a, b):
    return lax.dot_general(a, b, (((1,), (1,)), ((), ())), preferred_element_type=F32)


def _sigmoid(x):
    return 1.0 / (1.0 + jnp.exp(-x))


def _resident(shape):
    nd = len(shape)
    return pl.BlockSpec(shape, lambda *_: (0,) * nd, pipeline_mode=pl.Buffered(1))


SEG = {
    "gq": (0, 512), "gk": (512, 1024), "gv": (1024, 2048), "go": (2048, 3072),
    "r": (3072, 3072 + LANES),
    "aq": (3104, 4128), "akv": (4128, 4640), "ga": (4640, 5664), "gb": (5664, 6688),
}


def _in_proj_kernel(x_ref, mixg_ref, wt_ref, wgk_ref, bgk_ref, qn_ref, kn_ref, bg_ref,
                    rc_ref, rs1_ref, rs2_ref,
                    gq_ref, gk_ref, gv_ref, gvt_ref, la_ref, gga_ref, sgb_ref,
                    aq_ref, ak_ref, avt_ref):
    x = x_ref[0]
    ms = jnp.mean(x * x, axis=-1, keepdims=True)
    h = (x * lax.rsqrt(ms + EPS) * mixg_ref[...]).astype(BF16)

    def mm(name):
        lo, hi = SEG[name]
        return _dot_nt(h, wt_ref[lo:hi, :])


    rc, rs1, rs2 = rc_ref[...], rs1_ref[...], rs2_ref[...]

    def rope_tables(g, scale):
        g = jnp.broadcast_to(g, (8, ATT_HD)) * scale
        return rc * g[0:1], rs1 * pltpu.roll(g, 96, 1)[0:1], rs2 * pltpu.roll(g, 32, 1)[0:1]

    def norm_rope(a, tabs):
        r = lax.rsqrt(jnp.mean(a * a, axis=-1, keepdims=True) + EPS)
        return (a * tabs[0] + pltpu.roll(a, 96, 1) * tabs[1] + pltpu.roll(a, 32, 1) * tabs[2]) * r

    aq = mm("aq")
    q_tabs = rope_tables(qn_ref[...], (ATT_HD ** -0.5) * math.log2(math.e))
    for hh in range(ATT_HEADS):
        aq_ref[0, hh] = norm_rope(aq[:, hh * ATT_HD:(hh + 1) * ATT_HD], q_tabs).astype(BF16)
    akv = mm("akv")
    k_tabs = rope_tables(kn_ref[...], 1.0)
    for hh in range(ATT_KV_HEADS):
        ak_ref[0, hh] = norm_rope(akv[:, hh * ATT_HD:(hh + 1) * ATT_HD], k_tabs).astype(BF16)
        v = akv[:, 256 + hh * ATT_HD:256 + (hh + 1) * ATT_HD]
        avt_ref[0, hh] = v.T.astype(BF16)

    r = mm("r").astype(BF16)
    z = _dot(r, wgk_ref[...]) + bgk_ref[...]
    la = (jnp.minimum(z, 0.0) - jnp.log(1.0 + jnp.exp(-jnp.abs(z)))) * (1.0 / GLA_GATE_NORM)
    nqk = GLA_HEADS * GLA_DK
    for hh in range(GLA_HEADS):
        la_ref[0, hh] = jnp.concatenate(
            [la[:, hh * GLA_DK:(hh + 1) * GLA_DK], la[:, nqk + hh * GLA_DK:nqk + (hh + 1) * GLA_DK]],
            axis=1).astype(BF16)

    go = mm("go")
    ga = mm("ga") + bg_ref[0:1, :]
    gga_ref[0] = (go / ((1.0 + jnp.exp(-go)) * (1.0 + jnp.exp(-ga)))).astype(BF16)

    gb = mm("gb") + bg_ref[1:2, :]
    sgb_ref[0] = gb.astype(BF16)

    gv = mm("gv")
    for hh in range(GLA_HEADS):
        vh = gv[:, hh * GLA_DV:(hh + 1) * GLA_DV]
        gv_ref[0, hh] = vh.astype(BF16)
        for j in range(vh.shape[0] // GLA_C):
            gvt_ref[0, hh, j] = vh[j * GLA_C:(j + 1) * GLA_C].T.astype(BF16)
    gq = mm("gq") * (GLA_DK ** -0.5)
    gk = mm("gk")
    for hh in range(GLA_HEADS):
        sl = slice(hh * GLA_DK, (hh + 1) * GLA_DK)
        gq_ref[0, hh] = gq[:, sl].astype(BF16)
        gk_ref[0, hh] = gk[:, sl].astype(BF16)


def _in_proj(x, mixg, wt, wgk, bgk, qn, kn, bg, rc, rs1, rs2):
    B, S, D = x.shape
    tm = IN_TM
    grid = (B, S // tm)

    def hb(nh, width):
        return pl.BlockSpec((1, nh, tm, width), lambda b, i: (b, 0, i, 0))

    rowb = pl.BlockSpec((1, tm, D), lambda b, i: (b, i, 0))
    tab = pl.BlockSpec((tm, LANES), lambda b, i: (i, 0))
    out_shape = (
        jax.ShapeDtypeStruct((B, GLA_HEADS, S, GLA_DK), BF16),
        jax.ShapeDtypeStruct((B, GLA_HEADS, S, GLA_DK), BF16),
        jax.ShapeDtypeStruct((B, GLA_HEADS, S, GLA_DV), BF16),
        jax.ShapeDtypeStruct((B, GLA_HEADS, S // GLA_C, GLA_DV, GLA_C), BF16),
        jax.ShapeDtypeStruct((B, GLA_HEADS, S, 2 * GLA_DK), BF16),
        jax.ShapeDtypeStruct((B, S, D), BF16),
        jax.ShapeDtypeStruct((B, S, D), BF16),
        jax.ShapeDtypeStruct((B, ATT_HEADS, S, ATT_HD), BF16),
        jax.ShapeDtypeStruct((B, ATT_KV_HEADS, S, ATT_HD), BF16),
        jax.ShapeDtypeStruct((B, ATT_KV_HEADS, ATT_HD, S), BF16),
    )
    out_specs = (
        hb(GLA_HEADS, GLA_DK), hb(GLA_HEADS, GLA_DK), hb(GLA_HEADS, GLA_DV),
        pl.BlockSpec((1, GLA_HEADS, tm // GLA_C, GLA_DV, GLA_C), lambda b, i: (b, 0, i, 0, 0)),
        hb(GLA_HEADS, 2 * GLA_DK),
        rowb, rowb,
        hb(ATT_HEADS, ATT_HD), hb(ATT_KV_HEADS, ATT_HD),
        pl.BlockSpec((1, ATT_KV_HEADS, ATT_HD, tm), lambda b, i: (b, 0, 0, i)),
    )
    in_specs = [
        rowb, _resident(mixg.shape), _resident(wt.shape), _resident(wgk.shape), _resident(bgk.shape),
        _resident(qn.shape), _resident(kn.shape), _resident(bg.shape), tab, tab, tab,
    ]
    return pl.pallas_call(
        _in_proj_kernel, grid=grid, in_specs=in_specs, out_specs=out_specs, out_shape=out_shape,
        compiler_params=pltpu.CompilerParams(
            dimension_semantics=("parallel", "parallel"), vmem_limit_bytes=VMEM_LIMIT),
        name="in_proj",
    )(x, mixg, wt, wgk, bgk, qn, kn, bg, rc, rs1, rs2)


def _gla_kernel(q_ref, k_ref, v_ref, vt_ref, la_ref, gg_ref, gn_ref, o_ref,
                osum_ref, qi_ref, qd_ring, kd_ring, ks_ring, s_ring, ut_ref, st_ref, e_ref, carry_ref):
    C, dk = GLA_C, GLA_DK
    S = q_ref.shape[2]
    n = S // C

    row = lax.broadcasted_iota(jnp.int32, (C, C), 0)
    col = lax.broadcasted_iota(jnp.int32, (C, C), 1)
    tri = (row >= col).astype(BF16)
    fwd_pair = row >= col

    def rows_of(c):
        return pl.ds(c * C if isinstance(c, int) else pl.multiple_of(c * C, C), C)

    def prefix_sums(c):
        la = la_ref[0, 0, rows_of(c), :]
        return la, _dot(tri, la)

    def decay_factors(c, slot, la, pre):
        rows = rows_of(c)
        tot = pre[C - 1:C, :]
        dist_f = pre[:, :dk]
        dist_b = tot[:, dk:] - pre[:, dk:] + la[:, dk:].astype(F32)
        tot_f, tot_b = tot[:, :dk], tot[:, dk:]
        q = q_ref[0, 0, rows, :].astype(F32)
        k = k_ref[0, 0, rows, :].astype(F32)

        def factors(dist, mid, tt):
            qd = (q * jnp.exp(dist - mid)).astype(BF16)
            kd = (k * jnp.exp(mid - dist)).astype(BF16)
            qi = (q * jnp.exp(dist)).astype(BF16)
            ks = (k * jnp.exp(tt - dist)).astype(BF16)
            return qd, kd, qi, ks

        qd_f, kd_f, qi_f, ks_f = factors(dist_f, dist_f[C // 2 - 1:C // 2, :], tot_f)
        qd_b, kd_b, qi_b, ks_b = factors(dist_b, dist_b[C // 2:C // 2 + 1, :], tot_b)
        qd_ring[slot] = jnp.concatenate([qd_f, qd_b], axis=1)
        kd_ring[slot, :C, :dk] = kd_f
        kd_ring[slot, C:, dk:] = kd_b
        ks_ring[slot] = jnp.concatenate([ks_f, ks_b], axis=1)
        qi_ref[rows, :] = jnp.concatenate([qi_f, qi_b], axis=1)
        e_ref[c] = jnp.broadcast_to(jnp.exp(tot), e_ref.shape[1:])

    def intra_scores(slot):
        s_fb = _dot_nt(qd_ring[slot], kd_ring[slot])
        s_ring[slot] = jnp.where(fwd_pair, s_fb[:, :C], s_fb[:, C:]).astype(BF16)

    def intra_out(c, slot):
        osum_ref[rows_of(c), :] = _dot(s_ring[slot], v_ref[0, 0, rows_of(c), :])
        ut_ref[c] = _dot(vt_ref[0, 0, c], ks_ring[slot])

    ring = qd_ring.shape[0]
    lag_s, lag_o = GLA_LAG_SCORES, GLA_LAG_SCORES + GLA_LAG_OUT
    unroll = GLA_UNROLL
    assert ring > lag_o and unroll % ring == 0 and (n - lag_o) % unroll == 0
    kd_ring[...] = jnp.zeros_like(kd_ring)

    def step(i, i_mod_ring):
        static = isinstance(i, int)
        live = lambda c: (not static) or 0 <= c < n
        slot = lambda lag: (i_mod_ring - lag) % ring
        if live(i):
            la, pre = prefix_sums(i)
        if live(i - lag_o):
            intra_out(i - lag_o, slot(lag_o))
        if live(i - lag_s):
            intra_scores(slot(lag_s))
        if live(i):
            decay_factors(i, slot(0), la, pre)

    def phase_a(it, carry):
        base = lag_o + it * unroll
        for u in range(unroll):
            step(base + u, (lag_o + u) % ring)
        return carry

    for i in range(lag_o):
        step(i, i % ring)
    if (n - lag_o) // unroll == 1:
        phase_a(0, 0)
    else:
        lax.fori_loop(0, (n - lag_o) // unroll, phase_a, 0)
    for i in range(n, n + lag_o):
        step(i, i % ring)

    carry_ref[...] = jnp.zeros_like(carry_ref)

    def phase_b(i, carry):
        cf, cb = i, n - 1 - i
        st_f = carry_ref[:, :dk]
        st_b = carry_ref[:, dk:]
        st_ref[cf, :, :dk] = st_f.astype(BF16)
        st_ref[cb, :, dk:] = st_b.astype(BF16)
        carry_ref[:, :dk] = st_f * e_ref[cf, 0:1, :dk] + ut_ref[cf, :, :dk]
        carry_ref[:, dk:] = st_b * e_ref[cb, 0:1, dk:] + ut_ref[cb, :, dk:]
        return carry

    lax.fori_loop(0, n, phase_b, 0, unroll=2)

    def phase_c(c, carry):
        rows = rows_of(c)
        o = osum_ref[rows, :] + _dot_nt(qi_ref[rows, :], st_ref[c])
        ms = jnp.mean(o * o, axis=-1, keepdims=True)
        y = o * lax.rsqrt(ms + EPS) * gn_ref[...] * gg_ref[0, rows, :].astype(F32)
        o_ref[0, rows, :] = y.astype(BF16)
        return carry

    lax.fori_loop(0, n, phase_c, 0, unroll=8)


def _gla(gq, gk, gv, gvt, la, gg, gn):
    B, H, S, dk = gq.shape
    dv = gv.shape[-1]
    n = S // GLA_C
    hs = lambda w: pl.BlockSpec((1, 1, S, w), lambda b, h: (b, h, 0, 0))
    cs = pl.BlockSpec((1, S, dv), lambda b, h: (b, 0, h))
    return pl.pallas_call(
        _gla_kernel, grid=(B, H),
        in_specs=[hs(dk), hs(dk), hs(dv), pl.BlockSpec((1, 1, n, dv, GLA_C), lambda b, h: (b, h, 0, 0, 0)),
                  hs(2 * dk), cs, _resident(gn.shape)],
        out_specs=cs,
        out_shape=jax.ShapeDtypeStruct((B, S, H * dv), BF16),
        scratch_shapes=[
            pltpu.VMEM((S, dv), F32),
            pltpu.VMEM((S, 2 * dk), BF16),
            pltpu.VMEM((GLA_RING, GLA_C, 2 * dk), BF16),
            pltpu.VMEM((GLA_RING, 2 * GLA_C, 2 * dk), BF16),
            pltpu.VMEM((GLA_RING, GLA_C, 2 * dk), BF16),
            pltpu.VMEM((GLA_RING, GLA_C, GLA_C), BF16),
            pltpu.VMEM((n, dv, 2 * dk), F32),
            pltpu.VMEM((n, dv, 2 * dk), BF16),
            pltpu.VMEM((n, 8, 2 * dk), F32),
            pltpu.VMEM((dv, 2 * dk), F32),
        ],
        compiler_params=pltpu.CompilerParams(
            dimension_semantics=("parallel", "parallel"), vmem_limit_bytes=VMEM_LIMIT),
        name="gla",
    )(gq, gk, gv, gvt, la, gg, gn)


def _attn_kernel(q_ref, k_ref, vt_ref, o_ref):
    skv = k_ref.shape[1]
    nblk = skv // ATT_KVB
    nsub = q_ref.shape[1] // ATT_SUB

    def sub_rows(t):
        return pl.ds(t * ATT_SUB if isinstance(t, int) else pl.multiple_of(t * ATT_SUB, ATT_SUB), ATT_SUB)

    def scores(q, j):
        return _dot_nt(k_ref[0, j * ATT_KVB:(j + 1) * ATT_KVB, :], q)

    def emit(t, acc, den):
        o_ref[0, sub_rows(t), :] = (acc / den).T.astype(BF16)

    in_range = None
    for t in range(nsub):
        q = q_ref[0, sub_rows(t), :]
        acc = jnp.zeros((ATT_HD, ATT_SUB), F32)
        den = jnp.zeros((1, ATT_SUB), F32)
        s_next = scores(q, 0)
        for j in range(nblk):
            s = s_next
            if j + 1 < nblk:
                s_next = scores(q, j + 1)
            p = jnp.exp2(s)
            den = den + jnp.sum(p, axis=0, keepdims=True)
            acc = acc + _dot(vt_ref[0, :, j * ATT_KVB:(j + 1) * ATT_KVB], p.astype(BF16))
        emit(t, acc, den)
        ok = jnp.logical_and(jnp.min(den) >= 2.0 ** -100, jnp.max(den) <= 2.0 ** 100)
        ok = jnp.logical_and(ok, jnp.max(jnp.abs(acc)) < jnp.inf)
        in_range = ok if in_range is None else jnp.logical_and(in_range, ok)

    @pl.when(jnp.logical_not(in_range))
    def _():
        def redo(t, carry):
            q = q_ref[0, sub_rows(t), :]
            m = jnp.full((1, ATT_SUB), -jnp.inf, F32)
            acc = jnp.zeros((ATT_HD, ATT_SUB), F32)
            den = jnp.zeros((1, ATT_SUB), F32)
            for j in range(nblk):
                s = scores(q, j)
                m_new = jnp.maximum(m, jnp.max(s, axis=0, keepdims=True))
                alpha = jnp.exp2(m - m_new)
                p = jnp.exp2(s - m_new)
                den = den * alpha + jnp.sum(p, axis=0, keepdims=True)
                acc = acc * alpha + _dot(vt_ref[0, :, j * ATT_KVB:(j + 1) * ATT_KVB], p.astype(BF16))
                m = m_new
            emit(t, acc, den)
            return carry

        lax.fori_loop(0, nsub, redo, 0)


def _attn(q, k, vt):
    G, NQ, hd = q.shape
    S = k.shape[1]
    return pl.pallas_call(
        _attn_kernel, grid=(G, NQ // ATT_TQ),
        in_specs=[pl.BlockSpec((1, ATT_TQ, hd), lambda g, i: (g, i, 0)),
                  pl.BlockSpec((1, S, hd), lambda g, i: (g, 0, 0)),
                  pl.BlockSpec((1, vt.shape[1], S), lambda g, i: (g, 0, 0))],
        out_specs=pl.BlockSpec((1, ATT_TQ, hd), lambda g, i: (g, i, 0)),
        out_shape=jax.ShapeDtypeStruct((G, NQ, hd), BF16),
        compiler_params=pltpu.CompilerParams(
            dimension_semantics=("parallel", "parallel"), vmem_limit_bytes=VMEM_LIMIT),
        name="attn",
    )(q, k, vt)


def _post_kernel(x_ref, ma_ref, yb_ref, sgb_ref, wo_ref, fng_ref, wgu_ref, wd_ref, fin_ref, o_ref):
    dff = wd_ref.shape[0]
    tm = x_ref.shape[1]
    groups = [slice(r, r + POST_SUB) for r in range(0, tm, POST_SUB)]

    def rms(v, g_ref):
        return v * lax.rsqrt(jnp.mean(v * v, axis=-1, keepdims=True) + EPS) * g_ref[...]

    x1 = []
    for rs in groups:
        yb = jnp.concatenate([yb_ref[0, hh, rs, :] for hh in range(ATT_HEADS)], axis=-1)
        mixed = ma_ref[0, rs, :].astype(F32) + _sigmoid(sgb_ref[0, rs, :].astype(F32)) * yb.astype(F32)
        x1.append(x_ref[0, rs, :] + _dot(mixed.astype(BF16), wo_ref[...]))
    act = []
    for v in x1:
        h2 = rms(v, fng_ref).astype(BF16)
        gt = _dot(h2, wgu_ref[:, :dff])
        up = _dot(h2, wgu_ref[:, dff:])
        act.append((gt / (1.0 + jnp.exp(-gt)) * up).astype(BF16))
    for rs, v, a in zip(groups, x1, act):
        o_ref[0, rs, :] = rms(v + _dot(a, wd_ref[...]), fin_ref)


def _post(x, ma, yb, sgb, wo, fng, wgu, wd, fin):
    B, S, D = x.shape
    tm = POST_TM
    rowb = pl.BlockSpec((1, tm, D), lambda b, i: (b, i, 0))
    return pl.pallas_call(
        _post_kernel, grid=(B, S // tm),
        in_specs=[rowb, rowb,
                  pl.BlockSpec((1, ATT_HEADS, tm, ATT_HD), lambda b, i: (b, 0, i, 0)),
                  rowb, _resident(wo.shape), _resident(fng.shape), _resident(wgu.shape),
                  _resident(wd.shape), _resident(fin.shape)],
        out_specs=rowb,
        out_shape=jax.ShapeDtypeStruct((B, S, D), F32),
        compiler_params=pltpu.CompilerParams(
            dimension_semantics=("parallel", "parallel"), vmem_limit_bytes=VMEM_LIMIT),
        name="post",
    )(x, ma, yb, sgb, wo, fng, wgu, wd, fin)


def _rope_tables(seq_len):
    pos = np.arange(seq_len)
    freqs = ROPE_THETA ** (-np.arange(0, 64, 2, dtype=np.float32) / 64.0)
    ang_r = (pos // GRID_W).astype(np.float32)[:, None] * freqs[None, :]
    ang_c = (pos % GRID_W).astype(np.float32)[:, None] * freqs[None, :]
    zero = np.zeros_like(ang_r)
    rc = np.concatenate([np.cos(ang_r), np.cos(ang_r), np.cos(ang_c), np.cos(ang_c)], axis=1)
    rs1 = np.concatenate([-np.sin(ang_r), zero, -np.sin(ang_c), zero], axis=1)
    rs2 = np.concatenate([zero, np.sin(ang_r), zero, np.sin(ang_c)], axis=1)
    return (jnp.asarray(rc, F32), jnp.asarray(rs1, F32), jnp.asarray(rs2, F32))


def kernel(x, mix_norm, w_in, w_gk_fwd, b_gk_fwd, w_gk_bwd, b_gk_bwd, gla_norm, q_norm, k_norm,
           b_gate, w_o, ffn_norm, w_gate_up, w_down, final_norm):
    B, S, D = x.shape
    assert w_in.shape[0] == 1, "single-layer block"
    assert S % max(IN_TM, GLA_C, POST_TM, ATT_KVB) == 0 and (ATT_GROUP * S) % ATT_TQ == 0

    wt = jnp.swapaxes(w_in[0], 0, 1).astype(BF16)
    nqk = GLA_HEADS * GLA_DK
    wgk = jnp.zeros((LANES, 2 * nqk), F32)
    wgk = wgk.at[:GLA_RANK, :nqk].set(w_gk_fwd[0]).at[GLA_RANK:2 * GLA_RANK, nqk:].set(w_gk_bwd[0])
    wgk = wgk.astype(BF16)
    bgk = jnp.concatenate([b_gk_fwd[0], b_gk_bwd[0]])[None, :]
    rc, rs1, rs2 = _rope_tables(S)

    (gq, gk, gv, gvt, la, gga, sgb, aq, ak, avt) = _in_proj(
        x, mix_norm, wt, wgk, bgk, q_norm, k_norm, b_gate[0], rc, rs1, rs2)

    ma = _gla(gq, gk, gv, gvt, la, gga, gla_norm)

    yb = _attn(aq.reshape(B * ATT_KV_HEADS, ATT_GROUP * S, ATT_HD),
               ak.reshape(B * ATT_KV_HEADS, S, ATT_HD),
               avt.reshape(B * ATT_KV_HEADS, ATT_HD, S))
    yb = yb.reshape(B, ATT_HEADS, S, ATT_HD)

    return _post(x, ma, yb, sgb, w_o[0].astype(BF16), ffn_norm, w_gate_up[0].astype(BF16),
                 w_down[0].astype(BF16), final_norm[None, :])
```

```python
import functools
import math

import jax
import jax.numpy as jnp
import numpy as np
from jax import lax
from jax.experimental import pallas as pl
from jax.experimental.pallas import tpu as pltpu

F32 = jnp.float32
BF16 = jnp.bfloat16

EPS = 1e-6
GRID_W = 64
ROPE_THETA = 10000.0
GLA_HEADS = 4
GLA_DK = 128
GLA_DV = 256
GLA_RANK = 16
GLA_GATE_NORM = 16.0
ATT_HD = 128
ATT_HEADS = 8
ATT_KV_HEADS = 2
ATT_GROUP = ATT_HEADS // ATT_KV_HEADS

LANES = 128
VMEM_LIMIT = 56 * 1024 * 1024

IN_TM = 512
GLA_C = 128
GLA_LAG_SCORES = 2
GLA_LAG_OUT = 2
GLA_RING = 7
GLA_UNROLL = 14
ATT_TQ = 4096
ATT_SUB = 1024
ATT_KVB = 1024
POST_TM = 512
POST_SUB = 256


def _dot(a, b):
    return jnp.dot(a, b, preferred_element_type=F32)


def _dot_nt(a, b):
    return lax.dot_general(a, b, (((1,), (1,)), ((), ())), preferred_element_type=F32)


def _sigmoid(x):
    return 1.0 / (1.0 + jnp.exp(-x))


def _resident(shape):
    nd = len(shape)
    return pl.BlockSpec(shape, lambda *_: (0,) * nd, pipeline_mode=pl.Buffered(1))


SEG = {
    "gq": (0, 512), "gk": (512, 1024), "gv": (1024, 2048), "go": (2048, 3072),
    "r": (3072, 3072 + LANES),
    "aq": (3104, 4128), "akv": (4128, 4640), "ga": (4640, 5664), "gb": (5664, 6688),
}


def _in_proj_kernel(x_ref, mixg_ref, wt_ref, wgk_ref, bgk_ref, qn_ref, kn_ref, bg_ref,
                    rc_ref, rs1_ref, rs2_ref,
                    gq_ref, gk_ref, gv_ref, gvt_ref, la_ref, gga_ref, sgb_ref,
                    aq_ref, ak_ref, avt_ref):
    x = x_ref[0]
    ms = jnp.mean(x * x, axis=-1, keepdims=True)
    h = (x * lax.rsqrt(ms + EPS) * mixg_ref[...]).astype(BF16)

    def mm(name):
        lo, hi = SEG[name]
        return _dot_nt(h, wt_ref[lo:hi, :])


    rc, rs1, rs2 = rc_ref[...], rs1_ref[...], rs2_ref[...]

    def rope_tables(g, scale):
        g = jnp.broadcast_to(g, (8, ATT_HD)) * scale
        return rc * g[0:1], rs1 * pltpu.roll(g, 96, 1)[0:1], rs2 * pltpu.roll(g, 32, 1)[0:1]

    def norm_rope(a, tabs):
        r = lax.rsqrt(jnp.mean(a * a, axis=-1, keepdims=True) + EPS)
        return (a * tabs[0] + pltpu.roll(a, 96, 1) * tabs[1] + pltpu.roll(a, 32, 1) * tabs[2]) * r

    r = mm("r").astype(BF16)

    aq = mm("aq")
    q_tabs = rope_tables(qn_ref[...], (ATT_HD ** -0.5) * math.log2(math.e))
    for hh in range(ATT_HEADS):
        aq_ref[0, hh] = norm_rope(aq[:, hh * ATT_HD:(hh + 1) * ATT_HD], q_tabs).astype(BF16)
    akv = mm("akv")
    k_tabs = rope_tables(kn_ref[...], 1.0)
    for hh in range(ATT_KV_HEADS):
        ak_ref[0, hh] = norm_rope(akv[:, hh * ATT_HD:(hh + 1) * ATT_HD], k_tabs).astype(BF16)
        v = akv[:, 256 + hh * ATT_HD:256 + (hh + 1) * ATT_HD]
        avt_ref[0, hh] = v.T.astype(BF16)

    z = _dot(r, wgk_ref[...]) + bgk_ref[...]
    la = (jnp.minimum(z, 0.0) - jnp.log(1.0 + jnp.exp(-jnp.abs(z)))) * (1.0 / GLA_GATE_NORM)
    nqk = GLA_HEADS * GLA_DK
    for hh in range(GLA_HEADS):
        la_ref[0, hh] = jnp.concatenate(
            [la[:, hh * GLA_DK:(hh + 1) * GLA_DK], la[:, nqk + hh * GLA_DK:nqk + (hh + 1) * GLA_DK]],
            axis=1).astype(BF16)

    go = mm("go")
    ga = mm("ga") + bg_ref[0:1, :]
    gga_ref[0] = (go / ((1.0 + jnp.exp(-go)) * (1.0 + jnp.exp(-ga)))).astype(BF16)

    gb = mm("gb") + bg_ref[1:2, :]
    sgb_ref[0] = gb.astype(BF16)

    gv = mm("gv")
    for hh in range(GLA_HEADS):
        vh = gv[:, hh * GLA_DV:(hh + 1) * GLA_DV]
        gv_ref[0, hh] = vh.astype(BF16)
        for j in range(vh.shape[0] // GLA_C):
            gvt_ref[0, hh, j] = vh[j * GLA_C:(j + 1) * GLA_C].T.astype(BF16)
    gq = mm("gq") * (GLA_DK ** -0.5)
    gk = mm("gk")
    for hh in range(GLA_HEADS):
        sl = slice(hh * GLA_DK, (hh + 1) * GLA_DK)
        gq_ref[0, hh] = gq[:, sl].astype(BF16)
        gk_ref[0, hh] = gk[:, sl].astype(BF16)


def _in_proj(x, mixg, wt, wgk, bgk, qn, kn, bg, rc, rs1, rs2):
    B, S, D = x.shape
    tm = IN_TM
    grid = (B, S // tm)

    def hb(nh, width):
        return pl.BlockSpec((1, nh, tm, width), lambda b, i: (b, 0, i, 0))

    rowb = pl.BlockSpec((1, tm, D), lambda b, i: (b, i, 0))
    tab = pl.BlockSpec((tm, LANES), lambda b, i: (i, 0))
    out_shape = (
        jax.ShapeDtypeStruct((B, GLA_HEADS, S, GLA_DK), BF16),
        jax.ShapeDtypeStruct((B, GLA_HEADS, S, GLA_DK), BF16),
        jax.ShapeDtypeStruct((B, GLA_HEADS, S, GLA_DV), BF16),
        jax.ShapeDtypeStruct((B, GLA_HEADS, S // GLA_C, GLA_DV, GLA_C), BF16),
        jax.ShapeDtypeStruct((B, GLA_HEADS, S, 2 * GLA_DK), BF16),
        jax.ShapeDtypeStruct((B, S, D), BF16),
        jax.ShapeDtypeStruct((B, S, D), BF16),
        jax.ShapeDtypeStruct((B, ATT_HEADS, S, ATT_HD), BF16),
        jax.ShapeDtypeStruct((B, ATT_KV_HEADS, S, ATT_HD), BF16),
        jax.ShapeDtypeStruct((B, ATT_KV_HEADS, ATT_HD, S), BF16),
    )
    out_specs = (
        hb(GLA_HEADS, GLA_DK), hb(GLA_HEADS, GLA_DK), hb(GLA_HEADS, GLA_DV),
        pl.BlockSpec((1, GLA_HEADS, tm // GLA_C, GLA_DV, GLA_C), lambda b, i: (b, 0, i, 0, 0)),
        hb(GLA_HEADS, 2 * GLA_DK),
        rowb, rowb,
        hb(ATT_HEADS, ATT_HD), hb(ATT_KV_HEADS, ATT_HD),
        pl.BlockSpec((1, ATT_KV_HEADS, ATT_HD, tm), lambda b, i: (b, 0, 0, i)),
    )
    in_specs = [
        rowb, _resident(mixg.shape), _resident(wt.shape), _resident(wgk.shape), _resident(bgk.shape),
        _resident(qn.shape), _resident(kn.shape), _resident(bg.shape), tab, tab, tab,
    ]
    return pl.pallas_call(
        _in_proj_kernel, grid=grid, in_specs=in_specs, out_specs=out_specs, out_shape=out_shape,
        compiler_params=pltpu.CompilerParams(
            dimension_semantics=("parallel", "parallel"), vmem_limit_bytes=VMEM_LIMIT),
        name="in_proj",
    )(x, mixg, wt, wgk, bgk, qn, kn, bg, rc, rs1, rs2)


def _gla_kernel(q_ref, k_ref, v_ref, vt_ref, la_ref, gg_ref, gn_ref, o_ref,
                osum_ref, qi_ref, qd_ring, kd_ring, ks_ring, s_ring, ut_ref, st_ref, e_ref, carry_ref):
    C, dk = GLA_C, GLA_DK
    S = q_ref.shape[2]
    n = S // C

    row = lax.broadcasted_iota(jnp.int32, (C, C), 0)
    col = lax.broadcasted_iota(jnp.int32, (C, C), 1)
    tri = (row >= col).astype(BF16)
    fwd_pair = row >= col

    def rows_of(c):
        return pl.ds(c * C if isinstance(c, int) else pl.multiple_of(c * C, C), C)

    def prefix_sums(c):
        la = la_ref[0, 0, rows_of(c), :]
        return la, _dot(tri, la)

    def decay_factors(c, slot, la, pre):
        rows = rows_of(c)
        tot = pre[C - 1:C, :]
        dist_f = pre[:, :dk]
        dist_b = tot[:, dk:] - pre[:, dk:] + la[:, dk:].astype(F32)
        tot_f, tot_b = tot[:, :dk], tot[:, dk:]
        q = q_ref[0, 0, rows, :].astype(F32)
        k = k_ref[0, 0, rows, :].astype(F32)

        def factors(dist, mid, tt):
            qd = q * jnp.exp(dist - mid)
            kd = k * jnp.exp(mid - dist)
            qi = qd * jnp.exp(mid)
            ks = kd * jnp.exp(tt - mid)
            return qd.astype(BF16), kd.astype(BF16), qi.astype(BF16), ks.astype(BF16)

        qd_f, kd_f, qi_f, ks_f = factors(dist_f, dist_f[C // 2 - 1:C // 2, :], tot_f)
        qd_b, kd_b, qi_b, ks_b = factors(dist_b, dist_b[C // 2:C // 2 + 1, :], tot_b)
        qd_ring[slot] = jnp.concatenate([qd_f, qd_b], axis=1)
        kd_ring[slot, :C, :dk] = kd_f
        kd_ring[slot, C:, dk:] = kd_b
        ks_ring[slot] = jnp.concatenate([ks_f, ks_b], axis=1)
        qi_ref[rows, :] = jnp.concatenate([qi_f, qi_b], axis=1)
        e_ref[c] = jnp.broadcast_to(jnp.exp(tot), e_ref.shape[1:])

    def intra_scores(slot):
        s_fb = _dot_nt(qd_ring[slot], kd_ring[slot])
        s_ring[slot] = jnp.where(fwd_pair, s_fb[:, :C], s_fb[:, C:]).astype(BF16)

    def intra_out(c, slot):
        osum_ref[rows_of(c), :] = _dot(s_ring[slot], v_ref[0, 0, rows_of(c), :])
        ut_ref[c] = _dot(vt_ref[0, 0, c], ks_ring[slot])

    ring = qd_ring.shape[0]
    lag_s, lag_o = GLA_LAG_SCORES, GLA_LAG_SCORES + GLA_LAG_OUT
    unroll = GLA_UNROLL
    assert ring > lag_o and unroll % ring == 0 and (n - lag_o) % unroll == 0
    kd_ring[...] = jnp.zeros_like(kd_ring)

    def step(i, i_mod_ring):
        static = isinstance(i, int)
        live = lambda c: (not static) or 0 <= c < n
        slot = lambda lag: (i_mod_ring - lag) % ring
        if live(i):
            la, pre = prefix_sums(i)
        if live(i - lag_o):
            intra_out(i - lag_o, slot(lag_o))
        if live(i - lag_s):
            intra_scores(slot(lag_s))
        if live(i):
            decay_factors(i, slot(0), la, pre)

    def phase_a(it, carry):
        base = lag_o + it * unroll
        for u in range(unroll):
            step(base + u, (lag_o + u) % ring)
        return carry

    for i in range(lag_o):
        step(i, i % ring)
    if (n - lag_o) // unroll == 1:
        phase_a(0, 0)
    else:
        lax.fori_loop(0, (n - lag_o) // unroll, phase_a, 0)
    for i in range(n, n + lag_o):
        step(i, i % ring)

    carry_ref[...] = jnp.zeros_like(carry_ref)

    def phase_b(i, carry):
        cf, cb = i, n - 1 - i
        st_f = carry_ref[:, :dk]
        st_b = carry_ref[:, dk:]
        st_ref[cf, :, :dk] = st_f.astype(BF16)
        st_ref[cb, :, dk:] = st_b.astype(BF16)
        carry_ref[:, :dk] = st_f * e_ref[cf, 0:1, :dk] + ut_ref[cf, :, :dk]
        carry_ref[:, dk:] = st_b * e_ref[cb, 0:1, dk:] + ut_ref[cb, :, dk:]
        return carry

    lax.fori_loop(0, n, phase_b, 0, unroll=2)

    def phase_c(c, carry):
        rows = rows_of(c)
        o = osum_ref[rows, :] + _dot_nt(qi_ref[rows, :], st_ref[c])
        ms = jnp.mean(o * o, axis=-1, keepdims=True)
        y = o * lax.rsqrt(ms + EPS) * gn_ref[...] * gg_ref[0, rows, :].astype(F32)
        o_ref[0, rows, :] = y.astype(BF16)
        return carry

    lax.fori_loop(0, n, phase_c, 0, unroll=16)


def _gla(gq, gk, gv, gvt, la, gg, gn):
    B, H, S, dk = gq.shape
    dv = gv.shape[-1]
    n = S // GLA_C
    hs = lambda w: pl.BlockSpec((1, 1, S, w), lambda b, h: (b, h, 0, 0))
    cs = pl.BlockSpec((1, S, dv), lambda b, h: (b, 0, h))
    return pl.pallas_call(
        _gla_kernel, grid=(B, H),
        in_specs=[hs(dk), hs(dk), hs(dv), pl.BlockSpec((1, 1, n, dv, GLA_C), lambda b, h: (b, h, 0, 0, 0)),
                  hs(2 * dk), cs, _resident(gn.shape)],
        out_specs=cs,
        out_shape=jax.ShapeDtypeStruct((B, S, H * dv), BF16),
        scratch_shapes=[
            pltpu.VMEM((S, dv), F32),
            pltpu.VMEM((S, 2 * dk), BF16),
            pltpu.VMEM((GLA_RING, GLA_C, 2 * dk), BF16),
            pltpu.VMEM((GLA_RING, 2 * GLA_C, 2 * dk), BF16),
            pltpu.VMEM((GLA_RING, GLA_C, 2 * dk), BF16),
            pltpu.VMEM((GLA_RING, GLA_C, GLA_C), BF16),
            pltpu.VMEM((n, dv, 2 * dk), F32),
            pltpu.VMEM((n, dv, 2 * dk), BF16),
            pltpu.VMEM((n, 8, 2 * dk), F32),
            pltpu.VMEM((dv, 2 * dk), F32),
        ],
        compiler_params=pltpu.CompilerParams(
            dimension_semantics=("parallel", "parallel"), vmem_limit_bytes=VMEM_LIMIT),
        name="gla",
    )(gq, gk, gv, gvt, la, gg, gn)


def _attn_kernel(q_ref, k_ref, vt_ref, *refs, n_cast):
    for w_ref, wb_ref in zip(refs[:n_cast], refs[n_cast + 1:]):
        wb_ref[...] = w_ref[0].astype(BF16)
    o_ref = refs[n_cast]
    _attn_tile(q_ref, k_ref, vt_ref, o_ref)


def _attn_tile(q_ref, k_ref, vt_ref, o_ref):
    skv = k_ref.shape[1]
    nblk = skv // ATT_KVB
    nsub = q_ref.shape[1] // ATT_SUB

    def sub_rows(t):
        return pl.ds(t * ATT_SUB if isinstance(t, int) else pl.multiple_of(t * ATT_SUB, ATT_SUB), ATT_SUB)

    def scores(q, j):
        return _dot_nt(k_ref[0, j * ATT_KVB:(j + 1) * ATT_KVB, :], q)

    def emit(t, acc, den):
        o_ref[0, sub_rows(t), :] = (acc / den).T.astype(BF16)

    in_range = None
    for t in range(nsub):
        q = q_ref[0, sub_rows(t), :]
        acc = jnp.zeros((ATT_HD, ATT_SUB), F32)
        den = jnp.zeros((1, ATT_SUB), F32)
        s_next = scores(q, 0)
        for j in range(nblk):
            s = s_next
            if j + 1 < nblk:
                s_next = scores(q, j + 1)
            p = jnp.exp2(s)
            den = den + jnp.sum(p, axis=0, keepdims=True)
            acc = acc + _dot(vt_ref[0, :, j * ATT_KVB:(j + 1) * ATT_KVB], p.astype(BF16))
        emit(t, acc, den)
        ok = jnp.logical_and(jnp.min(den) >= 2.0 ** -100, jnp.max(den) <= 2.0 ** 100)
        ok = jnp.logical_and(ok, jnp.max(jnp.abs(acc)) < jnp.inf)
        in_range = ok if in_range is None else jnp.logical_and(in_range, ok)

    @pl.when(jnp.logical_not(in_range))
    def _():
        def redo(t, carry):
            q = q_ref[0, sub_rows(t), :]
            m = jnp.full((1, ATT_SUB), -jnp.inf, F32)
            acc = jnp.zeros((ATT_HD, ATT_SUB), F32)
            den = jnp.zeros((1, ATT_SUB), F32)
            for j in range(nblk):
                s = scores(q, j)
                m_new = jnp.maximum(m, jnp.max(s, axis=0, keepdims=True))
                alpha = jnp.exp2(m - m_new)
                p = jnp.exp2(s - m_new)
                den = den * alpha + jnp.sum(p, axis=0, keepdims=True)
                acc = acc * alpha + _dot(vt_ref[0, :, j * ATT_KVB:(j + 1) * ATT_KVB], p.astype(BF16))
                m = m_new
            emit(t, acc, den)
            return carry

        lax.fori_loop(0, nsub, redo, 0)


def _attn(q, k, vt, weights_f32):
    G, NQ, hd = q.shape
    S = k.shape[1]
    ni = NQ // ATT_TQ
    nsteps = G * ni
    w_in_specs, w_out_specs, w_out_shapes = [], [], []
    for w in weights_f32:
        _, rows, cols = w.shape
        share = next(s for s in (1, 2, 4) if rows % (nsteps // s) == 0 and (rows // (nsteps // s)) % 16 == 0)
        slab = rows // (nsteps // share)
        w_in_specs.append(pl.BlockSpec((1, slab, cols), lambda g, i, share=share: (0, (g * ni + i) // share, 0)))
        w_out_specs.append(pl.BlockSpec((slab, cols), lambda g, i, share=share: ((g * ni + i) // share, 0)))
        w_out_shapes.append(jax.ShapeDtypeStruct((rows, cols), BF16))
    out = pl.pallas_call(
        functools.partial(_attn_kernel, n_cast=len(weights_f32)), grid=(G, ni),
        in_specs=[pl.BlockSpec((1, ATT_TQ, hd), lambda g, i: (g, i, 0)),
                  pl.BlockSpec((1, S, hd), lambda g, i: (g, 0, 0)),
                  pl.BlockSpec((1, vt.shape[1], S), lambda g, i: (g, 0, 0))] + w_in_specs,
        out_specs=[pl.BlockSpec((1, ATT_TQ, hd), lambda g, i: (g, i, 0))] + w_out_specs,
        out_shape=[jax.ShapeDtypeStruct((G, NQ, hd), BF16)] + w_out_shapes,
        compiler_params=pltpu.CompilerParams(
            dimension_semantics=("arbitrary", "arbitrary"), vmem_limit_bytes=VMEM_LIMIT),
        name="attn",
    )(q, k, vt, *weights_f32)
    return out[0], out[1:]


def _post_kernel(x_ref, ma_ref, yb_ref, sgb_ref, wo_ref, fng_ref, wgu_ref, wd_ref, fin_ref, o_ref):
    dff = wd_ref.shape[0]
    tm = x_ref.shape[1]
    groups = [slice(r, r + POST_SUB) for r in range(0, tm, POST_SUB)]

    def rms(v, g_ref):
        return v * lax.rsqrt(jnp.mean(v * v, axis=-1, keepdims=True) + EPS) * g_ref[...]

    x1 = []
    for rs in groups:
        yb = jnp.concatenate([yb_ref[0, hh, rs, :] for hh in range(ATT_HEADS)], axis=-1)
        mixed = ma_ref[0, rs, :].astype(F32) + _sigmoid(sgb_ref[0, rs, :].astype(F32)) * yb.astype(F32)
        x1.append(x_ref[0, rs, :] + _dot(mixed.astype(BF16), wo_ref[...]))
    act = []
    for v in x1:
        h2 = rms(v, fng_ref).astype(BF16)
        gt = _dot(h2, wgu_ref[:, :dff])
        up = _dot(h2, wgu_ref[:, dff:])
        act.append((gt / (1.0 + jnp.exp(-gt)) * up).astype(BF16))
    for rs, v, a in zip(groups, x1, act):
        o_ref[0, rs, :] = rms(v + _dot(a, wd_ref[...]), fin_ref)


def _post(x, ma, yb, sgb, wo, fng, wgu, wd, fin):
    B, S, D = x.shape
    tm = POST_TM
    rowb = pl.BlockSpec((1, tm, D), lambda b, i: (b, i, 0))
    return pl.pallas_call(
        _post_kernel, grid=(B, S // tm),
        in_specs=[rowb, rowb,
                  pl.BlockSpec((1, ATT_HEADS, tm, ATT_HD), lambda b, i: (b, 0, i, 0)),
                  rowb, _resident(wo.shape), _resident(fng.shape), _resident(wgu.shape),
                  _resident(wd.shape), _resident(fin.shape)],
        out_specs=rowb,
        out_shape=jax.ShapeDtypeStruct((B, S, D), F32),
        compiler_params=pltpu.CompilerParams(
            dimension_semantics=("parallel", "parallel"), vmem_limit_bytes=VMEM_LIMIT),
        name="post",
    )(x, ma, yb, sgb, wo, fng, wgu, wd, fin)


def _rope_tables(seq_len):
    pos = np.arange(seq_len)
    freqs = ROPE_THETA ** (-np.arange(0, 64, 2, dtype=np.float32) / 64.0)
    ang_r = (pos // GRID_W).astype(np.float32)[:, None] * freqs[None, :]
    ang_c = (pos % GRID_W).astype(np.float32)[:, None] * freqs[None, :]
    zero = np.zeros_like(ang_r)
    rc = np.concatenate([np.cos(ang_r), np.cos(ang_r), np.cos(ang_c), np.cos(ang_c)], axis=1)
    rs1 = np.concatenate([-np.sin(ang_r), zero, -np.sin(ang_c), zero], axis=1)
    rs2 = np.concatenate([zero, np.sin(ang_r), zero, np.sin(ang_c)], axis=1)
    return (jnp.asarray(rc, F32), jnp.asarray(rs1, F32), jnp.asarray(rs2, F32))


def kernel(x, mix_norm, w_in, w_gk_fwd, b_gk_fwd, w_gk_bwd, b_gk_bwd, gla_norm, q_norm, k_norm,
           b_gate, w_o, ffn_norm, w_gate_up, w_down, final_norm):
    B, S, D = x.shape
    assert w_in.shape[0] == 1, "single-layer block"
    assert S % max(IN_TM, GLA_C, POST_TM, ATT_KVB) == 0 and (ATT_GROUP * S) % ATT_TQ == 0

    wt = jnp.swapaxes(w_in[0], 0, 1).astype(BF16)
    nqk = GLA_HEADS * GLA_DK
    wgk = jnp.zeros((LANES, 2 * nqk), F32)
    wgk = wgk.at[:GLA_RANK, :nqk].set(w_gk_fwd[0]).at[GLA_RANK:2 * GLA_RANK, nqk:].set(w_gk_bwd[0])
    wgk = wgk.astype(BF16)
    bgk = jnp.concatenate([b_gk_fwd[0], b_gk_bwd[0]])[None, :]
    rc, rs1, rs2 = _rope_tables(S)

    (gq, gk, gv, gvt, la, gga, sgb, aq, ak, avt) = _in_proj(
        x, mix_norm, wt, wgk, bgk, q_norm, k_norm, b_gate[0], rc, rs1, rs2)

    ma = _gla(gq, gk, gv, gvt, la, gga, gla_norm)

    yb, (wo, wgu, wd) = _attn(aq.reshape(B * ATT_KV_HEADS, ATT_GROUP * S, ATT_HD),
                              ak.reshape(B * ATT_KV_HEADS, S, ATT_HD),
                              avt.reshape(B * ATT_KV_HEADS, ATT_HD, S),
                              (w_o, w_gate_up, w_down))
    yb = yb.reshape(B, ATT_HEADS, S, ATT_HD)

    return _post(x, ma, yb, sgb, wo, ffn_norm, wgu, wd, final_norm[None, :])
```

```python
import functools
import math

import jax
import jax.numpy as jnp
import numpy as np
from jax import lax
from jax.experimental import pallas as pl
from jax.experimental.pallas import tpu as pltpu

F32 = jnp.float32
BF16 = jnp.bfloat16

EPS = 1e-6
GRID_W = 64
ROPE_THETA = 10000.0
GLA_HEADS = 4
GLA_DK = 128
GLA_DV = 256
GLA_RANK = 16
GLA_GATE_NORM = 16.0
D_MODEL = 1024
PK_Q, PK_K, PK_LA, PK_V, PK_W = 0, 128, 256, 512, 768
ATT_HD = 128
ATT_HEADS = 8
ATT_KV_HEADS = 2
ATT_GROUP = ATT_HEADS // ATT_KV_HEADS

LANES = 128
VMEM_LIMIT = 56 * 1024 * 1024

IN_TM = 512
GLA_C = 128
GLA_LAG_SCORES = 2
GLA_LAG_OUT = 2
GLA_RING = 7
GLA_UNROLL = 14
ATT_TQ = 4096
ATT_SUB = 1024
ATT_KVB = 1024
POST_TM = 512
POST_SUB = 256


def _dot(a, b):
    return jnp.dot(a, b, preferred_element_type=F32)


def _dot_nt(a, b):
    return lax.dot_general(a, b, (((1,), (1,)), ((), ())), preferred_element_type=F32)


def _sigmoid(x):
    return 1.0 / (1.0 + jnp.exp(-x))


def _resident(shape):
    nd = len(shape)
    return pl.BlockSpec(shape, lambda *_: (0,) * nd, pipeline_mode=pl.Buffered(1))


SEG = {
    "gq": (0, 512), "gk": (512, 1024), "gv": (1024, 2048), "go": (2048, 3072),
    "r": (3072, 3072 + LANES),
    "aq": (3104, 4128), "akv": (4128, 4640), "ga": (4640, 5664), "gb": (5664, 6688),
}


def _in_proj_kernel(x_ref, mixg_ref, wt_ref, wgk_ref, bgk_ref, qn_ref, kn_ref, bg_ref,
                    rc_ref, rs1_ref, rs2_ref,
                    pk_ref, gvt_ref, gs_ref,
                    aq_ref, ak_ref, avt_ref):
    x = x_ref[0]
    ms = jnp.mean(x * x, axis=-1, keepdims=True)
    h = (x * lax.rsqrt(ms + EPS) * mixg_ref[...]).astype(BF16)

    def mm(name):
        lo, hi = SEG[name]
        return _dot_nt(h, wt_ref[lo:hi, :])


    rc, rs1, rs2 = rc_ref[...], rs1_ref[...], rs2_ref[...]

    def rope_tables(g, scale):
        g = jnp.broadcast_to(g, (8, ATT_HD)) * scale
        return rc * g[0:1], rs1 * pltpu.roll(g, 96, 1)[0:1], rs2 * pltpu.roll(g, 32, 1)[0:1]

    def norm_rope(a, tabs):
        r = lax.rsqrt(jnp.mean(a * a, axis=-1, keepdims=True) + EPS)
        return (a * tabs[0] + pltpu.roll(a, 96, 1) * tabs[1] + pltpu.roll(a, 32, 1) * tabs[2]) * r

    r = mm("r").astype(BF16)

    aq = mm("aq")
    q_tabs = rope_tables(qn_ref[...], (ATT_HD ** -0.5) * math.log2(math.e))
    for hh in range(ATT_HEADS):
        aq_ref[0, hh] = norm_rope(aq[:, hh * ATT_HD:(hh + 1) * ATT_HD], q_tabs).astype(BF16)
    akv = mm("akv")
    k_tabs = rope_tables(kn_ref[...], 1.0)
    for hh in range(ATT_KV_HEADS):
        ak_ref[0, hh] = norm_rope(akv[:, hh * ATT_HD:(hh + 1) * ATT_HD], k_tabs).astype(BF16)
        v = akv[:, 256 + hh * ATT_HD:256 + (hh + 1) * ATT_HD]
        avt_ref[0, hh] = v.T.astype(BF16)

    z = _dot(r, wgk_ref[...]) + bgk_ref[...]
    la = (jnp.minimum(z, 0.0) - jnp.log(1.0 + jnp.exp(-jnp.abs(z)))) * (1.0 / GLA_GATE_NORM)
    nqk = GLA_HEADS * GLA_DK
    for hh in range(GLA_HEADS):
        pk_ref[0, hh, :, PK_LA:PK_V] = jnp.concatenate(
            [la[:, hh * GLA_DK:(hh + 1) * GLA_DK], la[:, nqk + hh * GLA_DK:nqk + (hh + 1) * GLA_DK]],
            axis=1).astype(BF16)

    go = mm("go")
    ga = mm("ga") + bg_ref[0:1, :]
    gs_ref[0, :, :D_MODEL] = (go / ((1.0 + jnp.exp(-go)) * (1.0 + jnp.exp(-ga)))).astype(BF16)

    gb = mm("gb") + bg_ref[1:2, :]
    gs_ref[0, :, D_MODEL:] = gb.astype(BF16)

    gv = mm("gv")
    for hh in range(GLA_HEADS):
        vh = gv[:, hh * GLA_DV:(hh + 1) * GLA_DV]
        pk_ref[0, hh, :, PK_V:] = vh.astype(BF16)
        for j in range(vh.shape[0] // GLA_C):
            gvt_ref[0, hh, j] = vh[j * GLA_C:(j + 1) * GLA_C].T.astype(BF16)
    gq = mm("gq") * (GLA_DK ** -0.5)
    gk = mm("gk")
    for hh in range(GLA_HEADS):
        sl = slice(hh * GLA_DK, (hh + 1) * GLA_DK)
        pk_ref[0, hh, :, PK_Q:PK_K] = gq[:, sl].astype(BF16)
        pk_ref[0, hh, :, PK_K:PK_LA] = gk[:, sl].astype(BF16)


def _in_proj(x, mixg, wt, wgk, bgk, qn, kn, bg, rc, rs1, rs2):
    B, S, D = x.shape
    tm = IN_TM
    grid = (B, S // tm)

    def hb(nh, width):
        return pl.BlockSpec((1, nh, tm, width), lambda b, i: (b, 0, i, 0))

    rowb = pl.BlockSpec((1, tm, D), lambda b, i: (b, i, 0))
    tab = pl.BlockSpec((tm, LANES), lambda b, i: (i, 0))
    out_shape = (
        jax.ShapeDtypeStruct((B, GLA_HEADS, S, PK_W), BF16),
        jax.ShapeDtypeStruct((B, GLA_HEADS, S // GLA_C, GLA_DV, GLA_C), BF16),
        jax.ShapeDtypeStruct((B, S, 2 * D), BF16),
        jax.ShapeDtypeStruct((B, ATT_HEADS, S, ATT_HD), BF16),
        jax.ShapeDtypeStruct((B, ATT_KV_HEADS, S, ATT_HD), BF16),
        jax.ShapeDtypeStruct((B, ATT_KV_HEADS, ATT_HD, S), BF16),
    )
    out_specs = (
        hb(GLA_HEADS, PK_W),
        pl.BlockSpec((1, GLA_HEADS, tm // GLA_C, GLA_DV, GLA_C), lambda b, i: (b, 0, i, 0, 0)),
        pl.BlockSpec((1, tm, 2 * D), lambda b, i: (b, i, 0)),
        hb(ATT_HEADS, ATT_HD), hb(ATT_KV_HEADS, ATT_HD),
        pl.BlockSpec((1, ATT_KV_HEADS, ATT_HD, tm), lambda b, i: (b, 0, 0, i)),
    )
    in_specs = [
        rowb, _resident(mixg.shape), _resident(wt.shape), _resident(wgk.shape), _resident(bgk.shape),
        _resident(qn.shape), _resident(kn.shape), _resident(bg.shape), tab, tab, tab,
    ]
    return pl.pallas_call(
        _in_proj_kernel, grid=grid, in_specs=in_specs, out_specs=out_specs, out_shape=out_shape,
        compiler_params=pltpu.CompilerParams(
            dimension_semantics=("parallel", "parallel"), vmem_limit_bytes=VMEM_LIMIT),
        name="in_proj",
    )(x, mixg, wt, wgk, bgk, qn, kn, bg, rc, rs1, rs2)


def _gla_kernel(pk_ref, vt_ref, gg_ref, gn_ref, o_ref,
                osum_ref, qi_ref, qd_ring, kd_ring, ks_ring, s_ring, ut_ref, st_ref, e_ref, carry_ref):
    C, dk = GLA_C, GLA_DK
    S = pk_ref.shape[2]
    n = S // C

    row = lax.broadcasted_iota(jnp.int32, (C, C), 0)
    col = lax.broadcasted_iota(jnp.int32, (C, C), 1)
    tri = (row >= col).astype(BF16)
    fwd_pair = row >= col

    def rows_of(c):
        return pl.ds(c * C if isinstance(c, int) else pl.multiple_of(c * C, C), C)

    def prefix_sums(c):
        la = pk_ref[0, 0, rows_of(c), PK_LA:PK_V]
        return la, _dot(tri, la)

    def decay_factors(c, slot, la, pre):
        rows = rows_of(c)
        tot = pre[C - 1:C, :]
        dist_f = pre[:, :dk]
        dist_b = tot[:, dk:] - pre[:, dk:] + la[:, dk:].astype(F32)
        tot_f, tot_b = tot[:, :dk], tot[:, dk:]
        q = pk_ref[0, 0, rows, PK_Q:PK_K].astype(F32)
        k = pk_ref[0, 0, rows, PK_K:PK_LA].astype(F32)

        def factors(dist, mid, tt):
            qd = q * jnp.exp(dist - mid)
            kd = k * jnp.exp(mid - dist)
            qi = qd * jnp.exp(mid)
            ks = kd * jnp.exp(tt - mid)
            return qd.astype(BF16), kd.astype(BF16), qi.astype(BF16), ks.astype(BF16)

        qd_f, kd_f, qi_f, ks_f = factors(dist_f, dist_f[C // 2 - 1:C // 2, :], tot_f)
        qd_b, kd_b, qi_b, ks_b = factors(dist_b, dist_b[C // 2:C // 2 + 1, :], tot_b)
        qd_ring[slot] = jnp.concatenate([qd_f, qd_b], axis=1)
        kd_ring[slot, :C, :dk] = kd_f
        kd_ring[slot, C:, dk:] = kd_b
        ks_ring[slot] = jnp.concatenate([ks_f, ks_b], axis=1)
        qi_ref[rows, :] = jnp.concatenate([qi_f, qi_b], axis=1)
        e_ref[c] = jnp.broadcast_to(jnp.exp(tot), e_ref.shape[1:])

    def intra_scores(slot):
        s_fb = _dot_nt(qd_ring[slot], kd_ring[slot])
        s_ring[slot] = jnp.where(fwd_pair, s_fb[:, :C], s_fb[:, C:]).astype(BF16)

    def intra_out(c, slot):
        osum_ref[rows_of(c), :] = _dot(s_ring[slot], pk_ref[0, 0, rows_of(c), PK_V:])
        ut_ref[c] = _dot(vt_ref[0, 0, c], ks_ring[slot])

    ring = qd_ring.shape[0]
    lag_s, lag_o = GLA_LAG_SCORES, GLA_LAG_SCORES + GLA_LAG_OUT
    unroll = GLA_UNROLL
    assert ring > lag_o and unroll % ring == 0 and (n - lag_o) % unroll == 0
    kd_ring[...] = jnp.zeros_like(kd_ring)

    def step(i, i_mod_ring):
        static = isinstance(i, int)
        live = lambda c: (not static) or 0 <= c < n
        slot = lambda lag: (i_mod_ring - lag) % ring
        if live(i):
            la, pre = prefix_sums(i)
        if live(i - lag_o):
            intra_out(i - lag_o, slot(lag_o))
        if live(i - lag_s):
            intra_scores(slot(lag_s))
        if live(i):
            decay_factors(i, slot(0), la, pre)

    def phase_a(it, carry):
        base = lag_o + it * unroll
        for u in range(unroll):
            step(base + u, (lag_o + u) % ring)
        return carry

    for i in range(lag_o):
        step(i, i % ring)
    if (n - lag_o) // unroll == 1:
        phase_a(0, 0)
    else:
        lax.fori_loop(0, (n - lag_o) // unroll, phase_a, 0)
    for i in range(n, n + lag_o):
        step(i, i % ring)

    carry_ref[...] = jnp.zeros_like(carry_ref)

    def phase_b(i, carry):
        cf, cb = i, n - 1 - i
        st_f = carry_ref[:, :dk]
        st_b = carry_ref[:, dk:]
        st_ref[cf, :, :dk] = st_f.astype(BF16)
        st_ref[cb, :, dk:] = st_b.astype(BF16)
        carry_ref[:, :dk] = st_f * e_ref[cf, 0:1, :dk] + ut_ref[cf, :, :dk]
        carry_ref[:, dk:] = st_b * e_ref[cb, 0:1, dk:] + ut_ref[cb, :, dk:]
        return carry

    lax.fori_loop(0, n, phase_b, 0, unroll=2)

    def phase_c(c, carry):
        rows = rows_of(c)
        o = osum_ref[rows, :] + _dot_nt(qi_ref[rows, :], st_ref[c])
        ms = jnp.mean(o * o, axis=-1, keepdims=True)
        y = o * lax.rsqrt(ms + EPS) * gn_ref[...] * gg_ref[0, rows, :].astype(F32)
        o_ref[0, rows, :] = y.astype(BF16)
        return carry

    lax.fori_loop(0, n, phase_c, 0, unroll=16)


def _gla(pk, gvt, gs, gn):
    B, H, S, _ = pk.shape
    dk, dv = GLA_DK, GLA_DV
    n = S // GLA_C
    hs = lambda w: pl.BlockSpec((1, 1, S, w), lambda b, h: (b, h, 0, 0))
    cs = pl.BlockSpec((1, S, dv), lambda b, h: (b, 0, h))
    return pl.pallas_call(
        _gla_kernel, grid=(B, H),
        in_specs=[hs(PK_W), pl.BlockSpec((1, 1, n, dv, GLA_C), lambda b, h: (b, h, 0, 0, 0)),
                  cs, _resident(gn.shape)],
        out_specs=cs,
        out_shape=jax.ShapeDtypeStruct((B, S, H * dv), BF16),
        scratch_shapes=[
            pltpu.VMEM((S, dv), F32),
            pltpu.VMEM((S, 2 * dk), BF16),
            pltpu.VMEM((GLA_RING, GLA_C, 2 * dk), BF16),
            pltpu.VMEM((GLA_RING, 2 * GLA_C, 2 * dk), BF16),
            pltpu.VMEM((GLA_RING, GLA_C, 2 * dk), BF16),
            pltpu.VMEM((GLA_RING, GLA_C, GLA_C), BF16),
            pltpu.VMEM((n, dv, 2 * dk), F32),
            pltpu.VMEM((n, dv, 2 * dk), BF16),
            pltpu.VMEM((n, 8, 2 * dk), F32),
            pltpu.VMEM((dv, 2 * dk), F32),
        ],
        compiler_params=pltpu.CompilerParams(
            dimension_semantics=("parallel", "parallel"), vmem_limit_bytes=VMEM_LIMIT),
        name="gla",
    )(pk, gvt, gs, gn)


def _attn_kernel(q_ref, k_ref, vt_ref, *refs, n_cast):
    for w_ref, wb_ref in zip(refs[:n_cast], refs[n_cast + 1:]):
        wb_ref[...] = w_ref[0].astype(BF16)
    o_ref = refs[n_cast]
    _attn_tile(q_ref, k_ref, vt_ref, o_ref)


def _attn_tile(q_ref, k_ref, vt_ref, o_ref):
    skv = k_ref.shape[1]
    nblk = skv // ATT_KVB
    nsub = q_ref.shape[1] // ATT_SUB

    def sub_rows(t):
        return pl.ds(t * ATT_SUB if isinstance(t, int) else pl.multiple_of(t * ATT_SUB, ATT_SUB), ATT_SUB)

    def scores(q, j):
        return _dot_nt(k_ref[0, j * ATT_KVB:(j + 1) * ATT_KVB, :], q)

    def emit(t, acc, den):
        o_ref[0, sub_rows(t), :] = (acc / den).T.astype(BF16)

    in_range = None
    for t in range(nsub):
        q = q_ref[0, sub_rows(t), :]
        acc = jnp.zeros((ATT_HD, ATT_SUB), F32)
        den = jnp.zeros((1, ATT_SUB), F32)
        s_next = scores(q, 0)
        for j in range(nblk):
            s = s_next
            if j + 1 < nblk:
                s_next = scores(q, j + 1)
            p = jnp.exp2(s)
            den = den + jnp.sum(p, axis=0, keepdims=True)
            acc = acc + _dot(vt_ref[0, :, j * ATT_KVB:(j + 1) * ATT_KVB], p.astype(BF16))
        emit(t, acc, den)
        ok = jnp.logical_and(jnp.min(den) >= 2.0 ** -100, jnp.max(den) <= 2.0 ** 100)
        ok = jnp.logical_and(ok, jnp.max(jnp.abs(acc)) < jnp.inf)
        in_range = ok if in_range is None else jnp.logical_and(in_range, ok)

    @pl.when(jnp.logical_not(in_range))
    def _():
        def redo(t, carry):
            q = q_ref[0, sub_rows(t), :]
            m = jnp.full((1, ATT_SUB), -jnp.inf, F32)
            acc = jnp.zeros((ATT_HD, ATT_SUB), F32)
            den = jnp.zeros((1, ATT_SUB), F32)
            for j in range(nblk):
                s = scores(q, j)
                m_new = jnp.maximum(m, jnp.max(s, axis=0, keepdims=True))
                alpha = jnp.exp2(m - m_new)
                p = jnp.exp2(s - m_new)
                den = den * alpha + jnp.sum(p, axis=0, keepdims=True)
                acc = acc * alpha + _dot(vt_ref[0, :, j * ATT_KVB:(j + 1) * ATT_KVB], p.astype(BF16))
                m = m_new
            emit(t, acc, den)
            return carry

        lax.fori_loop(0, nsub, redo, 0)


def _attn(q, k, vt, weights_f32):
    G, NQ, hd = q.shape
    S = k.shape[1]
    ni = NQ // ATT_TQ
    nsteps = G * ni
    w_in_specs, w_out_specs, w_out_shapes = [], [], []
    for w in weights_f32:
        _, rows, cols = w.shape
        share = next(s for s in (1, 2, 4) if rows % (nsteps // s) == 0 and (rows // (nsteps // s)) % 16 == 0)
        slab = rows // (nsteps // share)
        w_in_specs.append(pl.BlockSpec((1, slab, cols), lambda g, i, share=share: (0, (g * ni + i) // share, 0)))
        w_out_specs.append(pl.BlockSpec((slab, cols), lambda g, i, share=share: ((g * ni + i) // share, 0)))
        w_out_shapes.append(jax.ShapeDtypeStruct((rows, cols), BF16))
    out = pl.pallas_call(
        functools.partial(_attn_kernel, n_cast=len(weights_f32)), grid=(G, ni),
        in_specs=[pl.BlockSpec((1, ATT_TQ, hd), lambda g, i: (g, i, 0)),
                  pl.BlockSpec((1, S, hd), lambda g, i: (g, 0, 0)),
                  pl.BlockSpec((1, vt.shape[1], S), lambda g, i: (g, 0, 0))] + w_in_specs,
        out_specs=[pl.BlockSpec((1, ATT_TQ, hd), lambda g, i: (g, i, 0))] + w_out_specs,
        out_shape=[jax.ShapeDtypeStruct((G, NQ, hd), BF16)] + w_out_shapes,
        compiler_params=pltpu.CompilerParams(
            dimension_semantics=("arbitrary", "arbitrary"), vmem_limit_bytes=VMEM_LIMIT),
        name="attn",
    )(q, k, vt, *weights_f32)
    return out[0], out[1:]


def _post_kernel(x_ref, ma_ref, yb_ref, sgb_ref, wo_ref, fng_ref, wgu_ref, wd_ref, fin_ref, o_ref):
    dff = wd_ref.shape[0]
    tm = x_ref.shape[1]
    groups = [slice(r, r + POST_SUB) for r in range(0, tm, POST_SUB)]

    def rms(v, g_ref):
        return v * lax.rsqrt(jnp.mean(v * v, axis=-1, keepdims=True) + EPS) * g_ref[...]

    x1 = []
    for rs in groups:
        yb = jnp.concatenate([yb_ref[0, hh, rs, :] for hh in range(ATT_HEADS)], axis=-1)
        mixed = ma_ref[0, rs, :].astype(F32) + _sigmoid(sgb_ref[0, rs, :].astype(F32)) * yb.astype(F32)
        x1.append(x_ref[0, rs, :] + _dot(mixed.astype(BF16), wo_ref[...]))
    act = []
    for v in x1:
        h2 = rms(v, fng_ref).astype(BF16)
        gt = _dot(h2, wgu_ref[:, :dff])
        up = _dot(h2, wgu_ref[:, dff:])
        act.append((gt / (1.0 + jnp.exp(-gt)) * up).astype(BF16))
    for rs, v, a in zip(groups, x1, act):
        o_ref[0, rs, :] = rms(v + _dot(a, wd_ref[...]), fin_ref)


def _post(x, ma, yb, sgb, wo, fng, wgu, wd, fin):
    B, S, D = x.shape
    tm = POST_TM
    rowb = pl.BlockSpec((1, tm, D), lambda b, i: (b, i, 0))
    return pl.pallas_call(
        _post_kernel, grid=(B, S // tm),
        in_specs=[rowb, rowb,
                  pl.BlockSpec((1, ATT_HEADS, tm, ATT_HD), lambda b, i: (b, 0, i, 0)),
                  pl.BlockSpec((1, tm, D), lambda b, i: (b, i, 1)),
                  _resident(wo.shape), _resident(fng.shape), _resident(wgu.shape),
                  _resident(wd.shape), _resident(fin.shape)],
        out_specs=rowb,
        out_shape=jax.ShapeDtypeStruct((B, S, D), F32),
        compiler_params=pltpu.CompilerParams(
            dimension_semantics=("parallel", "parallel"), vmem_limit_bytes=VMEM_LIMIT),
        name="post",
    )(x, ma, yb, sgb, wo, fng, wgu, wd, fin)


def _rope_tables(seq_len):
    pos = np.arange(seq_len)
    freqs = ROPE_THETA ** (-np.arange(0, 64, 2, dtype=np.float32) / 64.0)
    ang_r = (pos // GRID_W).astype(np.float32)[:, None] * freqs[None, :]
    ang_c = (pos % GRID_W).astype(np.float32)[:, None] * freqs[None, :]
    zero = np.zeros_like(ang_r)
    rc = np.concatenate([np.cos(ang_r), np.cos(ang_r), np.cos(ang_c), np.cos(ang_c)], axis=1)
    rs1 = np.concatenate([-np.sin(ang_r), zero, -np.sin(ang_c), zero], axis=1)
    rs2 = np.concatenate([zero, np.sin(ang_r), zero, np.sin(ang_c)], axis=1)
    return (jnp.asarray(rc, F32), jnp.asarray(rs1, F32), jnp.asarray(rs2, F32))


def kernel(x, mix_norm, w_in, w_gk_fwd, b_gk_fwd, w_gk_bwd, b_gk_bwd, gla_norm, q_norm, k_norm,
           b_gate, w_o, ffn_norm, w_gate_up, w_down, final_norm):
    B, S, D = x.shape
    assert w_in.shape[0] == 1, "single-layer block"
    assert S % max(IN_TM, GLA_C, POST_TM, ATT_KVB) == 0 and (ATT_GROUP * S) % ATT_TQ == 0

    wt = jnp.swapaxes(w_in[0], 0, 1).astype(BF16)
    nqk = GLA_HEADS * GLA_DK
    wgk = jnp.zeros((LANES, 2 * nqk), F32)
    wgk = wgk.at[:GLA_RANK, :nqk].set(w_gk_fwd[0]).at[GLA_RANK:2 * GLA_RANK, nqk:].set(w_gk_bwd[0])
    wgk = wgk.astype(BF16)
    bgk = jnp.concatenate([b_gk_fwd[0], b_gk_bwd[0]])[None, :]
    rc, rs1, rs2 = _rope_tables(S)

    (pk, gvt, gs, aq, ak, avt) = _in_proj(
        x, mix_norm, wt, wgk, bgk, q_norm, k_norm, b_gate[0], rc, rs1, rs2)

    ma = _gla(pk, gvt, gs, gla_norm)

    yb, (wo, wgu, wd) = _attn(aq.reshape(B * ATT_KV_HEADS, ATT_GROUP * S, ATT_HD),
                              ak.reshape(B * ATT_KV_HEADS, S, ATT_HD),
                              avt.reshape(B * ATT_KV_HEADS, ATT_HD, S),
                              (w_o, w_gate_up, w_down))
    yb = yb.reshape(B, ATT_HEADS, S, ATT_HD)

    return _post(x, ma, yb, gs, wo, ffn_norm, wgu, wd, final_norm[None, :])
```

```python
import functools
import math

import jax
import jax.numpy as jnp
import numpy as np
from jax import lax
from jax.experimental import pallas as pl
from jax.experimental.pallas import tpu as pltpu

F32 = jnp.float32
BF16 = jnp.bfloat16

EPS = 1e-6
GRID_W = 64
ROPE_THETA = 10000.0
GLA_HEADS = 4
GLA_DK = 128
GLA_DV = 256
GLA_RANK = 16
GLA_GATE_NORM = 16.0
D_MODEL = 1024
PK_Q, PK_K, PK_LA, PK_V, PK_W = 0, 128, 256, 512, 768
ATT_HD = 128
ATT_HEADS = 8
ATT_KV_HEADS = 2
ATT_GROUP = ATT_HEADS // ATT_KV_HEADS

LANES = 128
VMEM_LIMIT = 56 * 1024 * 1024

IN_TM = 512
GLA_C = 128
GLA_LAG_SCORES = 2
GLA_LAG_OUT = 2
GLA_RING = 7
GLA_UNROLL = 14
ATT_TQ = 4096
ATT_SUB = 1024
ATT_KVB = 1024
POST_TM = 512
POST_SUB = 256


def _dot(a, b):
    return jnp.dot(a, b, preferred_element_type=F32)


def _dot_nt(a, b):
    return lax.dot_general(a, b, (((1,), (1,)), ((), ())), preferred_element_type=F32)


def _sigmoid(x):
    return 1.0 / (1.0 + jnp.exp(-x))


def _resident(shape):
    nd = len(shape)
    return pl.BlockSpec(shape, lambda *_: (0,) * nd, pipeline_mode=pl.Buffered(1))


SEG = {
    "gq": (0, 512), "gk": (512, 1024), "gv": (1024, 2048), "go": (2048, 3072),
    "r": (3072, 3072 + LANES),
    "ga": (4640, 5664), "gb": (5664, 6688),
}
SEG_QKV = {"aq": (0, 1024), "akv": (1024, 1536)}
W_QKV_ROWS = (3104, 4640)


def _in_proj_kernel(x_ref, mixg_ref, wt_ref, wqkv_ref, wgk_ref, bgk_ref, qn_ref, kn_ref, bg_ref,
                    tc_ref, ts_ref,
                    pk_ref, gvt_ref, gs_ref,
                    aq_ref, ak_ref, avt_ref):
    x = x_ref[0]
    ms = jnp.mean(x * x, axis=-1, keepdims=True)
    h = (x * lax.rsqrt(ms + EPS) * mixg_ref[...]).astype(BF16)

    def mm(name):
        if name in SEG_QKV:
            lo, hi = SEG_QKV[name]
            return _dot_nt(h, wqkv_ref[lo:hi, :])
        lo, hi = SEG[name]
        return _dot_nt(h, wt_ref[lo:hi, :])


    tc, ts = tc_ref[...], ts_ref[...]
    half = ATT_HD // 2

    def rope_tables(g, scale):
        g = jnp.broadcast_to(g, (8, ATT_HD)) * scale
        return tc * g[0:1], ts * pltpu.roll(g, half, 1)[0:1]

    def norm_rope(a, tabs):
        r = lax.rsqrt(jnp.mean(a * a, axis=-1, keepdims=True) + EPS)
        return (a * tabs[0] + pltpu.roll(a, half, 1) * tabs[1]) * r

    r = mm("r").astype(BF16)

    aq = mm("aq")
    q_tabs = rope_tables(qn_ref[...], (ATT_HD ** -0.5) * math.log2(math.e))
    for hh in range(ATT_HEADS):
        aq_ref[0, hh] = norm_rope(aq[:, hh * ATT_HD:(hh + 1) * ATT_HD], q_tabs).astype(BF16)
    akv = mm("akv")
    k_tabs = rope_tables(kn_ref[...], 1.0)
    for hh in range(ATT_KV_HEADS):
        ak_ref[0, hh] = norm_rope(akv[:, hh * ATT_HD:(hh + 1) * ATT_HD], k_tabs).astype(BF16)
        v = akv[:, 256 + hh * ATT_HD:256 + (hh + 1) * ATT_HD]
        avt_ref[0, hh] = v.T.astype(BF16)

    z = _dot(r, wgk_ref[...]) + bgk_ref[...]
    la = (jnp.minimum(z, 0.0) - jnp.log(1.0 + jnp.exp(-jnp.abs(z)))) * (1.0 / GLA_GATE_NORM)
    nqk = GLA_HEADS * GLA_DK
    for hh in range(GLA_HEADS):
        pk_ref[0, hh, :, PK_LA:PK_V] = jnp.concatenate(
            [la[:, hh * GLA_DK:(hh + 1) * GLA_DK], la[:, nqk + hh * GLA_DK:nqk + (hh + 1) * GLA_DK]],
            axis=1).astype(BF16)

    go = mm("go")
    ga = mm("ga") + bg_ref[0:1, :]
    gs_ref[0, :, :D_MODEL] = (go / ((1.0 + jnp.exp(-go)) * (1.0 + jnp.exp(-ga)))).astype(BF16)

    gb = mm("gb") + bg_ref[1:2, :]
    gs_ref[0, :, D_MODEL:] = gb.astype(BF16)

    gv = mm("gv")
    for hh in range(GLA_HEADS):
        vh = gv[:, hh * GLA_DV:(hh + 1) * GLA_DV]
        pk_ref[0, hh, :, PK_V:] = vh.astype(BF16)
        for j in range(vh.shape[0] // GLA_C):
            gvt_ref[0, hh, j] = vh[j * GLA_C:(j + 1) * GLA_C].T.astype(BF16)
    gq = mm("gq") * (GLA_DK ** -0.5)
    gk = mm("gk")
    for hh in range(GLA_HEADS):
        sl = slice(hh * GLA_DK, (hh + 1) * GLA_DK)
        pk_ref[0, hh, :, PK_Q:PK_K] = gq[:, sl].astype(BF16)
        pk_ref[0, hh, :, PK_K:PK_LA] = gk[:, sl].astype(BF16)


def _in_proj(x, mixg, wt, wqkv, wgk, bgk, qn, kn, bg, tc, ts):
    B, S, D = x.shape
    tm = IN_TM
    grid = (B, S // tm)

    def hb(nh, width):
        return pl.BlockSpec((1, nh, tm, width), lambda b, i: (b, 0, i, 0))

    rowb = pl.BlockSpec((1, tm, D), lambda b, i: (b, i, 0))
    tab = pl.BlockSpec((tm, LANES), lambda b, i: (i, 0))
    out_shape = (
        jax.ShapeDtypeStruct((B, GLA_HEADS, S, PK_W), BF16),
        jax.ShapeDtypeStruct((B, GLA_HEADS, S // GLA_C, GLA_DV, GLA_C), BF16),
        jax.ShapeDtypeStruct((B, S, 2 * D), BF16),
        jax.ShapeDtypeStruct((B, ATT_HEADS, S, ATT_HD), BF16),
        jax.ShapeDtypeStruct((B, ATT_KV_HEADS, S, ATT_HD), BF16),
        jax.ShapeDtypeStruct((B, ATT_KV_HEADS, ATT_HD, S), BF16),
    )
    out_specs = (
        hb(GLA_HEADS, PK_W),
        pl.BlockSpec((1, GLA_HEADS, tm // GLA_C, GLA_DV, GLA_C), lambda b, i: (b, 0, i, 0, 0)),
        pl.BlockSpec((1, tm, 2 * D), lambda b, i: (b, i, 0)),
        hb(ATT_HEADS, ATT_HD), hb(ATT_KV_HEADS, ATT_HD),
        pl.BlockSpec((1, ATT_KV_HEADS, ATT_HD, tm), lambda b, i: (b, 0, 0, i)),
    )
    in_specs = [
        rowb, _resident(mixg.shape), _resident(wt.shape), _resident(wqkv.shape), _resident(wgk.shape),
        _resident(bgk.shape), _resident(qn.shape), _resident(kn.shape), _resident(bg.shape), tab, tab,
    ]
    return pl.pallas_call(
        _in_proj_kernel, grid=grid, in_specs=in_specs, out_specs=out_specs, out_shape=out_shape,
        compiler_params=pltpu.CompilerParams(
            dimension_semantics=("parallel", "parallel"), vmem_limit_bytes=VMEM_LIMIT),
        name="in_proj",
    )(x, mixg, wt, wqkv, wgk, bgk, qn, kn, bg, tc, ts)


def _gla_kernel(pk_ref, vt_ref, gg_ref, gn_ref, o_ref,
                osum_ref, qi_ref, qd_ring, kd_ring, ks_ring, s_ring, ut_ref, st_ref, e_ref):
    C, dk = GLA_C, GLA_DK
    S = pk_ref.shape[2]
    n = S // C

    row = lax.broadcasted_iota(jnp.int32, (C, C), 0)
    col = lax.broadcasted_iota(jnp.int32, (C, C), 1)
    tri = (row >= col).astype(BF16)
    fwd_pair = row >= col

    def rows_of(c):
        return pl.ds(c * C if isinstance(c, int) else pl.multiple_of(c * C, C), C)

    def prefix_sums(c):
        la = pk_ref[0, 0, rows_of(c), PK_LA:PK_V]
        return la, _dot(tri, la)

    def decay_factors(c, slot, la, pre):
        rows = rows_of(c)
        tot = pre[C - 1:C, :]
        dist_f = pre[:, :dk]
        dist_b = tot[:, dk:] - pre[:, dk:] + la[:, dk:].astype(F32)
        tot_f, tot_b = tot[:, :dk], tot[:, dk:]
        q = pk_ref[0, 0, rows, PK_Q:PK_K].astype(F32)
        k = pk_ref[0, 0, rows, PK_K:PK_LA].astype(F32)

        def factors(dist, mid, tt):
            qd = q * jnp.exp(dist - mid)
            kd = k * jnp.exp(mid - dist)
            qi = qd * jnp.exp(mid)
            ks = kd * jnp.exp(tt - mid)
            return qd.astype(BF16), kd.astype(BF16), qi.astype(BF16), ks.astype(BF16)

        qd_f, kd_f, qi_f, ks_f = factors(dist_f, dist_f[C // 2 - 1:C // 2, :], tot_f)
        qd_b, kd_b, qi_b, ks_b = factors(dist_b, dist_b[C // 2:C // 2 + 1, :], tot_b)
        qd_ring[slot] = jnp.concatenate([qd_f, qd_b], axis=1)
        kd_ring[slot, :C, :dk] = kd_f
        kd_ring[slot, C:, dk:] = kd_b
        ks_ring[slot] = jnp.concatenate([ks_f, ks_b], axis=1)
        qi_ref[rows, :] = jnp.concatenate([qi_f, qi_b], axis=1)
        e_ref[c] = jnp.broadcast_to(jnp.exp(tot), e_ref.shape[1:])

    def intra_scores(slot):
        s_fb = _dot_nt(qd_ring[slot], kd_ring[slot])
        s_ring[slot] = jnp.where(fwd_pair, s_fb[:, :C], s_fb[:, C:]).astype(BF16)

    def intra_out(c, slot):
        osum_ref[rows_of(c), :] = _dot(s_ring[slot], pk_ref[0, 0, rows_of(c), PK_V:])
        ut_ref[c] = _dot(vt_ref[0, 0, c], ks_ring[slot])

    ring = qd_ring.shape[0]
    lag_s, lag_o = GLA_LAG_SCORES, GLA_LAG_SCORES + GLA_LAG_OUT
    unroll = GLA_UNROLL
    assert ring > lag_o and unroll % ring == 0 and (n - lag_o) % unroll == 0
    kd_ring[...] = jnp.zeros_like(kd_ring)

    def step(i, i_mod_ring):
        static = isinstance(i, int)
        live = lambda c: (not static) or 0 <= c < n
        slot = lambda lag: (i_mod_ring - lag) % ring
        if live(i):
            la, pre = prefix_sums(i)
        if live(i - lag_o):
            intra_out(i - lag_o, slot(lag_o))
        if live(i - lag_s):
            intra_scores(slot(lag_s))
        if live(i):
            decay_factors(i, slot(0), la, pre)

    def phase_a(it, carry):
        base = lag_o + it * unroll
        for u in range(unroll):
            step(base + u, (lag_o + u) % ring)
        return carry

    for i in range(lag_o):
        step(i, i % ring)
    if (n - lag_o) // unroll == 1:
        phase_a(0, 0)
    else:
        lax.fori_loop(0, (n - lag_o) // unroll, phase_a, 0)
    for i in range(n, n + lag_o):
        step(i, i % ring)

    dv_half = ut_ref.shape[1] // 2
    for r0 in (0, dv_half):
        rows_h = slice(r0, r0 + dv_half)

        def phase_b(i, carry, rows_h=rows_h):
            st_f, st_b = carry
            cf, cb = i, n - 1 - i
            st_ref[cf, rows_h, :dk] = st_f.astype(BF16)
            st_ref[cb, rows_h, dk:] = st_b.astype(BF16)
            return (st_f * e_ref[cf, 0:1, :dk] + ut_ref[cf, rows_h, :dk],
                    st_b * e_ref[cb, 0:1, dk:] + ut_ref[cb, rows_h, dk:])

        zero = jnp.zeros((dv_half, dk), F32)
        lax.fori_loop(0, n, phase_b, (zero, zero), unroll=2)

    def phase_c(c, carry):
        rows = rows_of(c)
        o = osum_ref[rows, :] + _dot_nt(qi_ref[rows, :], st_ref[c])
        ms = jnp.mean(o * o, axis=-1, keepdims=True)
        y = o * lax.rsqrt(ms + EPS) * gn_ref[...] * gg_ref[0, rows, :].astype(F32)
        o_ref[0, rows, :] = y.astype(BF16)
        return carry

    lax.fori_loop(0, n, phase_c, 0, unroll=16)


def _gla(pk, gvt, gs, gn):
    B, H, S, _ = pk.shape
    dk, dv = GLA_DK, GLA_DV
    n = S // GLA_C
    hs = lambda w: pl.BlockSpec((1, 1, S, w), lambda b, h: (b, h, 0, 0))
    cs = pl.BlockSpec((1, S, dv), lambda b, h: (b, 0, h))
    return pl.pallas_call(
        _gla_kernel, grid=(B, H),
        in_specs=[hs(PK_W), pl.BlockSpec((1, 1, n, dv, GLA_C), lambda b, h: (b, h, 0, 0, 0)),
                  cs, _resident(gn.shape)],
        out_specs=cs,
        out_shape=jax.ShapeDtypeStruct((B, S, H * dv), BF16),
        scratch_shapes=[
            pltpu.VMEM((S, dv), F32),
            pltpu.VMEM((S, 2 * dk), BF16),
            pltpu.VMEM((GLA_RING, GLA_C, 2 * dk), BF16),
            pltpu.VMEM((GLA_RING, 2 * GLA_C, 2 * dk), BF16),
            pltpu.VMEM((GLA_RING, GLA_C, 2 * dk), BF16),
            pltpu.VMEM((GLA_RING, GLA_C, GLA_C), BF16),
            pltpu.VMEM((n, dv, 2 * dk), F32),
            pltpu.VMEM((n, dv, 2 * dk), BF16),
            pltpu.VMEM((n, 8, 2 * dk), F32),
        ],
        compiler_params=pltpu.CompilerParams(
            dimension_semantics=("parallel", "parallel"), vmem_limit_bytes=VMEM_LIMIT),
        name="gla",
    )(pk, gvt, gs, gn)


def _attn_kernel(q_ref, k_ref, vt_ref, *refs, n_cast):
    for w_ref, wb_ref in zip(refs[:n_cast], refs[n_cast + 1:]):
        wb_ref[...] = w_ref[0].astype(BF16)
    o_ref = refs[n_cast]
    _attn_tile(q_ref, k_ref, vt_ref, o_ref)


def _attn_tile(q_ref, k_ref, vt_ref, o_ref):
    skv = k_ref.shape[1]
    nblk = skv // ATT_KVB
    nsub = q_ref.shape[1] // ATT_SUB

    def sub_rows(t):
        return pl.ds(t * ATT_SUB if isinstance(t, int) else pl.multiple_of(t * ATT_SUB, ATT_SUB), ATT_SUB)

    def scores(q, j):
        return _dot_nt(k_ref[0, j * ATT_KVB:(j + 1) * ATT_KVB, :], q)

    def emit(t, acc, den):
        o_ref[0, sub_rows(t), :] = (acc / den).T.astype(BF16)

    in_range = None
    for t in range(nsub):
        q = q_ref[0, sub_rows(t), :]
        acc = jnp.zeros((ATT_HD, ATT_SUB), F32)
        den = jnp.zeros((1, ATT_SUB), F32)
        s_next = scores(q, 0)
        for j in range(nblk):
            s = s_next
            if j + 1 < nblk:
                s_next = scores(q, j + 1)
            p = jnp.exp2(s)
            den = den + jnp.sum(p, axis=0, keepdims=True)
            acc = acc + _dot(vt_ref[0, :, j * ATT_KVB:(j + 1) * ATT_KVB], p.astype(BF16))
        emit(t, acc, den)
        ok = jnp.logical_and(jnp.min(den) >= 2.0 ** -100, jnp.max(den) <= 2.0 ** 100)
        ok = jnp.logical_and(ok, jnp.max(jnp.abs(acc)) < jnp.inf)
        in_range = ok if in_range is None else jnp.logical_and(in_range, ok)

    @pl.when(jnp.logical_not(in_range))
    def _():
        def redo(t, carry):
            q = q_ref[0, sub_rows(t), :]
            m = jnp.full((1, ATT_SUB), -jnp.inf, F32)
            acc = jnp.zeros((ATT_HD, ATT_SUB), F32)
            den = jnp.zeros((1, ATT_SUB), F32)
            for j in range(nblk):
                s = scores(q, j)
                m_new = jnp.maximum(m, jnp.max(s, axis=0, keepdims=True))
                alpha = jnp.exp2(m - m_new)
                p = jnp.exp2(s - m_new)
                den = den * alpha + jnp.sum(p, axis=0, keepdims=True)
                acc = acc * alpha + _dot(vt_ref[0, :, j * ATT_KVB:(j + 1) * ATT_KVB], p.astype(BF16))
                m = m_new
            emit(t, acc, den)
            return carry

        lax.fori_loop(0, nsub, redo, 0)


def _attn(q, k, vt, weights_f32):
    G, NQ, hd = q.shape
    S = k.shape[1]
    ni = NQ // ATT_TQ
    nsteps = G * ni
    w_in_specs, w_out_specs, w_out_shapes = [], [], []
    for w in weights_f32:
        _, rows, cols = w.shape
        share = next(s for s in (1, 2, 4) if rows % (nsteps // s) == 0 and (rows // (nsteps // s)) % 16 == 0)
        slab = rows // (nsteps // share)
        w_in_specs.append(pl.BlockSpec((1, slab, cols), lambda g, i, share=share: (0, (g * ni + i) // share, 0)))
        w_out_specs.append(pl.BlockSpec((slab, cols), lambda g, i, share=share: ((g * ni + i) // share, 0)))
        w_out_shapes.append(jax.ShapeDtypeStruct((rows, cols), BF16))
    out = pl.pallas_call(
        functools.partial(_attn_kernel, n_cast=len(weights_f32)), grid=(G, ni),
        in_specs=[pl.BlockSpec((1, ATT_TQ, hd), lambda g, i: (g, i, 0)),
                  pl.BlockSpec((1, S, hd), lambda g, i: (g, 0, 0)),
                  pl.BlockSpec((1, vt.shape[1], S), lambda g, i: (g, 0, 0))] + w_in_specs,
        out_specs=[pl.BlockSpec((1, ATT_TQ, hd), lambda g, i: (g, i, 0))] + w_out_specs,
        out_shape=[jax.ShapeDtypeStruct((G, NQ, hd), BF16)] + w_out_shapes,
        compiler_params=pltpu.CompilerParams(
            dimension_semantics=("arbitrary", "arbitrary"), vmem_limit_bytes=VMEM_LIMIT),
        name="attn",
    )(q, k, vt, *weights_f32)
    return out[0], out[1:]


def _post_kernel(x_ref, ma_ref, yb_ref, sgb_ref, wo_ref, fng_ref, wgu_ref, wd_ref, fin_ref, o_ref):
    dff = wd_ref.shape[0]
    tm = x_ref.shape[1]
    groups = [slice(r, r + POST_SUB) for r in range(0, tm, POST_SUB)]

    def rms(v, g_ref):
        return v * lax.rsqrt(jnp.mean(v * v, axis=-1, keepdims=True) + EPS) * g_ref[...]

    x1 = []
    for rs in groups:
        yb = jnp.concatenate([yb_ref[0, hh, rs, :] for hh in range(ATT_HEADS)], axis=-1)
        mixed = ma_ref[0, rs, :].astype(F32) + _sigmoid(sgb_ref[0, rs, :].astype(F32)) * yb.astype(F32)
        x1.append(x_ref[0, rs, :] + _dot(mixed.astype(BF16), wo_ref[...]))
    act = []
    for v in x1:
        h2 = rms(v, fng_ref).astype(BF16)
        gt = _dot(h2, wgu_ref[:, :dff])
        up = _dot(h2, wgu_ref[:, dff:])
        act.append((gt / (1.0 + jnp.exp(-gt)) * up).astype(BF16))
    for rs, v, a in zip(groups, x1, act):
        o_ref[0, rs, :] = rms(v + _dot(a, wd_ref[...]), fin_ref)


def _post(x, ma, yb, sgb, wo, fng, wgu, wd, fin):
    B, S, D = x.shape
    tm = POST_TM
    rowb = pl.BlockSpec((1, tm, D), lambda b, i: (b, i, 0))
    return pl.pallas_call(
        _post_kernel, grid=(B, S // tm),
        in_specs=[rowb, rowb,
                  pl.BlockSpec((1, ATT_HEADS, tm, ATT_HD), lambda b, i: (b, 0, i, 0)),
                  pl.BlockSpec((1, tm, D), lambda b, i: (b, i, 1)),
                  _resident(wo.shape), _resident(fng.shape), _resident(wgu.shape),
                  _resident(wd.shape), _resident(fin.shape)],
        out_specs=rowb,
        out_shape=jax.ShapeDtypeStruct((B, S, D), F32),
        compiler_params=pltpu.CompilerParams(
            dimension_semantics=("parallel", "parallel"), vmem_limit_bytes=VMEM_LIMIT),
        name="post",
    )(x, ma, yb, sgb, wo, fng, wgu, wd, fin)


ROPE_PERM = np.concatenate([np.arange(0, 32), np.arange(64, 96), np.arange(32, 64), np.arange(96, 128)])


def _rope_tables(seq_len):
    pos = np.arange(seq_len)
    freqs = ROPE_THETA ** (-np.arange(0, 64, 2, dtype=np.float32) / 64.0)
    ang_r = (pos // GRID_W).astype(np.float32)[:, None] * freqs[None, :]
    ang_c = (pos % GRID_W).astype(np.float32)[:, None] * freqs[None, :]
    tc = np.concatenate([np.cos(ang_r), np.cos(ang_c), np.cos(ang_r), np.cos(ang_c)], axis=1)
    ts = np.concatenate([-np.sin(ang_r), -np.sin(ang_c), np.sin(ang_r), np.sin(ang_c)], axis=1)
    return jnp.asarray(tc, F32), jnp.asarray(ts, F32)


def _qkv_rows(w_in):
    lo, hi = W_QKV_ROWS
    n_qk = (ATT_HEADS + ATT_KV_HEADS) * ATT_HD
    rows = jnp.swapaxes(w_in[0][:, lo:hi], 0, 1)
    qk = rows[:n_qk].reshape(ATT_HEADS + ATT_KV_HEADS, 2, 2, 32, -1)
    qk = jnp.swapaxes(qk, 1, 2).reshape(n_qk, -1)
    return jnp.concatenate([qk, rows[n_qk:]], axis=0).astype(BF16)


def kernel(x, mix_norm, w_in, w_gk_fwd, b_gk_fwd, w_gk_bwd, b_gk_bwd, gla_norm, q_norm, k_norm,
           b_gate, w_o, ffn_norm, w_gate_up, w_down, final_norm):
    B, S, D = x.shape
    assert w_in.shape[0] == 1, "single-layer block"
    assert S % max(IN_TM, GLA_C, POST_TM, ATT_KVB) == 0 and (ATT_GROUP * S) % ATT_TQ == 0

    wt = jnp.swapaxes(w_in[0], 0, 1).astype(BF16)
    wqkv = _qkv_rows(w_in)
    nqk = GLA_HEADS * GLA_DK
    wgk = jnp.zeros((LANES, 2 * nqk), F32)
    wgk = wgk.at[:GLA_RANK, :nqk].set(w_gk_fwd[0]).at[GLA_RANK:2 * GLA_RANK, nqk:].set(w_gk_bwd[0])
    wgk = wgk.astype(BF16)
    bgk = jnp.concatenate([b_gk_fwd[0], b_gk_bwd[0]])[None, :]
    tc, ts = _rope_tables(S)

    (pk, gvt, gs, aq, ak, avt) = _in_proj(
        x, mix_norm, wt, wqkv, wgk, bgk, q_norm[:, ROPE_PERM], k_norm[:, ROPE_PERM], b_gate[0], tc, ts)

    ma = _gla(pk, gvt, gs, gla_norm)

    yb, (wo, wgu, wd) = _attn(aq.reshape(B * ATT_KV_HEADS, ATT_GROUP * S, ATT_HD),
                              ak.reshape(B * ATT_KV_HEADS, S, ATT_HD),
                              avt.reshape(B * ATT_KV_HEADS, ATT_HD, S),
                              (w_o, w_gate_up, w_down))
    yb = yb.reshape(B, ATT_HEADS, S, ATT_HD)

    return _post(x, ma, yb, gs, wo, ffn_norm, wgu, wd, final_norm[None, :])
```

```python
import functools
import math

import jax
import jax.numpy as jnp
import numpy as np
from jax import lax
from jax.experimental import pallas as pl
from jax.experimental.pallas import tpu as pltpu

F32 = jnp.float32
BF16 = jnp.bfloat16

EPS = 1e-6
GRID_W = 64
ROPE_THETA = 10000.0
GLA_HEADS = 4
GLA_DK = 128
GLA_DV = 256
GLA_RANK = 16
GLA_GATE_NORM = 16.0
D_MODEL = 1024
PK_Q, PK_K, PK_LA, PK_V, PK_W = 0, 128, 256, 512, 768
ATT_HD = 128
ATT_HEADS = 8
ATT_KV_HEADS = 2
ATT_GROUP = ATT_HEADS // ATT_KV_HEADS

LANES = 128
VMEM_LIMIT = 56 * 1024 * 1024

IN_TM = 512
GLA_C = 128
GLA_LAG_SCORES = 2
GLA_LAG_OUT = 2
GLA_RING = 7
GLA_UNROLL = 14
ATT_TQ = 4096
ATT_SUB = 1024
ATT_KVB = 1024
POST_TM = 512
POST_SUB = 256


def _dot(a, b):
    return jnp.dot(a, b, preferred_element_type=F32)


def _dot_nt(a, b):
    return lax.dot_general(a, b, (((1,), (1,)), ((), ())), preferred_element_type=F32)


def _sigmoid(x):
    return 1.0 / (1.0 + jnp.exp(-x))


def _resident(shape):
    nd = len(shape)
    return pl.BlockSpec(shape, lambda *_: (0,) * nd, pipeline_mode=pl.Buffered(1))


SEG = {
    "gq": (0, 512), "gk": (512, 1024), "gv": (1024, 2048), "go": (2048, 3072),
    "r": (3072, 3072 + LANES),
    "aq": (3104, 4128), "akv": (4128, 4640), "ga": (4640, 5664), "gb": (5664, 6688),
}


def _in_proj_kernel(x_ref, mixg_ref, wt_ref, wgk_ref, bgk_ref, kn_ref, bg_ref,
                    rc_ref, rs1_ref, rs2_ref,
                    pk_ref, gvt_ref, gs_ref,
                    aq_ref, ak_ref, avt_ref):
    x = x_ref[0]
    ms = jnp.mean(x * x, axis=-1, keepdims=True)
    h = (x * lax.rsqrt(ms + EPS) * mixg_ref[...]).astype(BF16)

    def mm(name):
        lo, hi = SEG[name]
        return _dot_nt(h, wt_ref[lo:hi, :])


    rc, rs1, rs2 = rc_ref[...], rs1_ref[...], rs2_ref[...]

    def rope_tables(g, scale):
        g = jnp.broadcast_to(g, (8, ATT_HD)) * scale
        return rc * g[0:1], rs1 * pltpu.roll(g, 96, 1)[0:1], rs2 * pltpu.roll(g, 32, 1)[0:1]

    def norm_rope(a, tabs):
        r = lax.rsqrt(jnp.mean(a * a, axis=-1, keepdims=True) + EPS)
        return (a * tabs[0] + pltpu.roll(a, 96, 1) * tabs[1] + pltpu.roll(a, 32, 1) * tabs[2]) * r

    r = mm("r").astype(BF16)

    aq = mm("aq")
    for hh in range(ATT_HEADS):
        aq_ref[0, hh] = aq[:, hh * ATT_HD:(hh + 1) * ATT_HD].astype(BF16)

    go = mm("go")
    ga = mm("ga") + bg_ref[0:1, :]
    gs_ref[0, :, :D_MODEL] = (go / ((1.0 + jnp.exp(-go)) * (1.0 + jnp.exp(-ga)))).astype(BF16)

    akv = mm("akv")
    k_tabs = rope_tables(kn_ref[...], 1.0)
    for hh in range(ATT_KV_HEADS):
        ak_ref[0, hh] = norm_rope(akv[:, hh * ATT_HD:(hh + 1) * ATT_HD], k_tabs).astype(BF16)
        v = akv[:, 256 + hh * ATT_HD:256 + (hh + 1) * ATT_HD]
        avt_ref[0, hh] = v.T.astype(BF16)

    z = _dot(r, wgk_ref[...]) + bgk_ref[...]
    la = (jnp.minimum(z, 0.0) - jnp.log(1.0 + jnp.exp(-jnp.abs(z)))) * (1.0 / GLA_GATE_NORM)
    nqk = GLA_HEADS * GLA_DK
    for hh in range(GLA_HEADS):
        pk_ref[0, hh, :, PK_LA:PK_V] = jnp.concatenate(
            [la[:, hh * GLA_DK:(hh + 1) * GLA_DK], la[:, nqk + hh * GLA_DK:nqk + (hh + 1) * GLA_DK]],
            axis=1).astype(BF16)

    gb = mm("gb") + bg_ref[1:2, :]
    gs_ref[0, :, D_MODEL:] = gb.astype(BF16)

    gv = mm("gv")
    for hh in range(GLA_HEADS):
        vh = gv[:, hh * GLA_DV:(hh + 1) * GLA_DV]
        pk_ref[0, hh, :, PK_V:] = vh.astype(BF16)
        for j in range(vh.shape[0] // GLA_C):
            gvt_ref[0, hh, j] = vh[j * GLA_C:(j + 1) * GLA_C].T.astype(BF16)
    gq = mm("gq") * (GLA_DK ** -0.5)
    gk = mm("gk")
    for hh in range(GLA_HEADS):
        sl = slice(hh * GLA_DK, (hh + 1) * GLA_DK)
        pk_ref[0, hh, :, PK_Q:PK_K] = gq[:, sl].astype(BF16)
        pk_ref[0, hh, :, PK_K:PK_LA] = gk[:, sl].astype(BF16)


def _in_proj(x, mixg, wt, wgk, bgk, kn, bg, rc, rs1, rs2):
    B, S, D = x.shape
    tm = IN_TM
    grid = (B, S // tm)

    def hb(nh, width):
        return pl.BlockSpec((1, nh, tm, width), lambda b, i: (b, 0, i, 0))

    rowb = pl.BlockSpec((1, tm, D), lambda b, i: (b, i, 0))
    tab = pl.BlockSpec((tm, LANES), lambda b, i: (i, 0))
    out_shape = (
        jax.ShapeDtypeStruct((B, GLA_HEADS, S, PK_W), BF16),
        jax.ShapeDtypeStruct((B, GLA_HEADS, S // GLA_C, GLA_DV, GLA_C), BF16),
        jax.ShapeDtypeStruct((B, S, 2 * D), BF16),
        jax.ShapeDtypeStruct((B, ATT_HEADS, S, ATT_HD), BF16),
        jax.ShapeDtypeStruct((B, ATT_KV_HEADS, S, ATT_HD), BF16),
        jax.ShapeDtypeStruct((B, ATT_KV_HEADS, ATT_HD, S), BF16),
    )
    out_specs = (
        hb(GLA_HEADS, PK_W),
        pl.BlockSpec((1, GLA_HEADS, tm // GLA_C, GLA_DV, GLA_C), lambda b, i: (b, 0, i, 0, 0)),
        pl.BlockSpec((1, tm, 2 * D), lambda b, i: (b, i, 0)),
        hb(ATT_HEADS, ATT_HD), hb(ATT_KV_HEADS, ATT_HD),
        pl.BlockSpec((1, ATT_KV_HEADS, ATT_HD, tm), lambda b, i: (b, 0, 0, i)),
    )
    in_specs = [
        rowb, _resident(mixg.shape), _resident(wt.shape), _resident(wgk.shape), _resident(bgk.shape),
        _resident(kn.shape), _resident(bg.shape), tab, tab, tab,
    ]
    return pl.pallas_call(
        _in_proj_kernel, grid=grid, in_specs=in_specs, out_specs=out_specs, out_shape=out_shape,
        compiler_params=pltpu.CompilerParams(
            dimension_semantics=("parallel", "parallel"), vmem_limit_bytes=VMEM_LIMIT),
        name="in_proj",
    )(x, mixg, wt, wgk, bgk, kn, bg, rc, rs1, rs2)


def _gla_kernel(pk_ref, vt_ref, gg_ref, gn_ref, o_ref,
                osum_ref, qi_ref, qd_ring, kd_ring, ks_ring, s_ring, ut_ref, st_ref, e_ref, carry_ref):
    C, dk = GLA_C, GLA_DK
    S = pk_ref.shape[2]
    n = S // C

    row = lax.broadcasted_iota(jnp.int32, (C, C), 0)
    col = lax.broadcasted_iota(jnp.int32, (C, C), 1)
    tri = (row >= col).astype(BF16)
    fwd_pair = row >= col

    def rows_of(c):
        return pl.ds(c * C if isinstance(c, int) else pl.multiple_of(c * C, C), C)

    def prefix_sums(c):
        la = pk_ref[0, 0, rows_of(c), PK_LA:PK_V]
        return la, _dot(tri, la)

    def decay_factors(c, slot, la, pre):
        rows = rows_of(c)
        tot = pre[C - 1:C, :]
        dist_f = pre[:, :dk]
        dist_b = tot[:, dk:] - pre[:, dk:] + la[:, dk:].astype(F32)
        tot_f, tot_b = tot[:, :dk], tot[:, dk:]
        q = pk_ref[0, 0, rows, PK_Q:PK_K].astype(F32)
        k = pk_ref[0, 0, rows, PK_K:PK_LA].astype(F32)

        def factors(dist, mid, tt):
            qd = q * jnp.exp(dist - mid)
            kd = k * jnp.exp(mid - dist)
            qi = qd * jnp.exp(mid)
            ks = kd * jnp.exp(tt - mid)
            return qd.astype(BF16), kd.astype(BF16), qi.astype(BF16), ks.astype(BF16)

        qd_f, kd_f, qi_f, ks_f = factors(dist_f, dist_f[C // 2 - 1:C // 2, :], tot_f)
        qd_b, kd_b, qi_b, ks_b = factors(dist_b, dist_b[C // 2:C // 2 + 1, :], tot_b)
        qd_ring[slot] = jnp.concatenate([qd_f, qd_b], axis=1)
        kd_ring[slot, :C, :dk] = kd_f
        kd_ring[slot, C:, dk:] = kd_b
        ks_ring[slot] = jnp.concatenate([ks_f, ks_b], axis=1)
        qi_ref[rows, :] = jnp.concatenate([qi_f, qi_b], axis=1)
        e_ref[c] = jnp.broadcast_to(jnp.exp(tot), e_ref.shape[1:])

    def intra_scores(slot):
        s_fb = _dot_nt(qd_ring[slot], kd_ring[slot])
        s_ring[slot] = jnp.where(fwd_pair, s_fb[:, :C], s_fb[:, C:]).astype(BF16)

    def intra_out(c, slot):
        osum_ref[rows_of(c), :] = _dot(s_ring[slot], pk_ref[0, 0, rows_of(c), PK_V:])
        ut_ref[c] = _dot(vt_ref[0, 0, c], ks_ring[slot])

    ring = qd_ring.shape[0]
    lag_s, lag_o = GLA_LAG_SCORES, GLA_LAG_SCORES + GLA_LAG_OUT
    unroll = GLA_UNROLL
    assert ring > lag_o and unroll % ring == 0 and (n - lag_o) % unroll == 0
    kd_ring[...] = jnp.zeros_like(kd_ring)

    def step(i, i_mod_ring):
        static = isinstance(i, int)
        live = lambda c: (not static) or 0 <= c < n
        slot = lambda lag: (i_mod_ring - lag) % ring
        if live(i):
            la, pre = prefix_sums(i)
        if live(i - lag_o):
            intra_out(i - lag_o, slot(lag_o))
        if live(i - lag_s):
            intra_scores(slot(lag_s))
        if live(i):
            decay_factors(i, slot(0), la, pre)

    def phase_a(it, carry):
        base = lag_o + it * unroll
        for u in range(unroll):
            step(base + u, (lag_o + u) % ring)
        return carry

    for i in range(lag_o):
        step(i, i % ring)
    if (n - lag_o) // unroll == 1:
        phase_a(0, 0)
    else:
        lax.fori_loop(0, (n - lag_o) // unroll, phase_a, 0)
    for i in range(n, n + lag_o):
        step(i, i % ring)

    carry_ref[...] = jnp.zeros_like(carry_ref)

    def phase_b(i, carry):
        cf, cb = i, n - 1 - i
        st_f = carry_ref[:, :dk]
        st_b = carry_ref[:, dk:]
        st_ref[cf, :, :dk] = st_f.astype(BF16)
        st_ref[cb, :, dk:] = st_b.astype(BF16)
        carry_ref[:, :dk] = st_f * e_ref[cf, 0:1, :dk] + ut_ref[cf, :, :dk]
        carry_ref[:, dk:] = st_b * e_ref[cb, 0:1, dk:] + ut_ref[cb, :, dk:]
        return carry

    lax.fori_loop(0, n, phase_b, 0, unroll=2)

    def phase_c(c, carry):
        rows = rows_of(c)
        o = osum_ref[rows, :] + _dot_nt(qi_ref[rows, :], st_ref[c])
        ms = jnp.mean(o * o, axis=-1, keepdims=True)
        y = o * lax.rsqrt(ms + EPS) * gn_ref[...] * gg_ref[0, rows, :].astype(F32)
        o_ref[0, rows, :] = y.astype(BF16)
        return carry

    lax.fori_loop(0, n, phase_c, 0, unroll=16)


def _gla(pk, gvt, gs, gn):
    B, H, S, _ = pk.shape
    dk, dv = GLA_DK, GLA_DV
    n = S // GLA_C
    hs = lambda w: pl.BlockSpec((1, 1, S, w), lambda b, h: (b, h, 0, 0))
    cs = pl.BlockSpec((1, S, dv), lambda b, h: (b, 0, h))
    return pl.pallas_call(
        _gla_kernel, grid=(B, H),
        in_specs=[hs(PK_W), pl.BlockSpec((1, 1, n, dv, GLA_C), lambda b, h: (b, h, 0, 0, 0)),
                  cs, _resident(gn.shape)],
        out_specs=cs,
        out_shape=jax.ShapeDtypeStruct((B, S, H * dv), BF16),
        scratch_shapes=[
            pltpu.VMEM((S, dv), F32),
            pltpu.VMEM((S, 2 * dk), BF16),
            pltpu.VMEM((GLA_RING, GLA_C, 2 * dk), BF16),
            pltpu.VMEM((GLA_RING, 2 * GLA_C, 2 * dk), BF16),
            pltpu.VMEM((GLA_RING, GLA_C, 2 * dk), BF16),
            pltpu.VMEM((GLA_RING, GLA_C, GLA_C), BF16),
            pltpu.VMEM((n, dv, 2 * dk), F32),
            pltpu.VMEM((n, dv, 2 * dk), BF16),
            pltpu.VMEM((n, 8, 2 * dk), F32),
            pltpu.VMEM((dv, 2 * dk), F32),
        ],
        compiler_params=pltpu.CompilerParams(
            dimension_semantics=("parallel", "parallel"), vmem_limit_bytes=VMEM_LIMIT),
        name="gla",
    )(pk, gvt, gs, gn)


def _attn_kernel(q_ref, k_ref, vt_ref, qn_ref, rc_ref, rs1_ref, rs2_ref, *refs, n_cast):
    for w_ref, wb_ref in zip(refs[:n_cast], refs[n_cast + 1:]):
        wb_ref[...] = w_ref[0].astype(BF16)
    o_ref = refs[n_cast]
    _attn_tile(q_ref, k_ref, vt_ref, qn_ref, (rc_ref, rs1_ref, rs2_ref), o_ref)


def _attn_tile(q_ref, k_ref, vt_ref, qn_ref, rope_refs, o_ref):
    skv = k_ref.shape[1]
    nblk = skv // ATT_KVB
    nsub = q_ref.shape[1] // ATT_SUB

    def sub_rows(t):
        return pl.ds(t * ATT_SUB if isinstance(t, int) else pl.multiple_of(t * ATT_SUB, ATT_SUB), ATT_SUB)

    seq = rope_refs[0].shape[0]
    pos0 = (pl.program_id(1) * q_ref.shape[1]) % seq
    gain = qn_ref[...] * ((ATT_HD ** -0.5) * math.log2(math.e))

    def q_of(t):
        a = q_ref[0, sub_rows(t), :].astype(F32)
        rows = pl.ds(pl.multiple_of(pos0 + t * ATT_SUB, ATT_SUB), ATT_SUB)
        rc, rs1, rs2 = (ref[rows, :] for ref in rope_refs)
        r = lax.rsqrt(jnp.mean(a * a, axis=-1, keepdims=True) + EPS)
        a = a * gain
        return ((a * rc + pltpu.roll(a, 96, 1) * rs1 + pltpu.roll(a, 32, 1) * rs2) * r).astype(BF16)

    def scores(q, j):
        return _dot_nt(k_ref[0, j * ATT_KVB:(j + 1) * ATT_KVB, :], q)

    def emit(t, acc, den):
        o_ref[0, sub_rows(t), :] = (acc / den).T.astype(BF16)

    in_range = None
    for t in range(nsub):
        q = q_of(t)
        acc = jnp.zeros((ATT_HD, ATT_SUB), F32)
        den = jnp.zeros((1, ATT_SUB), F32)
        s_next = scores(q, 0)
        for j in range(nblk):
            s = s_next
            if j + 1 < nblk:
                s_next = scores(q, j + 1)
            p = jnp.exp2(s)
            den = den + jnp.sum(p, axis=0, keepdims=True)
            acc = acc + _dot(vt_ref[0, :, j * ATT_KVB:(j + 1) * ATT_KVB], p.astype(BF16))
        emit(t, acc, den)
        ok = jnp.logical_and(jnp.min(den) >= 2.0 ** -100, jnp.max(den) <= 2.0 ** 100)
        ok = jnp.logical_and(ok, jnp.max(jnp.abs(acc)) < jnp.inf)
        in_range = ok if in_range is None else jnp.logical_and(in_range, ok)

    @pl.when(jnp.logical_not(in_range))
    def _():
        def redo(t, carry):
            q = q_of(t)
            m = jnp.full((1, ATT_SUB), -jnp.inf, F32)
            acc = jnp.zeros((ATT_HD, ATT_SUB), F32)
            den = jnp.zeros((1, ATT_SUB), F32)
            for j in range(nblk):
                s = scores(q, j)
                m_new = jnp.maximum(m, jnp.max(s, axis=0, keepdims=True))
                alpha = jnp.exp2(m - m_new)
                p = jnp.exp2(s - m_new)
                den = den * alpha + jnp.sum(p, axis=0, keepdims=True)
                acc = acc * alpha + _dot(vt_ref[0, :, j * ATT_KVB:(j + 1) * ATT_KVB], p.astype(BF16))
                m = m_new
            emit(t, acc, den)
            return carry

        lax.fori_loop(0, nsub, redo, 0)


def _attn(q, k, vt, qn, rope, weights_f32):
    G, NQ, hd = q.shape
    S = k.shape[1]
    ni = NQ // ATT_TQ
    nsteps = G * ni
    w_in_specs, w_out_specs, w_out_shapes = [], [], []
    for w in weights_f32:
        _, rows, cols = w.shape
        share = next(s for s in (1, 2, 4) if rows % (nsteps // s) == 0 and (rows // (nsteps // s)) % 16 == 0)
        slab = rows // (nsteps // share)
        w_in_specs.append(pl.BlockSpec((1, slab, cols), lambda g, i, share=share: (0, (g * ni + i) // share, 0)))
        w_out_specs.append(pl.BlockSpec((slab, cols), lambda g, i, share=share: ((g * ni + i) // share, 0)))
        w_out_shapes.append(jax.ShapeDtypeStruct((rows, cols), BF16))
    out = pl.pallas_call(
        functools.partial(_attn_kernel, n_cast=len(weights_f32)), grid=(G, ni),
        in_specs=[pl.BlockSpec((1, ATT_TQ, hd), lambda g, i: (g, i, 0)),
                  pl.BlockSpec((1, S, hd), lambda g, i: (g, 0, 0)),
                  pl.BlockSpec((1, vt.shape[1], S), lambda g, i: (g, 0, 0)),
                  _resident(qn.shape)] + [_resident(t.shape) for t in rope] + w_in_specs,
        out_specs=[pl.BlockSpec((1, ATT_TQ, hd), lambda g, i: (g, i, 0))] + w_out_specs,
        out_shape=[jax.ShapeDtypeStruct((G, NQ, hd), BF16)] + w_out_shapes,
        compiler_params=pltpu.CompilerParams(
            dimension_semantics=("arbitrary", "arbitrary"), vmem_limit_bytes=VMEM_LIMIT),
        name="attn",
    )(q, k, vt, qn, *rope, *weights_f32)
    return out[0], out[1:]


def _post_kernel(x_ref, ma_ref, yb_ref, sgb_ref, wo_ref, fng_ref, wgu_ref, wd_ref, fin_ref, o_ref):
    dff = wd_ref.shape[0]
    tm = x_ref.shape[1]
    groups = [slice(r, r + POST_SUB) for r in range(0, tm, POST_SUB)]

    def rms(v, g_ref):
        return v * lax.rsqrt(jnp.mean(v * v, axis=-1, keepdims=True) + EPS) * g_ref[...]

    x1 = []
    for rs in groups:
        yb = jnp.concatenate([yb_ref[0, hh, rs, :] for hh in range(ATT_HEADS)], axis=-1)
        mixed = ma_ref[0, rs, :].astype(F32) + _sigmoid(sgb_ref[0, rs, :].astype(F32)) * yb.astype(F32)
        x1.append(x_ref[0, rs, :] + _dot(mixed.astype(BF16), wo_ref[...]))
    act = []
    for v in x1:
        h2 = rms(v, fng_ref).astype(BF16)
        gt = _dot(h2, wgu_ref[:, :dff])
        up = _dot(h2, wgu_ref[:, dff:])
        act.append((gt / (1.0 + jnp.exp(-gt)) * up).astype(BF16))
    for rs, v, a in zip(groups, x1, act):
        o_ref[0, rs, :] = rms(v + _dot(a, wd_ref[...]), fin_ref)


def _post(x, ma, yb, sgb, wo, fng, wgu, wd, fin):
    B, S, D = x.shape
    tm = POST_TM
    rowb = pl.BlockSpec((1, tm, D), lambda b, i: (b, i, 0))
    return pl.pallas_call(
        _post_kernel, grid=(B, S // tm),
        in_specs=[rowb, rowb,
                  pl.BlockSpec((1, ATT_HEADS, tm, ATT_HD), lambda b, i: (b, 0, i, 0)),
                  pl.BlockSpec((1, tm, D), lambda b, i: (b, i, 1)),
                  _resident(wo.shape), _resident(fng.shape), _resident(wgu.shape),
                  _resident(wd.shape), _resident(fin.shape)],
        out_specs=rowb,
        out_shape=jax.ShapeDtypeStruct((B, S, D), F32),
        compiler_params=pltpu.CompilerParams(
            dimension_semantics=("parallel", "parallel"), vmem_limit_bytes=VMEM_LIMIT),
        name="post",
    )(x, ma, yb, sgb, wo, fng, wgu, wd, fin)


def _rope_tables(seq_len):
    pos = np.arange(seq_len)
    freqs = ROPE_THETA ** (-np.arange(0, 64, 2, dtype=np.float32) / 64.0)
    ang_r = (pos // GRID_W).astype(np.float32)[:, None] * freqs[None, :]
    ang_c = (pos % GRID_W).astype(np.float32)[:, None] * freqs[None, :]
    zero = np.zeros_like(ang_r)
    rc = np.concatenate([np.cos(ang_r), np.cos(ang_r), np.cos(ang_c), np.cos(ang_c)], axis=1)
    rs1 = np.concatenate([-np.sin(ang_r), zero, -np.sin(ang_c), zero], axis=1)
    rs2 = np.concatenate([zero, np.sin(ang_r), zero, np.sin(ang_c)], axis=1)
    return (jnp.asarray(rc, F32), jnp.asarray(rs1, F32), jnp.asarray(rs2, F32))


def kernel(x, mix_norm, w_in, w_gk_fwd, b_gk_fwd, w_gk_bwd, b_gk_bwd, gla_norm, q_norm, k_norm,
           b_gate, w_o, ffn_norm, w_gate_up, w_down, final_norm):
    B, S, D = x.shape
    assert w_in.shape[0] == 1, "single-layer block"
    assert S % max(IN_TM, GLA_C, POST_TM, ATT_KVB) == 0 and (ATT_GROUP * S) % ATT_TQ == 0

    wt = jnp.swapaxes(w_in[0], 0, 1).astype(BF16)
    nqk = GLA_HEADS * GLA_DK
    wgk = jnp.zeros((LANES, 2 * nqk), F32)
    wgk = wgk.at[:GLA_RANK, :nqk].set(w_gk_fwd[0]).at[GLA_RANK:2 * GLA_RANK, nqk:].set(w_gk_bwd[0])
    wgk = wgk.astype(BF16)
    bgk = jnp.concatenate([b_gk_fwd[0], b_gk_bwd[0]])[None, :]
    rc, rs1, rs2 = _rope_tables(S)

    (pk, gvt, gs, aq, ak, avt) = _in_proj(
        x, mix_norm, wt, wgk, bgk, k_norm, b_gate[0], rc, rs1, rs2)

    ma = _gla(pk, gvt, gs, gla_norm)

    yb, (wo, wgu, wd) = _attn(aq.reshape(B * ATT_KV_HEADS, ATT_GROUP * S, ATT_HD),
                              ak.reshape(B * ATT_KV_HEADS, S, ATT_HD),
                              avt.reshape(B * ATT_KV_HEADS, ATT_HD, S),
                              q_norm, (rc, rs1, rs2), (w_o, w_gate_up, w_down))
    yb = yb.reshape(B, ATT_HEADS, S, ATT_HD)

    return _post(x, ma, yb, gs, wo, ffn_norm, wgu, wd, final_norm[None, :])
```

```python
import functools
import math

import jax
import jax.numpy as jnp
import numpy as np
from jax import lax
from jax.experimental import pallas as pl
from jax.experimental.pallas import tpu as pltpu

F32 = jnp.float32
BF16 = jnp.bfloat16

EPS = 1e-6
GRID_W = 64
ROPE_THETA = 10000.0
GLA_HEADS = 4
GLA_DK = 128
GLA_DV = 256
GLA_RANK = 16
GLA_GATE_NORM = 16.0
D_MODEL = 1024
PK_Q, PK_K, PK_LA, PK_V, PK_W = 0, 128, 256, 512, 768
ATT_HD = 128
ATT_HEADS = 8
ATT_KV_HEADS = 2
ATT_GROUP = ATT_HEADS // ATT_KV_HEADS

LANES = 128
VMEM_LIMIT = 56 * 1024 * 1024

IN_TM = 512
GLA_C = 128
GLA_LAG_SCORES = 2
GLA_LAG_OUT = 2
GLA_RING = 7
GLA_UNROLL = 14
ATT_TQ = 4096
ATT_SUB = 1024
ATT_KVB = 1024
POST_TM = 512
POST_SUB = 256


def _dot(a, b):
    return jnp.dot(a, b, preferred_element_type=F32)


def _dot_nt(a, b):
    return lax.dot_general(a, b, (((1,), (1,)), ((), ())), preferred_element_type=F32)


def _sigmoid(x):
    return 1.0 / (1.0 + jnp.exp(-x))


def _resident(shape):
    nd = len(shape)
    return pl.BlockSpec(shape, lambda *_: (0,) * nd, pipeline_mode=pl.Buffered(1))


SEG = {
    "gq": (0, 512), "gk": (512, 1024), "gv": (1024, 2048), "go": (2048, 3072),
    "r": (3072, 3072 + LANES),
    "aq": (3104, 4128), "akv": (4128, 4640), "ga": (4640, 5664), "gb": (5664, 6688),
}


def _in_proj_kernel(x_ref, mixg_ref, wt_ref, wgk_ref, bgk_ref, qn_ref, kn_ref, bg_ref,
                    rc_ref, rs1_ref, rs2_ref,
                    pk_ref, gs_ref,
                    aq_ref, ak_ref, avt_ref):
    x = x_ref[0]
    ms = jnp.mean(x * x, axis=-1, keepdims=True)
    h = (x * lax.rsqrt(ms + EPS) * mixg_ref[...]).astype(BF16)

    def mm(name):
        lo, hi = SEG[name]
        return _dot_nt(h, wt_ref[lo:hi, :])


    rc, rs1, rs2 = rc_ref[...], rs1_ref[...], rs2_ref[...]

    def rope_tables(g, scale):
        g = jnp.broadcast_to(g, (8, ATT_HD)) * scale
        return rc * g[0:1], rs1 * pltpu.roll(g, 96, 1)[0:1], rs2 * pltpu.roll(g, 32, 1)[0:1]

    def norm_rope(a, tabs):
        r = lax.rsqrt(jnp.mean(a * a, axis=-1, keepdims=True) + EPS)
        return (a * tabs[0] + pltpu.roll(a, 96, 1) * tabs[1] + pltpu.roll(a, 32, 1) * tabs[2]) * r

    r = mm("r").astype(BF16)

    aq = mm("aq")
    q_tabs = rope_tables(qn_ref[...], (ATT_HD ** -0.5) * math.log2(math.e))
    for hh in range(ATT_HEADS):
        aq_ref[0, hh] = norm_rope(aq[:, hh * ATT_HD:(hh + 1) * ATT_HD], q_tabs).astype(BF16)

    go = mm("go")
    ga = mm("ga") + bg_ref[0:1, :]
    gs_ref[0, :, :D_MODEL] = (go / ((1.0 + jnp.exp(-go)) * (1.0 + jnp.exp(-ga)))).astype(BF16)

    akv = mm("akv")
    k_tabs = rope_tables(kn_ref[...], 1.0)
    for hh in range(ATT_KV_HEADS):
        ak_ref[0, hh] = norm_rope(akv[:, hh * ATT_HD:(hh + 1) * ATT_HD], k_tabs).astype(BF16)
        v = akv[:, 256 + hh * ATT_HD:256 + (hh + 1) * ATT_HD]
        avt_ref[0, hh] = v.T.astype(BF16)

    z = _dot(r, wgk_ref[...]) + bgk_ref[...]
    la = (jnp.minimum(z, 0.0) - jnp.log(1.0 + jnp.exp(-jnp.abs(z)))) * (1.0 / GLA_GATE_NORM)
    nqk = GLA_HEADS * GLA_DK
    for hh in range(GLA_HEADS):
        pk_ref[0, hh, :, PK_LA:PK_V] = jnp.concatenate(
            [la[:, hh * GLA_DK:(hh + 1) * GLA_DK], la[:, nqk + hh * GLA_DK:nqk + (hh + 1) * GLA_DK]],
            axis=1).astype(BF16)

    gb = mm("gb") + bg_ref[1:2, :]
    gs_ref[0, :, D_MODEL:] = gb.astype(BF16)

    gv = mm("gv")
    for hh in range(GLA_HEADS):
        vh = gv[:, hh * GLA_DV:(hh + 1) * GLA_DV]
        pk_ref[0, hh, :, PK_V:] = vh.astype(BF16)
    gq = mm("gq") * (GLA_DK ** -0.5)
    gk = mm("gk")
    for hh in range(GLA_HEADS):
        sl = slice(hh * GLA_DK, (hh + 1) * GLA_DK)
        pk_ref[0, hh, :, PK_Q:PK_K] = gq[:, sl].astype(BF16)
        pk_ref[0, hh, :, PK_K:PK_LA] = gk[:, sl].astype(BF16)


def _in_proj(x, mixg, wt, wgk, bgk, qn, kn, bg, rc, rs1, rs2):
    B, S, D = x.shape
    tm = IN_TM
    grid = (B, S // tm)

    def hb(nh, width):
        return pl.BlockSpec((1, nh, tm, width), lambda b, i: (b, 0, i, 0))

    rowb = pl.BlockSpec((1, tm, D), lambda b, i: (b, i, 0))
    tab = pl.BlockSpec((tm, LANES), lambda b, i: (i, 0))
    out_shape = (
        jax.ShapeDtypeStruct((B, GLA_HEADS, S, PK_W), BF16),
        jax.ShapeDtypeStruct((B, S, 2 * D), BF16),
        jax.ShapeDtypeStruct((B, ATT_HEADS, S, ATT_HD), BF16),
        jax.ShapeDtypeStruct((B, ATT_KV_HEADS, S, ATT_HD), BF16),
        jax.ShapeDtypeStruct((B, ATT_KV_HEADS, ATT_HD, S), BF16),
    )
    out_specs = (
        hb(GLA_HEADS, PK_W),
        pl.BlockSpec((1, tm, 2 * D), lambda b, i: (b, i, 0)),
        hb(ATT_HEADS, ATT_HD), hb(ATT_KV_HEADS, ATT_HD),
        pl.BlockSpec((1, ATT_KV_HEADS, ATT_HD, tm), lambda b, i: (b, 0, 0, i)),
    )
    in_specs = [
        rowb, _resident(mixg.shape), _resident(wt.shape), _resident(wgk.shape), _resident(bgk.shape),
        _resident(qn.shape), _resident(kn.shape), _resident(bg.shape), tab, tab, tab,
    ]
    return pl.pallas_call(
        _in_proj_kernel, grid=grid, in_specs=in_specs, out_specs=out_specs, out_shape=out_shape,
        compiler_params=pltpu.CompilerParams(
            dimension_semantics=("parallel", "parallel"), vmem_limit_bytes=VMEM_LIMIT),
        name="in_proj",
    )(x, mixg, wt, wgk, bgk, qn, kn, bg, rc, rs1, rs2)


def _gla_kernel(pk_ref, gg_ref, gn_ref, o_ref,
                osum_ref, qi_ref, qd_ring, kd_ring, ks_ring, s_ring, ut_ref, st_ref, e_ref, carry_ref):
    C, dk = GLA_C, GLA_DK
    S = pk_ref.shape[2]
    n = S // C

    row = lax.broadcasted_iota(jnp.int32, (C, C), 0)
    col = lax.broadcasted_iota(jnp.int32, (C, C), 1)
    tri = (row >= col).astype(BF16)
    fwd_pair = row >= col

    def rows_of(c):
        return pl.ds(c * C if isinstance(c, int) else pl.multiple_of(c * C, C), C)

    def prefix_sums(c):
        la = pk_ref[0, 0, rows_of(c), PK_LA:PK_V]
        return la, _dot(tri, la)

    def decay_factors(c, slot, la, pre):
        rows = rows_of(c)
        tot = pre[C - 1:C, :]
        dist_f = pre[:, :dk]
        dist_b = tot[:, dk:] - pre[:, dk:] + la[:, dk:].astype(F32)
        tot_f, tot_b = tot[:, :dk], tot[:, dk:]
        q = pk_ref[0, 0, rows, PK_Q:PK_K].astype(F32)
        k = pk_ref[0, 0, rows, PK_K:PK_LA].astype(F32)

        def factors(dist, mid, tt):
            qd = q * jnp.exp(dist - mid)
            kd = k * jnp.exp(mid - dist)
            qi = qd * jnp.exp(mid)
            ks = kd * jnp.exp(tt - mid)
            return qd.astype(BF16), kd.astype(BF16), qi.astype(BF16), ks.astype(BF16)

        qd_f, kd_f, qi_f, ks_f = factors(dist_f, dist_f[C // 2 - 1:C // 2, :], tot_f)
        qd_b, kd_b, qi_b, ks_b = factors(dist_b, dist_b[C // 2:C // 2 + 1, :], tot_b)
        qd_ring[slot] = jnp.concatenate([qd_f, qd_b], axis=1)
        kd_ring[slot, :C, :dk] = kd_f
        kd_ring[slot, C:, dk:] = kd_b
        ks_ring[slot] = jnp.concatenate([ks_f, ks_b], axis=1)
        qi_ref[rows, :] = jnp.concatenate([qi_f, qi_b], axis=1)
        e_ref[c] = jnp.broadcast_to(jnp.exp(tot), e_ref.shape[1:])

    def intra_scores(slot):
        s_fb = _dot_nt(qd_ring[slot], kd_ring[slot])
        s_ring[slot] = jnp.where(fwd_pair, s_fb[:, :C], s_fb[:, C:]).astype(BF16)

    def intra_out(c, slot):
        v = pk_ref[0, 0, rows_of(c), PK_V:]
        osum_ref[rows_of(c), :] = _dot(s_ring[slot], v)
        ut_ref[c] = _dot(v.T, ks_ring[slot])

    ring = qd_ring.shape[0]
    lag_s, lag_o = GLA_LAG_SCORES, GLA_LAG_SCORES + GLA_LAG_OUT
    unroll = GLA_UNROLL
    assert ring > lag_o and unroll % ring == 0 and (n - lag_o) % unroll == 0
    kd_ring[...] = jnp.zeros_like(kd_ring)

    def step(i, i_mod_ring):
        static = isinstance(i, int)
        live = lambda c: (not static) or 0 <= c < n
        slot = lambda lag: (i_mod_ring - lag) % ring
        if live(i):
            la, pre = prefix_sums(i)
        if live(i - lag_o):
            intra_out(i - lag_o, slot(lag_o))
        if live(i - lag_s):
            intra_scores(slot(lag_s))
        if live(i):
            decay_factors(i, slot(0), la, pre)

    def phase_a(it, carry):
        base = lag_o + it * unroll
        for u in range(unroll):
            step(base + u, (lag_o + u) % ring)
        return carry

    for i in range(lag_o):
        step(i, i % ring)
    if (n - lag_o) // unroll == 1:
        phase_a(0, 0)
    else:
        lax.fori_loop(0, (n - lag_o) // unroll, phase_a, 0)
    for i in range(n, n + lag_o):
        step(i, i % ring)

    carry_ref[...] = jnp.zeros_like(carry_ref)

    def phase_b(i, carry):
        cf, cb = i, n - 1 - i
        st_f = carry_ref[:, :dk]
        st_b = carry_ref[:, dk:]
        st_ref[cf, :, :dk] = st_f.astype(BF16)
        st_ref[cb, :, dk:] = st_b.astype(BF16)
        carry_ref[:, :dk] = st_f * e_ref[cf, 0:1, :dk] + ut_ref[cf, :, :dk]
        carry_ref[:, dk:] = st_b * e_ref[cb, 0:1, dk:] + ut_ref[cb, :, dk:]
        return carry

    lax.fori_loop(0, n, phase_b, 0, unroll=2)

    def phase_c(c, carry):
        rows = rows_of(c)
        o = osum_ref[rows, :] + _dot_nt(qi_ref[rows, :], st_ref[c])
        ms = jnp.mean(o * o, axis=-1, keepdims=True)
        y = o * lax.rsqrt(ms + EPS) * gn_ref[...] * gg_ref[0, rows, :].astype(F32)
        o_ref[0, rows, :] = y.astype(BF16)
        return carry

    lax.fori_loop(0, n, phase_c, 0, unroll=16)


def _gla(pk, gs, gn):
    B, H, S, _ = pk.shape
    dk, dv = GLA_DK, GLA_DV
    n = S // GLA_C
    hs = lambda w: pl.BlockSpec((1, 1, S, w), lambda b, h: (b, h, 0, 0))
    cs = pl.BlockSpec((1, S, dv), lambda b, h: (b, 0, h))
    return pl.pallas_call(
        _gla_kernel, grid=(B, H),
        in_specs=[hs(PK_W), cs, _resident(gn.shape)],
        out_specs=cs,
        out_shape=jax.ShapeDtypeStruct((B, S, H * dv), BF16),
        scratch_shapes=[
            pltpu.VMEM((S, dv), F32),
            pltpu.VMEM((S, 2 * dk), BF16),
            pltpu.VMEM((GLA_RING, GLA_C, 2 * dk), BF16),
            pltpu.VMEM((GLA_RING, 2 * GLA_C, 2 * dk), BF16),
            pltpu.VMEM((GLA_RING, GLA_C, 2 * dk), BF16),
            pltpu.VMEM((GLA_RING, GLA_C, GLA_C), BF16),
            pltpu.VMEM((n, dv, 2 * dk), F32),
            pltpu.VMEM((n, dv, 2 * dk), BF16),
            pltpu.VMEM((n, 8, 2 * dk), F32),
            pltpu.VMEM((dv, 2 * dk), F32),
        ],
        compiler_params=pltpu.CompilerParams(
            dimension_semantics=("parallel", "parallel"), vmem_limit_bytes=VMEM_LIMIT),
        name="gla",
    )(pk, gs, gn)


def _attn_kernel(q_ref, k_ref, vt_ref, *refs, n_cast):
    for w_ref, wb_ref in zip(refs[:n_cast], refs[n_cast + 1:]):
        wb_ref[...] = w_ref[0].astype(BF16)
    o_ref = refs[n_cast]
    _attn_tile(q_ref, k_ref, vt_ref, o_ref)


def _attn_tile(q_ref, k_ref, vt_ref, o_ref):
    skv = k_ref.shape[1]
    nblk = skv // ATT_KVB
    nsub = q_ref.shape[1] // ATT_SUB

    def sub_rows(t):
        return pl.ds(t * ATT_SUB if isinstance(t, int) else pl.multiple_of(t * ATT_SUB, ATT_SUB), ATT_SUB)

    def scores(q, j):
        return _dot_nt(k_ref[0, j * ATT_KVB:(j + 1) * ATT_KVB, :], q)

    def emit(t, acc, den):
        o_ref[0, sub_rows(t), :] = (acc / den).T.astype(BF16)

    in_range = None
    for t in range(nsub):
        q = q_ref[0, sub_rows(t), :]
        acc = jnp.zeros((ATT_HD, ATT_SUB), F32)
        den = jnp.zeros((1, ATT_SUB), F32)
        s_next = scores(q, 0)
        for j in range(nblk):
            s = s_next
            if j + 1 < nblk:
                s_next = scores(q, j + 1)
            p = jnp.exp2(s)
            den = den + jnp.sum(p, axis=0, keepdims=True)
            acc = acc + _dot(vt_ref[0, :, j * ATT_KVB:(j + 1) * ATT_KVB], p.astype(BF16))
        emit(t, acc, den)
        ok = jnp.logical_and(jnp.min(den) >= 2.0 ** -100, jnp.max(den) <= 2.0 ** 100)
        ok = jnp.logical_and(ok, jnp.max(jnp.abs(acc)) < jnp.inf)
        in_range = ok if in_range is None else jnp.logical_and(in_range, ok)

    @pl.when(jnp.logical_not(in_range))
    def _():
        def redo(t, carry):
            q = q_ref[0, sub_rows(t), :]
            m = jnp.full((1, ATT_SUB), -jnp.inf, F32)
            acc = jnp.zeros((ATT_HD, ATT_SUB), F32)
            den = jnp.zeros((1, ATT_SUB), F32)
            for j in range(nblk):
                s = scores(q, j)
                m_new = jnp.maximum(m, jnp.max(s, axis=0, keepdims=True))
                alpha = jnp.exp2(m - m_new)
                p = jnp.exp2(s - m_new)
                den = den * alpha + jnp.sum(p, axis=0, keepdims=True)
                acc = acc * alpha + _dot(vt_ref[0, :, j * ATT_KVB:(j + 1) * ATT_KVB], p.astype(BF16))
                m = m_new
            emit(t, acc, den)
            return carry

        lax.fori_loop(0, nsub, redo, 0)


def _attn(q, k, vt, weights_f32):
    G, NQ, hd = q.shape
    S = k.shape[1]
    ni = NQ // ATT_TQ
    nsteps = G * ni
    w_in_specs, w_out_specs, w_out_shapes = [], [], []
    for w in weights_f32:
        _, rows, cols = w.shape
        share = next(s for s in (1, 2, 4) if rows % (nsteps // s) == 0 and (rows // (nsteps // s)) % 16 == 0)
        slab = rows // (nsteps // share)
        w_in_specs.append(pl.BlockSpec((1, slab, cols), lambda g, i, share=share: (0, (g * ni + i) // share, 0)))
        w_out_specs.append(pl.BlockSpec((slab, cols), lambda g, i, share=share: ((g * ni + i) // share, 0)))
        w_out_shapes.append(jax.ShapeDtypeStruct((rows, cols), BF16))
    out = pl.pallas_call(
        functools.partial(_attn_kernel, n_cast=len(weights_f32)), grid=(G, ni),
        in_specs=[pl.BlockSpec((1, ATT_TQ, hd), lambda g, i: (g, i, 0)),
                  pl.BlockSpec((1, S, hd), lambda g, i: (g, 0, 0)),
                  pl.BlockSpec((1, vt.shape[1], S), lambda g, i: (g, 0, 0))] + w_in_specs,
        out_specs=[pl.BlockSpec((1, ATT_TQ, hd), lambda g, i: (g, i, 0))] + w_out_specs,
        out_shape=[jax.ShapeDtypeStruct((G, NQ, hd), BF16)] + w_out_shapes,
        compiler_params=pltpu.CompilerParams(
            dimension_semantics=("arbitrary", "arbitrary"), vmem_limit_bytes=VMEM_LIMIT),
        name="attn",
    )(q, k, vt, *weights_f32)
    return out[0], out[1:]


def _post_kernel(x_ref, ma_ref, yb_ref, sgb_ref, wo_ref, fng_ref, wgu_ref, wd_ref, fin_ref, o_ref):
    dff = wd_ref.shape[0]
    tm = x_ref.shape[1]
    groups = [slice(r, r + POST_SUB) for r in range(0, tm, POST_SUB)]

    def rms(v, g_ref):
        return v * lax.rsqrt(jnp.mean(v * v, axis=-1, keepdims=True) + EPS) * g_ref[...]

    x1 = []
    for rs in groups:
        yb = jnp.concatenate([yb_ref[0, hh, rs, :] for hh in range(ATT_HEADS)], axis=-1)
        mixed = ma_ref[0, rs, :].astype(F32) + _sigmoid(sgb_ref[0, rs, :].astype(F32)) * yb.astype(F32)
        x1.append(x_ref[0, rs, :] + _dot(mixed.astype(BF16), wo_ref[...]))
    act = []
    for v in x1:
        h2 = rms(v, fng_ref).astype(BF16)
        gt = _dot(h2, wgu_ref[:, :dff])
        up = _dot(h2, wgu_ref[:, dff:])
        act.append((gt / (1.0 + jnp.exp(-gt)) * up).astype(BF16))
    for rs, v, a in zip(groups, x1, act):
        o_ref[0, rs, :] = rms(v + _dot(a, wd_ref[...]), fin_ref)


def _post(x, ma, yb, sgb, wo, fng, wgu, wd, fin):
    B, S, D = x.shape
    tm = POST_TM
    rowb = pl.BlockSpec((1, tm, D), lambda b, i: (b, i, 0))
    return pl.pallas_call(
        _post_kernel, grid=(B, S // tm),
        in_specs=[rowb, rowb,
                  pl.BlockSpec((1, ATT_HEADS, tm, ATT_HD), lambda b, i: (b, 0, i, 0)),
                  pl.BlockSpec((1, tm, D), lambda b, i: (b, i, 1)),
                  _resident(wo.shape), _resident(fng.shape), _resident(wgu.shape),
                  _resident(wd.shape), _resident(fin.shape)],
        out_specs=rowb,
        out_shape=jax.ShapeDtypeStruct((B, S, D), F32),
        compiler_params=pltpu.CompilerParams(
            dimension_semantics=("parallel", "parallel"), vmem_limit_bytes=VMEM_LIMIT),
        name="post",
    )(x, ma, yb, sgb, wo, fng, wgu, wd, fin)


def _rope_tables(seq_len):
    pos = np.arange(seq_len)
    freqs = ROPE_THETA ** (-np.arange(0, 64, 2, dtype=np.float32) / 64.0)
    ang_r = (pos // GRID_W).astype(np.float32)[:, None] * freqs[None, :]
    ang_c = (pos % GRID_W).astype(np.float32)[:, None] * freqs[None, :]
    zero = np.zeros_like(ang_r)
    rc = np.concatenate([np.cos(ang_r), np.cos(ang_r), np.cos(ang_c), np.cos(ang_c)], axis=1)
    rs1 = np.concatenate([-np.sin(ang_r), zero, -np.sin(ang_c), zero], axis=1)
    rs2 = np.concatenate([zero, np.sin(ang_r), zero, np.sin(ang_c)], axis=1)
    return (jnp.asarray(rc, F32), jnp.asarray(rs1, F32), jnp.asarray(rs2, F32))


def kernel(x, mix_norm, w_in, w_gk_fwd, b_gk_fwd, w_gk_bwd, b_gk_bwd, gla_norm, q_norm, k_norm,
           b_gate, w_o, ffn_norm, w_gate_up, w_down, final_norm):
    B, S, D = x.shape
    assert w_in.shape[0] == 1, "single-layer block"
    assert S % max(IN_TM, GLA_C, POST_TM, ATT_KVB) == 0 and (ATT_GROUP * S) % ATT_TQ == 0

    wt = jnp.swapaxes(w_in[0], 0, 1).astype(BF16)
    nqk = GLA_HEADS * GLA_DK
    wgk = jnp.zeros((LANES, 2 * nqk), F32)
    wgk = wgk.at[:GLA_RANK, :nqk].set(w_gk_fwd[0]).at[GLA_RANK:2 * GLA_RANK, nqk:].set(w_gk_bwd[0])
    wgk = wgk.astype(BF16)
    bgk = jnp.concatenate([b_gk_fwd[0], b_gk_bwd[0]])[None, :]
    rc, rs1, rs2 = _rope_tables(S)

    (pk, gs, aq, ak, avt) = _in_proj(
        x, mix_norm, wt, wgk, bgk, q_norm, k_norm, b_gate[0], rc, rs1, rs2)

    ma = _gla(pk, gs, gla_norm)

    yb, (wo, wgu, wd) = _attn(aq.reshape(B * ATT_KV_HEADS, ATT_GROUP * S, ATT_HD),
                              ak.reshape(B * ATT_KV_HEADS, S, ATT_HD),
                              avt.reshape(B * ATT_KV_HEADS, ATT_HD, S),
                              (w_o, w_gate_up, w_down))
    yb = yb.reshape(B, ATT_HEADS, S, ATT_HD)

    return _post(x, ma, yb, gs, wo, ffn_norm, wgu, wd, final_norm[None, :])
```

```python
import functools
import math

import jax
import jax.numpy as jnp
import numpy as np
from jax import lax
from jax.experimental import pallas as pl
from jax.experimental.pallas import tpu as pltpu

F32 = jnp.float32
BF16 = jnp.bfloat16

EPS = 1e-6
GRID_W = 64
ROPE_THETA = 10000.0
GLA_HEADS = 4
GLA_DK = 128
GLA_DV = 256
GLA_RANK = 16
GLA_GATE_NORM = 16.0
D_MODEL = 1024
PK_Q, PK_K, PK_LA, PK_V, PK_W = 0, 128, 256, 512, 768
ATT_HD = 128
ATT_HEADS = 8
ATT_KV_HEADS = 2
ATT_GROUP = ATT_HEADS // ATT_KV_HEADS

LANES = 128
VMEM_LIMIT = 56 * 1024 * 1024

IN_TM = 512
GLA_C = 128
GLA_LAG_SCORES = 2
GLA_LAG_OUT = 2
GLA_RING = 7
GLA_UNROLL = 28
ATT_TQ = 4096
ATT_SUB = 1024
ATT_KVB = 1024
POST_TM = 512
POST_SUB = 256


def _dot(a, b):
    return jnp.dot(a, b, preferred_element_type=F32)


def _dot_nt(a, b):
    return lax.dot_general(a, b, (((1,), (1,)), ((), ())), preferred_element_type=F32)


def _sigmoid(x):
    return 1.0 / (1.0 + jnp.exp(-x))


def _resident(shape):
    nd = len(shape)
    return pl.BlockSpec(shape, lambda *_: (0,) * nd, pipeline_mode=pl.Buffered(1))


def _segments():
    sizes = (("gq", GLA_HEADS * GLA_DK), ("gk", GLA_HEADS * GLA_DK), ("gv", GLA_HEADS * GLA_DV),
             ("go", GLA_HEADS * GLA_DV), ("rf", GLA_RANK), ("rb", GLA_RANK),
             ("aq", ATT_HEADS * ATT_HD), ("ak", ATT_KV_HEADS * ATT_HD), ("av", ATT_KV_HEADS * ATT_HD),
             ("ga", D_MODEL), ("gb", D_MODEL))
    at, pos = {}, 0
    for name, width in sizes:
        at[name] = pos
        pos += width
    return {"gq": (at["gq"], at["gk"]), "gk": (at["gk"], at["gv"]), "gv": (at["gv"], at["go"]),
            "go": (at["go"], at["rf"]), "r": (at["rf"], at["rf"] + LANES), "aq": (at["aq"], at["ak"]),
            "akv": (at["ak"], at["ga"]), "ga": (at["ga"], at["gb"]), "gb": (at["gb"], pos)}


SEG = _segments()
ROPE_PAIR = ATT_HD // 4


def _in_proj_kernel(x_ref, mixg_ref, wt_ref, wgk_ref, bgk_ref, qn_ref, kn_ref, bg_ref,
                    rc_ref, rs1_ref, rs2_ref,
                    pk_ref, gs_ref,
                    aq_ref, ak_ref, avt_ref):
    x = x_ref[0]
    ms = jnp.mean(x * x, axis=-1, keepdims=True)
    h = (x * lax.rsqrt(ms + EPS) * mixg_ref[...]).astype(BF16)

    def mm(name):
        lo, hi = SEG[name]
        return _dot_nt(h, wt_ref[lo:hi, :])


    rc, rs1, rs2 = rc_ref[...], rs1_ref[...], rs2_ref[...]

    def rope_tables(g, scale):
        g = jnp.broadcast_to(g, (8, ATT_HD)) * scale
        return (rc * g[0:1], rs1 * pltpu.roll(g, ATT_HD - ROPE_PAIR, 1)[0:1],
                rs2 * pltpu.roll(g, ROPE_PAIR, 1)[0:1])

    def norm_rope(a, tabs):
        r = lax.rsqrt(jnp.mean(a * a, axis=-1, keepdims=True) + EPS)
        return (a * tabs[0] + pltpu.roll(a, ATT_HD - ROPE_PAIR, 1) * tabs[1]
                + pltpu.roll(a, ROPE_PAIR, 1) * tabs[2]) * r

    r = mm("r").astype(BF16)

    aq = mm("aq")
    q_tabs = rope_tables(qn_ref[...], (ATT_HD ** -0.5) * math.log2(math.e))
    for hh in range(ATT_HEADS):
        aq_ref[0, hh] = norm_rope(aq[:, hh * ATT_HD:(hh + 1) * ATT_HD], q_tabs).astype(BF16)

    go = mm("go")
    ga = mm("ga") + bg_ref[0:1, :]
    gs_ref[0, :, :D_MODEL] = (go / ((1.0 + jnp.exp(-go)) * (1.0 + jnp.exp(-ga)))).astype(BF16)

    akv = mm("akv")
    k_tabs = rope_tables(kn_ref[...], 1.0)
    for hh in range(ATT_KV_HEADS):
        ak_ref[0, hh] = norm_rope(akv[:, hh * ATT_HD:(hh + 1) * ATT_HD], k_tabs).astype(BF16)
        v = akv[:, (ATT_KV_HEADS + hh) * ATT_HD:(ATT_KV_HEADS + hh + 1) * ATT_HD]
        avt_ref[0, hh] = v.T.astype(BF16)

    z = _dot(r, wgk_ref[...]) + bgk_ref[...]
    la = (jnp.minimum(z, 0.0) - jnp.log(1.0 + jnp.exp(-jnp.abs(z)))) * (1.0 / GLA_GATE_NORM)
    nqk = GLA_HEADS * GLA_DK
    for hh in range(GLA_HEADS):
        pk_ref[0, hh, :, PK_LA:PK_V] = jnp.concatenate(
            [la[:, hh * GLA_DK:(hh + 1) * GLA_DK], la[:, nqk + hh * GLA_DK:nqk + (hh + 1) * GLA_DK]],
            axis=1).astype(BF16)

    gb = mm("gb") + bg_ref[1:2, :]
    gs_ref[0, :, D_MODEL:] = gb.astype(BF16)

    gv = mm("gv")
    for hh in range(GLA_HEADS):
        vh = gv[:, hh * GLA_DV:(hh + 1) * GLA_DV]
        pk_ref[0, hh, :, PK_V:] = vh.astype(BF16)
    gq = mm("gq") * (GLA_DK ** -0.5)
    gk = mm("gk")
    for hh in range(GLA_HEADS):
        sl = slice(hh * GLA_DK, (hh + 1) * GLA_DK)
        pk_ref[0, hh, :, PK_Q:PK_K] = gq[:, sl].astype(BF16)
        pk_ref[0, hh, :, PK_K:PK_LA] = gk[:, sl].astype(BF16)


def _in_proj(x, mixg, wt, wgk, bgk, qn, kn, bg, rc, rs1, rs2):
    B, S, D = x.shape
    tm = IN_TM
    grid = (B, S // tm)

    def hb(nh, width):
        return pl.BlockSpec((1, nh, tm, width), lambda b, i: (b, 0, i, 0))

    rowb = pl.BlockSpec((1, tm, D), lambda b, i: (b, i, 0))
    tab = pl.BlockSpec((tm, LANES), lambda b, i: (i, 0))
    out_shape = (
        jax.ShapeDtypeStruct((B, GLA_HEADS, S, PK_W), BF16),
        jax.ShapeDtypeStruct((B, S, 2 * D), BF16),
        jax.ShapeDtypeStruct((B, ATT_HEADS, S, ATT_HD), BF16),
        jax.ShapeDtypeStruct((B, ATT_KV_HEADS, S, ATT_HD), BF16),
        jax.ShapeDtypeStruct((B, ATT_KV_HEADS, ATT_HD, S), BF16),
    )
    out_specs = (
        hb(GLA_HEADS, PK_W),
        pl.BlockSpec((1, tm, 2 * D), lambda b, i: (b, i, 0)),
        hb(ATT_HEADS, ATT_HD), hb(ATT_KV_HEADS, ATT_HD),
        pl.BlockSpec((1, ATT_KV_HEADS, ATT_HD, tm), lambda b, i: (b, 0, 0, i)),
    )
    in_specs = [
        rowb, _resident(mixg.shape), _resident(wt.shape), _resident(wgk.shape), _resident(bgk.shape),
        _resident(qn.shape), _resident(kn.shape), _resident(bg.shape), tab, tab, tab,
    ]
    return pl.pallas_call(
        _in_proj_kernel, grid=grid, in_specs=in_specs, out_specs=out_specs, out_shape=out_shape,
        compiler_params=pltpu.CompilerParams(
            dimension_semantics=("parallel", "parallel"), vmem_limit_bytes=VMEM_LIMIT),
        name="in_proj",
    )(x, mixg, wt, wgk, bgk, qn, kn, bg, rc, rs1, rs2)


def _gla_kernel(pk_ref, gg_ref, gn_ref, o_ref,
                osum_ref, qi_ref, qd_ring, kd_ring, ks_ring, s_ring, ut_ref, st_ref, e_ref, carry_ref):
    C, dk = GLA_C, GLA_DK
    S = pk_ref.shape[2]
    n = S // C

    row = lax.broadcasted_iota(jnp.int32, (C, C), 0)
    col = lax.broadcasted_iota(jnp.int32, (C, C), 1)
    tri = (row >= col).astype(BF16)
    fwd_pair = row >= col

    def rows_of(c):
        return pl.ds(c * C if isinstance(c, int) else pl.multiple_of(c * C, C), C)

    def prefix_sums(c):
        la = pk_ref[0, 0, rows_of(c), PK_LA:PK_V]
        return la, _dot(tri, la)

    def decay_factors(c, slot, la, pre):
        rows = rows_of(c)
        tot = pre[C - 1:C, :]
        dist_f = pre[:, :dk]
        dist_b = tot[:, dk:] - pre[:, dk:] + la[:, dk:].astype(F32)
        tot_f, tot_b = tot[:, :dk], tot[:, dk:]
        q = pk_ref[0, 0, rows, PK_Q:PK_K].astype(F32)
        k = pk_ref[0, 0, rows, PK_K:PK_LA].astype(F32)

        def factors(dist, mid, tt):
            qd = q * jnp.exp(dist - mid)
            kd = k * jnp.exp(mid - dist)
            qi = qd * jnp.exp(mid)
            ks = kd * jnp.exp(tt - mid)
            return qd.astype(BF16), kd.astype(BF16), qi.astype(BF16), ks.astype(BF16)

        qd_f, kd_f, qi_f, ks_f = factors(dist_f, dist_f[C // 2 - 1:C // 2, :], tot_f)
        qd_b, kd_b, qi_b, ks_b = factors(dist_b, dist_b[C // 2:C // 2 + 1, :], tot_b)
        qd_ring[slot] = jnp.concatenate([qd_f, qd_b], axis=1)
        kd_ring[slot, :C, :dk] = kd_f
        kd_ring[slot, C:, dk:] = kd_b
        ks_ring[slot] = jnp.concatenate([ks_f, ks_b], axis=1)
        qi_ref[rows, :] = jnp.concatenate([qi_f, qi_b], axis=1)
        e_ref[c] = jnp.broadcast_to(jnp.exp(tot), e_ref.shape[1:])

    def intra_scores(slot):
        s_fb = _dot_nt(qd_ring[slot], kd_ring[slot])
        s_ring[slot] = jnp.where(fwd_pair, s_fb[:, :C], s_fb[:, C:]).astype(BF16)

    def intra_out(c, slot):
        v = pk_ref[0, 0, rows_of(c), PK_V:]
        osum_ref[rows_of(c), :] = _dot(s_ring[slot], v)
        ut_ref[c] = _dot(v.T, ks_ring[slot])

    ring = qd_ring.shape[0]
    lag_s, lag_o = GLA_LAG_SCORES, GLA_LAG_SCORES + GLA_LAG_OUT
    unroll = GLA_UNROLL
    assert ring > lag_o and unroll % ring == 0 and (n - lag_o) % unroll == 0
    kd_ring[...] = jnp.zeros_like(kd_ring)

    def step(i, i_mod_ring):
        static = isinstance(i, int)
        live = lambda c: (not static) or 0 <= c < n
        slot = lambda lag: (i_mod_ring - lag) % ring
        if live(i):
            la, pre = prefix_sums(i)
        if live(i - lag_o):
            intra_out(i - lag_o, slot(lag_o))
        if live(i - lag_s):
            intra_scores(slot(lag_s))
        if live(i):
            decay_factors(i, slot(0), la, pre)

    def phase_a(it, carry):
        base = lag_o + it * unroll
        for u in range(unroll):
            step(base + u, (lag_o + u) % ring)
        return carry

    for i in range(lag_o):
        step(i, i % ring)
    if (n - lag_o) // unroll == 1:
        phase_a(0, 0)
    else:
        lax.fori_loop(0, (n - lag_o) // unroll, phase_a, 0)
    for i in range(n, n + lag_o):
        step(i, i % ring)

    carry_ref[...] = jnp.zeros_like(carry_ref)

    def phase_b(i, carry):
        cf, cb = i, n - 1 - i
        st_f = carry_ref[:, :dk]
        st_b = carry_ref[:, dk:]
        st_ref[cf, :, :dk] = st_f.astype(BF16)
        st_ref[cb, :, dk:] = st_b.astype(BF16)
        carry_ref[:, :dk] = st_f * e_ref[cf, 0:1, :dk] + ut_ref[cf, :, :dk]
        carry_ref[:, dk:] = st_b * e_ref[cb, 0:1, dk:] + ut_ref[cb, :, dk:]
        return carry

    lax.fori_loop(0, n, phase_b, 0, unroll=4)

    def phase_c(c, carry):
        rows = rows_of(c)
        o = osum_ref[rows, :] + _dot_nt(qi_ref[rows, :], st_ref[c])
        ms = jnp.mean(o * o, axis=-1, keepdims=True)
        y = o * lax.rsqrt(ms + EPS) * gn_ref[...] * gg_ref[0, rows, :].astype(F32)
        o_ref[0, rows, :] = y.astype(BF16)
        return carry

    lax.fori_loop(0, n, phase_c, 0, unroll=16)


def _gla(pk, gs, gn):
    B, H, S, _ = pk.shape
    dk, dv = GLA_DK, GLA_DV
    n = S // GLA_C
    hs = lambda w: pl.BlockSpec((1, 1, S, w), lambda b, h: (b, h, 0, 0))
    cs = pl.BlockSpec((1, S, dv), lambda b, h: (b, 0, h))
    return pl.pallas_call(
        _gla_kernel, grid=(B, H),
        in_specs=[hs(PK_W), cs, _resident(gn.shape)],
        out_specs=cs,
        out_shape=jax.ShapeDtypeStruct((B, S, H * dv), BF16),
        scratch_shapes=[
            pltpu.VMEM((S, dv), F32),
            pltpu.VMEM((S, 2 * dk), BF16),
            pltpu.VMEM((GLA_RING, GLA_C, 2 * dk), BF16),
            pltpu.VMEM((GLA_RING, 2 * GLA_C, 2 * dk), BF16),
            pltpu.VMEM((GLA_RING, GLA_C, 2 * dk), BF16),
            pltpu.VMEM((GLA_RING, GLA_C, GLA_C), BF16),
            pltpu.VMEM((n, dv, 2 * dk), F32),
            pltpu.VMEM((n, dv, 2 * dk), BF16),
            pltpu.VMEM((n, 8, 2 * dk), F32),
            pltpu.VMEM((dv, 2 * dk), F32),
        ],
        compiler_params=pltpu.CompilerParams(
            dimension_semantics=("parallel", "parallel"), vmem_limit_bytes=VMEM_LIMIT),
        name="gla",
    )(pk, gs, gn)


def _attn_kernel(q_ref, k_ref, vt_ref, *refs, n_cast):
    for w_ref, wb_ref in zip(refs[:n_cast], refs[n_cast + 1:]):
        wb_ref[...] = w_ref[0].astype(BF16)
    o_ref = refs[n_cast]
    _attn_tile(q_ref, k_ref, vt_ref, o_ref)


def _attn_tile(q_ref, k_ref, vt_ref, o_ref):
    skv = k_ref.shape[1]
    nblk = skv // ATT_KVB
    nsub = q_ref.shape[1] // ATT_SUB

    def sub_rows(t):
        return pl.ds(t * ATT_SUB if isinstance(t, int) else pl.multiple_of(t * ATT_SUB, ATT_SUB), ATT_SUB)

    def scores(q, j):
        return _dot_nt(k_ref[0, j * ATT_KVB:(j + 1) * ATT_KVB, :], q)

    def emit(t, acc, den):
        o_ref[0, sub_rows(t), :] = (acc / den).T.astype(BF16)

    in_range = None
    for t in range(nsub):
        q = q_ref[0, sub_rows(t), :]
        acc = jnp.zeros((ATT_HD, ATT_SUB), F32)
        den = jnp.zeros((1, ATT_SUB), F32)
        s_next = scores(q, 0)
        for j in range(nblk):
            s = s_next
            if j + 1 < nblk:
                s_next = scores(q, j + 1)
            p = jnp.exp2(s)
            den = den + jnp.sum(p, axis=0, keepdims=True)
            acc = acc + _dot(vt_ref[0, :, j * ATT_KVB:(j + 1) * ATT_KVB], p.astype(BF16))
        emit(t, acc, den)
        ok = jnp.logical_and(jnp.min(den) >= 2.0 ** -100, jnp.max(den) <= 2.0 ** 100)
        ok = jnp.logical_and(ok, jnp.max(jnp.abs(acc)) < jnp.inf)
        in_range = ok if in_range is None else jnp.logical_and(in_range, ok)

    @pl.when(jnp.logical_not(in_range))
    def _():
        def redo(t, carry):
            q = q_ref[0, sub_rows(t), :]
            m = jnp.full((1, ATT_SUB), -jnp.inf, F32)
            acc = jnp.zeros((ATT_HD, ATT_SUB), F32)
            den = jnp.zeros((1, ATT_SUB), F32)
            for j in range(nblk):
                s = scores(q, j)
                m_new = jnp.maximum(m, jnp.max(s, axis=0, keepdims=True))
                alpha = jnp.exp2(m - m_new)
                p = jnp.exp2(s - m_new)
                den = den * alpha + jnp.sum(p, axis=0, keepdims=True)
                acc = acc * alpha + _dot(vt_ref[0, :, j * ATT_KVB:(j + 1) * ATT_KVB], p.astype(BF16))
                m = m_new
            emit(t, acc, den)
            return carry

        lax.fori_loop(0, nsub, redo, 0)


def _attn(q, k, vt, weights_f32):
    G, NQ, hd = q.shape
    S = k.shape[1]
    ni = NQ // ATT_TQ
    nsteps = G * ni
    w_in_specs, w_out_specs, w_out_shapes = [], [], []
    for w in weights_f32:
        _, rows, cols = w.shape
        share = next(s for s in (1, 2, 4) if rows % (nsteps // s) == 0 and (rows // (nsteps // s)) % 16 == 0)
        slab = rows // (nsteps // share)
        w_in_specs.append(pl.BlockSpec((1, slab, cols), lambda g, i, share=share: (0, (g * ni + i) // share, 0)))
        w_out_specs.append(pl.BlockSpec((slab, cols), lambda g, i, share=share: ((g * ni + i) // share, 0)))
        w_out_shapes.append(jax.ShapeDtypeStruct((rows, cols), BF16))
    out = pl.pallas_call(
        functools.partial(_attn_kernel, n_cast=len(weights_f32)), grid=(G, ni),
        in_specs=[pl.BlockSpec((1, ATT_TQ, hd), lambda g, i: (g, i, 0)),
                  pl.BlockSpec((1, S, hd), lambda g, i: (g, 0, 0)),
                  pl.BlockSpec((1, vt.shape[1], S), lambda g, i: (g, 0, 0))] + w_in_specs,
        out_specs=[pl.BlockSpec((1, ATT_TQ, hd), lambda g, i: (g, i, 0))] + w_out_specs,
        out_shape=[jax.ShapeDtypeStruct((G, NQ, hd), BF16)] + w_out_shapes,
        compiler_params=pltpu.CompilerParams(
            dimension_semantics=("arbitrary", "arbitrary"), vmem_limit_bytes=VMEM_LIMIT),
        name="attn",
    )(q, k, vt, *weights_f32)
    return out[0], out[1:]


def _post_kernel(x_ref, ma_ref, yb_ref, sgb_ref, wo_ref, fng_ref, wgu_ref, wd_ref, fin_ref, o_ref):
    dff = wd_ref.shape[0]
    tm = x_ref.shape[1]
    groups = [slice(r, r + POST_SUB) for r in range(0, tm, POST_SUB)]

    def rms(v, g_ref):
        return v * lax.rsqrt(jnp.mean(v * v, axis=-1, keepdims=True) + EPS) * g_ref[...]

    x1 = []
    for rs in groups:
        yb = jnp.concatenate([yb_ref[0, hh, rs, :] for hh in range(ATT_HEADS)], axis=-1)
        mixed = ma_ref[0, rs, :].astype(F32) + _sigmoid(sgb_ref[0, rs, :].astype(F32)) * yb.astype(F32)
        x1.append(x_ref[0, rs, :] + _dot(mixed.astype(BF16), wo_ref[...]))
    act = []
    for v in x1:
        h2 = rms(v, fng_ref).astype(BF16)
        gt = _dot(h2, wgu_ref[:, :dff])
        up = _dot(h2, wgu_ref[:, dff:])
        act.append((gt / (1.0 + jnp.exp(-gt)) * up).astype(BF16))
    for rs, v, a in zip(groups, x1, act):
        o_ref[0, rs, :] = rms(v + _dot(a, wd_ref[...]), fin_ref)


def _post(x, ma, yb, sgb, wo, fng, wgu, wd, fin):
    B, S, D = x.shape
    tm = POST_TM
    rowb = pl.BlockSpec((1, tm, D), lambda b, i: (b, i, 0))
    return pl.pallas_call(
        _post_kernel, grid=(B, S // tm),
        in_specs=[rowb, rowb,
                  pl.BlockSpec((1, ATT_HEADS, tm, ATT_HD), lambda b, i: (b, 0, i, 0)),
                  pl.BlockSpec((1, tm, D), lambda b, i: (b, i, 1)),
                  _resident(wo.shape), _resident(fng.shape), _resident(wgu.shape),
                  _resident(wd.shape), _resident(fin.shape)],
        out_specs=rowb,
        out_shape=jax.ShapeDtypeStruct((B, S, D), F32),
        compiler_params=pltpu.CompilerParams(
            dimension_semantics=("parallel", "parallel"), vmem_limit_bytes=VMEM_LIMIT),
        name="post",
    )(x, ma, yb, sgb, wo, fng, wgu, wd, fin)


def _rope_tables(seq_len):
    pos = np.arange(seq_len)
    freqs = ROPE_THETA ** (-np.arange(0, 64, 2, dtype=np.float32) / 64.0)
    ang_r = (pos // GRID_W).astype(np.float32)[:, None] * freqs[None, :]
    ang_c = (pos % GRID_W).astype(np.float32)[:, None] * freqs[None, :]
    zero = np.zeros_like(ang_r)
    rc = np.concatenate([np.cos(ang_r), np.cos(ang_r), np.cos(ang_c), np.cos(ang_c)], axis=1)
    rs1 = np.concatenate([-np.sin(ang_r), zero, -np.sin(ang_c), zero], axis=1)
    rs2 = np.concatenate([zero, np.sin(ang_r), zero, np.sin(ang_c)], axis=1)
    return (jnp.asarray(rc, F32), jnp.asarray(rs1, F32), jnp.asarray(rs2, F32))


def kernel(x, mix_norm, w_in, w_gk_fwd, b_gk_fwd, w_gk_bwd, b_gk_bwd, gla_norm, q_norm, k_norm,
           b_gate, w_o, ffn_norm, w_gate_up, w_down, final_norm):
    B, S, D = x.shape
    assert w_in.shape[0] == 1, "single-layer block"
    assert S % max(IN_TM, GLA_C, POST_TM, ATT_KVB) == 0 and (ATT_GROUP * S) % ATT_TQ == 0

    wt = jnp.swapaxes(w_in[0], 0, 1).astype(BF16)
    nqk = GLA_HEADS * GLA_DK
    wgk = jnp.zeros((LANES, 2 * nqk), F32)
    wgk = wgk.at[:GLA_RANK, :nqk].set(w_gk_fwd[0]).at[GLA_RANK:2 * GLA_RANK, nqk:].set(w_gk_bwd[0])
    wgk = wgk.astype(BF16)
    bgk = jnp.concatenate([b_gk_fwd[0], b_gk_bwd[0]])[None, :]
    rc, rs1, rs2 = _rope_tables(S)

    (pk, gs, aq, ak, avt) = _in_proj(
        x, mix_norm, wt, wgk, bgk, q_norm, k_norm, b_gate[0], rc, rs1, rs2)

    ma = _gla(pk, gs, gla_norm)

    yb, (wo, wgu, wd) = _attn(aq.reshape(B * ATT_KV_HEADS, ATT_GROUP * S, ATT_HD),
                              ak.reshape(B * ATT_KV_HEADS, S, ATT_HD),
                              avt.reshape(B * ATT_KV_HEADS, ATT_HD, S),
                              (w_o, w_gate_up, w_down))
    yb = yb.reshape(B, ATT_HEADS, S, ATT_HD)

    return _post(x, ma, yb, gs, wo, ffn_norm, wgu, wd, final_norm[None, :])
```

```python
import functools
import math

import jax
import jax.numpy as jnp
import numpy as np
from jax import lax
from jax.experimental import pallas as pl
from jax.experimental.pallas import tpu as pltpu

F32 = jnp.float32
BF16 = jnp.bfloat16

EPS = 1e-6
GRID_W = 64
ROPE_THETA = 10000.0
GLA_HEADS = 4
GLA_DK = 128
GLA_DV = 256
GLA_RANK = 16
GLA_GATE_NORM = 16.0
D_MODEL = 1024
PK_Q, PK_K, PK_LA, PK_V, PK_W = 0, 128, 256, 512, 768
ATT_HD = 128
ATT_HEADS = 8
ATT_KV_HEADS = 2
ATT_GROUP = ATT_HEADS // ATT_KV_HEADS

LANES = 128
VMEM_LIMIT = 56 * 1024 * 1024

IN_TM = 512
GLA_C = 128
GLA_LAG_SCORES = 2
GLA_LAG_OUT = 2
GLA_RING = 7
GLA_UNROLL = 28
ATT_TQ = 4096
ATT_SUB = 1024
ATT_KVB = 1024
ATT_KV_EDGE = 256
POST_TM = 512
POST_SUB = 256


def _dot(a, b):
    return jnp.dot(a, b, preferred_element_type=F32)


def _dot_nt(a, b):
    return lax.dot_general(a, b, (((1,), (1,)), ((), ())), preferred_element_type=F32)


def _sigmoid(x):
    return 1.0 / (1.0 + jnp.exp(-x))


def _resident(shape):
    nd = len(shape)
    return pl.BlockSpec(shape, lambda *_: (0,) * nd, pipeline_mode=pl.Buffered(1))


def _segments():
    sizes = (("gq", GLA_HEADS * GLA_DK), ("gk", GLA_HEADS * GLA_DK), ("gv", GLA_HEADS * GLA_DV),
             ("go", GLA_HEADS * GLA_DV), ("rf", GLA_RANK), ("rb", GLA_RANK),
             ("aq", ATT_HEADS * ATT_HD), ("ak", ATT_KV_HEADS * ATT_HD), ("av", ATT_KV_HEADS * ATT_HD),
             ("ga", D_MODEL), ("gb", D_MODEL))
    at, pos = {}, 0
    for name, width in sizes:
        at[name] = pos
        pos += width
    return {"gq": (at["gq"], at["gk"]), "gk": (at["gk"], at["gv"]), "gv": (at["gv"], at["go"]),
            "go": (at["go"], at["rf"]), "r": (at["rf"], at["rf"] + LANES), "aq": (at["aq"], at["ak"]),
            "akv": (at["ak"], at["ga"]), "ga": (at["ga"], at["gb"]), "gb": (at["gb"], pos)}


SEG = _segments()
ROPE_PAIR = ATT_HD // 4


def _in_proj_kernel(x_ref, mixg_ref, wt_ref, wgk_ref, bgk_ref, qn_ref, kn_ref, bg_ref,
                    rc_ref, rs1_ref, rs2_ref,
                    pk_ref, gs_ref,
                    aq_ref, ak_ref, avt_ref):
    x = x_ref[0]
    ms = jnp.mean(x * x, axis=-1, keepdims=True)
    h = (x * lax.rsqrt(ms + EPS) * mixg_ref[...]).astype(BF16)

    def mm(name):
        lo, hi = SEG[name]
        return _dot_nt(h, wt_ref[lo:hi, :])


    rc, rs1, rs2 = rc_ref[...], rs1_ref[...], rs2_ref[...]

    def rope_tables(g, scale):
        g = jnp.broadcast_to(g, (8, ATT_HD)) * scale
        return (rc * g[0:1], rs1 * pltpu.roll(g, ATT_HD - ROPE_PAIR, 1)[0:1],
                rs2 * pltpu.roll(g, ROPE_PAIR, 1)[0:1])

    def norm_rope(a, tabs):
        r = lax.rsqrt(jnp.mean(a * a, axis=-1, keepdims=True) + EPS)
        return (a * tabs[0] + pltpu.roll(a, ATT_HD - ROPE_PAIR, 1) * tabs[1]
                + pltpu.roll(a, ROPE_PAIR, 1) * tabs[2]) * r

    r = mm("r").astype(BF16)

    aq = mm("aq")
    q_tabs = rope_tables(qn_ref[...], (ATT_HD ** -0.5) * math.log2(math.e))
    for hh in range(ATT_HEADS):
        aq_ref[0, hh] = norm_rope(aq[:, hh * ATT_HD:(hh + 1) * ATT_HD], q_tabs).astype(BF16)

    go = mm("go")
    ga = mm("ga") + bg_ref[0:1, :]
    gs_ref[0, :, :D_MODEL] = (go / ((1.0 + jnp.exp(-go)) * (1.0 + jnp.exp(-ga)))).astype(BF16)

    akv = mm("akv")
    k_tabs = rope_tables(kn_ref[...], 1.0)
    for hh in range(ATT_KV_HEADS):
        ak_ref[0, hh] = norm_rope(akv[:, hh * ATT_HD:(hh + 1) * ATT_HD], k_tabs).astype(BF16)
        v = akv[:, (ATT_KV_HEADS + hh) * ATT_HD:(ATT_KV_HEADS + hh + 1) * ATT_HD]
        avt_ref[0, hh] = v.T.astype(BF16)

    z = _dot(r, wgk_ref[...]) + bgk_ref[...]
    la = (jnp.minimum(z, 0.0) - jnp.log(1.0 + jnp.exp(-jnp.abs(z)))) * (1.0 / GLA_GATE_NORM)
    nqk = GLA_HEADS * GLA_DK
    for hh in range(GLA_HEADS):
        pk_ref[0, hh, :, PK_LA:PK_V] = jnp.concatenate(
            [la[:, hh * GLA_DK:(hh + 1) * GLA_DK], la[:, nqk + hh * GLA_DK:nqk + (hh + 1) * GLA_DK]],
            axis=1).astype(BF16)

    gb = mm("gb") + bg_ref[1:2, :]
    gs_ref[0, :, D_MODEL:] = gb.astype(BF16)

    gv = mm("gv")
    for hh in range(GLA_HEADS):
        vh = gv[:, hh * GLA_DV:(hh + 1) * GLA_DV]
        pk_ref[0, hh, :, PK_V:] = vh.astype(BF16)
    gq = mm("gq") * (GLA_DK ** -0.5)
    gk = mm("gk")
    for hh in range(GLA_HEADS):
        sl = slice(hh * GLA_DK, (hh + 1) * GLA_DK)
        pk_ref[0, hh, :, PK_Q:PK_K] = gq[:, sl].astype(BF16)
        pk_ref[0, hh, :, PK_K:PK_LA] = gk[:, sl].astype(BF16)


def _in_proj(x, mixg, wt, wgk, bgk, qn, kn, bg, rc, rs1, rs2):
    B, S, D = x.shape
    tm = IN_TM
    grid = (B, S // tm)

    def hb(nh, width):
        return pl.BlockSpec((1, nh, tm, width), lambda b, i: (b, 0, i, 0))

    rowb = pl.BlockSpec((1, tm, D), lambda b, i: (b, i, 0))
    tab = pl.BlockSpec((tm, LANES), lambda b, i: (i, 0))
    out_shape = (
        jax.ShapeDtypeStruct((B, GLA_HEADS, S, PK_W), BF16),
        jax.ShapeDtypeStruct((B, S, 2 * D), BF16),
        jax.ShapeDtypeStruct((B, ATT_HEADS, S, ATT_HD), BF16),
        jax.ShapeDtypeStruct((B, ATT_KV_HEADS, S, ATT_HD), BF16),
        jax.ShapeDtypeStruct((B, ATT_KV_HEADS, ATT_HD, S), BF16),
    )
    out_specs = (
        hb(GLA_HEADS, PK_W),
        pl.BlockSpec((1, tm, 2 * D), lambda b, i: (b, i, 0)),
        hb(ATT_HEADS, ATT_HD), hb(ATT_KV_HEADS, ATT_HD),
        pl.BlockSpec((1, ATT_KV_HEADS, ATT_HD, tm), lambda b, i: (b, 0, 0, i)),
    )
    in_specs = [
        rowb, _resident(mixg.shape), _resident(wt.shape), _resident(wgk.shape), _resident(bgk.shape),
        _resident(qn.shape), _resident(kn.shape), _resident(bg.shape), tab, tab, tab,
    ]
    return pl.pallas_call(
        _in_proj_kernel, grid=grid, in_specs=in_specs, out_specs=out_specs, out_shape=out_shape,
        compiler_params=pltpu.CompilerParams(
            dimension_semantics=("parallel", "parallel"), vmem_limit_bytes=VMEM_LIMIT),
        name="in_proj",
    )(x, mixg, wt, wgk, bgk, qn, kn, bg, rc, rs1, rs2)


def _gla_kernel(pk_ref, gg_ref, gn_ref, o_ref,
                osum_ref, qi_ref, qd_ring, kd_ring, ks_ring, s_ring, ut_ref, st_ref, e_ref, carry_ref):
    C, dk = GLA_C, GLA_DK
    S = pk_ref.shape[2]
    n = S // C

    row = lax.broadcasted_iota(jnp.int32, (C, C), 0)
    col = lax.broadcasted_iota(jnp.int32, (C, C), 1)
    tri = (row >= col).astype(BF16)
    fwd_pair = row >= col

    def rows_of(c):
        return pl.ds(c * C if isinstance(c, int) else pl.multiple_of(c * C, C), C)

    def prefix_sums(c):
        la = pk_ref[0, 0, rows_of(c), PK_LA:PK_V]
        return la, _dot(tri, la)

    def decay_factors(c, slot, la, pre):
        rows = rows_of(c)
        tot = pre[C - 1:C, :]
        dist_f = pre[:, :dk]
        dist_b = tot[:, dk:] - pre[:, dk:] + la[:, dk:].astype(F32)
        tot_f, tot_b = tot[:, :dk], tot[:, dk:]
        q = pk_ref[0, 0, rows, PK_Q:PK_K].astype(F32)
        k = pk_ref[0, 0, rows, PK_K:PK_LA].astype(F32)

        def factors(dist, mid, tt):
            qd = q * jnp.exp(dist - mid)
            kd = k * jnp.exp(mid - dist)
            qi = qd * jnp.exp(mid)
            ks = kd * jnp.exp(tt - mid)
            return qd.astype(BF16), kd.astype(BF16), qi.astype(BF16), ks.astype(BF16)

        qd_f, kd_f, qi_f, ks_f = factors(dist_f, dist_f[C // 2 - 1:C // 2, :], tot_f)
        qd_b, kd_b, qi_b, ks_b = factors(dist_b, dist_b[C // 2:C // 2 + 1, :], tot_b)
        qd_ring[slot] = jnp.concatenate([qd_f, qd_b], axis=1)
        kd_ring[slot, :C, :dk] = kd_f
        kd_ring[slot, C:, dk:] = kd_b
        ks_ring[slot] = jnp.concatenate([ks_f, ks_b], axis=1)
        qi_ref[rows, :] = jnp.concatenate([qi_f, qi_b], axis=1)
        e_ref[c] = jnp.broadcast_to(jnp.exp(tot), e_ref.shape[1:])

    def intra_scores(slot):
        s_fb = _dot_nt(qd_ring[slot], kd_ring[slot])
        s_ring[slot] = jnp.where(fwd_pair, s_fb[:, :C], s_fb[:, C:]).astype(BF16)

    def intra_out(c, slot):
        v = pk_ref[0, 0, rows_of(c), PK_V:]
        osum_ref[rows_of(c), :] = _dot(s_ring[slot], v)
        ut_ref[c] = _dot(v.T, ks_ring[slot])

    ring = qd_ring.shape[0]
    lag_s, lag_o = GLA_LAG_SCORES, GLA_LAG_SCORES + GLA_LAG_OUT
    unroll = GLA_UNROLL
    assert ring > lag_o and unroll % ring == 0 and (n - lag_o) % unroll == 0
    kd_ring[...] = jnp.zeros_like(kd_ring)

    def step(i, i_mod_ring):
        static = isinstance(i, int)
        live = lambda c: (not static) or 0 <= c < n
        slot = lambda lag: (i_mod_ring - lag) % ring
        if live(i):
            la, pre = prefix_sums(i)
        if live(i - lag_o):
            intra_out(i - lag_o, slot(lag_o))
        if live(i - lag_s):
            intra_scores(slot(lag_s))
        if live(i):
            decay_factors(i, slot(0), la, pre)

    def phase_a(it, carry):
        base = lag_o + it * unroll
        for u in range(unroll):
            step(base + u, (lag_o + u) % ring)
        return carry

    for i in range(lag_o):
        step(i, i % ring)
    if (n - lag_o) // unroll == 1:
        phase_a(0, 0)
    else:
        lax.fori_loop(0, (n - lag_o) // unroll, phase_a, 0)
    for i in range(n, n + lag_o):
        step(i, i % ring)

    carry_ref[...] = jnp.zeros_like(carry_ref)

    def phase_b(i, carry):
        cf, cb = i, n - 1 - i
        st_f = carry_ref[:, :dk]
        st_b = carry_ref[:, dk:]
        st_ref[cf, :, :dk] = st_f.astype(BF16)
        st_ref[cb, :, dk:] = st_b.astype(BF16)
        carry_ref[:, :dk] = st_f * e_ref[cf, 0:1, :dk] + ut_ref[cf, :, :dk]
        carry_ref[:, dk:] = st_b * e_ref[cb, 0:1, dk:] + ut_ref[cb, :, dk:]
        return carry

    lax.fori_loop(0, n, phase_b, 0, unroll=4)

    def phase_c(c, carry):
        rows = rows_of(c)
        o = osum_ref[rows, :] + _dot_nt(qi_ref[rows, :], st_ref[c])
        ms = jnp.mean(o * o, axis=-1, keepdims=True)
        y = o * lax.rsqrt(ms + EPS) * gn_ref[...] * gg_ref[0, rows, :].astype(F32)
        o_ref[0, rows, :] = y.astype(BF16)
        return carry

    lax.fori_loop(0, n, phase_c, 0, unroll=16)


def _gla(pk, gs, gn):
    B, H, S, _ = pk.shape
    dk, dv = GLA_DK, GLA_DV
    n = S // GLA_C
    hs = lambda w: pl.BlockSpec((1, 1, S, w), lambda b, h: (b, h, 0, 0))
    cs = pl.BlockSpec((1, S, dv), lambda b, h: (b, 0, h))
    return pl.pallas_call(
        _gla_kernel, grid=(B, H),
        in_specs=[hs(PK_W), cs, _resident(gn.shape)],
        out_specs=cs,
        out_shape=jax.ShapeDtypeStruct((B, S, H * dv), BF16),
        scratch_shapes=[
            pltpu.VMEM((S, dv), F32),
            pltpu.VMEM((S, 2 * dk), BF16),
            pltpu.VMEM((GLA_RING, GLA_C, 2 * dk), BF16),
            pltpu.VMEM((GLA_RING, 2 * GLA_C, 2 * dk), BF16),
            pltpu.VMEM((GLA_RING, GLA_C, 2 * dk), BF16),
            pltpu.VMEM((GLA_RING, GLA_C, GLA_C), BF16),
            pltpu.VMEM((n, dv, 2 * dk), F32),
            pltpu.VMEM((n, dv, 2 * dk), BF16),
            pltpu.VMEM((n, 8, 2 * dk), F32),
            pltpu.VMEM((dv, 2 * dk), F32),
        ],
        compiler_params=pltpu.CompilerParams(
            dimension_semantics=("parallel", "parallel"), vmem_limit_bytes=VMEM_LIMIT),
        name="gla",
    )(pk, gs, gn)


def _attn_kernel(q_ref, k_ref, vt_ref, *refs, n_cast):
    for w_ref, wb_ref in zip(refs[:n_cast], refs[n_cast + 1:]):
        wb_ref[...] = w_ref[0].astype(BF16)
    o_ref = refs[n_cast]
    _attn_tile(q_ref, k_ref, vt_ref, o_ref)


def _attn_tile(q_ref, k_ref, vt_ref, o_ref):
    skv = k_ref.shape[1]
    nsub = q_ref.shape[1] // ATT_SUB
    uniform = [(lo, lo + ATT_KVB) for lo in range(0, skv, ATT_KVB)]

    def key_blocks(t):
        blocks = list(uniform)
        if t == 0:
            (lo, hi), rest = blocks[0], blocks[1:]
            blocks = [(lo, lo + ATT_KV_EDGE), (lo + ATT_KV_EDGE, hi)] + rest
        if t == nsub - 1:
            (lo, hi), rest = blocks[-1], blocks[:-1]
            blocks = rest + [(lo, hi - ATT_KV_EDGE), (hi - ATT_KV_EDGE, hi)]
        return blocks

    def sub_rows(t):
        return pl.ds(t * ATT_SUB if isinstance(t, int) else pl.multiple_of(t * ATT_SUB, ATT_SUB), ATT_SUB)

    def scores(q, blk):
        return _dot_nt(k_ref[0, blk[0]:blk[1], :], q)

    def emit(t, acc, den):
        o_ref[0, sub_rows(t), :] = (acc / den).T.astype(BF16)

    in_range = None
    for t in range(nsub):
        q = q_ref[0, sub_rows(t), :]
        acc = jnp.zeros((ATT_HD, ATT_SUB), F32)
        den = jnp.zeros((1, ATT_SUB), F32)
        blocks = key_blocks(t)
        s_next = scores(q, blocks[0])
        for j, (lo, hi) in enumerate(blocks):
            s = s_next
            if j + 1 < len(blocks):
                s_next = scores(q, blocks[j + 1])
            p = jnp.exp2(s)
            den = den + jnp.sum(p, axis=0, keepdims=True)
            acc = acc + _dot(vt_ref[0, :, lo:hi], p.astype(BF16))
        emit(t, acc, den)
        ok = jnp.logical_and(jnp.min(den) >= 2.0 ** -100, jnp.max(den) <= 2.0 ** 100)
        ok = jnp.logical_and(ok, jnp.max(jnp.abs(acc)) < jnp.inf)
        in_range = ok if in_range is None else jnp.logical_and(in_range, ok)

    @pl.when(jnp.logical_not(in_range))
    def _():
        def redo(t, carry):
            q = q_ref[0, sub_rows(t), :]
            m = jnp.full((1, ATT_SUB), -jnp.inf, F32)
            acc = jnp.zeros((ATT_HD, ATT_SUB), F32)
            den = jnp.zeros((1, ATT_SUB), F32)
            for lo, hi in uniform:
                s = scores(q, (lo, hi))
                m_new = jnp.maximum(m, jnp.max(s, axis=0, keepdims=True))
                alpha = jnp.exp2(m - m_new)
                p = jnp.exp2(s - m_new)
                den = den * alpha + jnp.sum(p, axis=0, keepdims=True)
                acc = acc * alpha + _dot(vt_ref[0, :, lo:hi], p.astype(BF16))
                m = m_new
            emit(t, acc, den)
            return carry

        lax.fori_loop(0, nsub, redo, 0)


def _attn(q, k, vt, weights_f32):
    G, NQ, hd = q.shape
    S = k.shape[1]
    ni = NQ // ATT_TQ
    nsteps = G * ni
    w_in_specs, w_out_specs, w_out_shapes = [], [], []
    for w in weights_f32:
        _, rows, cols = w.shape
        share = next(s for s in (1, 2, 4) if rows % (nsteps // s) == 0 and (rows // (nsteps // s)) % 16 == 0)
        slab = rows // (nsteps // share)
        w_in_specs.append(pl.BlockSpec((1, slab, cols), lambda g, i, share=share: (0, (g * ni + i) // share, 0)))
        w_out_specs.append(pl.BlockSpec((slab, cols), lambda g, i, share=share: ((g * ni + i) // share, 0)))
        w_out_shapes.append(jax.ShapeDtypeStruct((rows, cols), BF16))
    out = pl.pallas_call(
        functools.partial(_attn_kernel, n_cast=len(weights_f32)), grid=(G, ni),
        in_specs=[pl.BlockSpec((1, ATT_TQ, hd), lambda g, i: (g, i, 0)),
                  pl.BlockSpec((1, S, hd), lambda g, i: (g, 0, 0)),
                  pl.BlockSpec((1, vt.shape[1], S), lambda g, i: (g, 0, 0))] + w_in_specs,
        out_specs=[pl.BlockSpec((1, ATT_TQ, hd), lambda g, i: (g, i, 0))] + w_out_specs,
        out_shape=[jax.ShapeDtypeStruct((G, NQ, hd), BF16)] + w_out_shapes,
        compiler_params=pltpu.CompilerParams(
            dimension_semantics=("arbitrary", "arbitrary"), vmem_limit_bytes=VMEM_LIMIT),
        name="attn",
    )(q, k, vt, *weights_f32)
    return out[0], out[1:]


def _post_kernel(x_ref, ma_ref, yb_ref, sgb_ref, wo_ref, fng_ref, wgu_ref, wd_ref, fin_ref, o_ref):
    dff = wd_ref.shape[0]
    tm = x_ref.shape[1]
    groups = [slice(r, r + POST_SUB) for r in range(0, tm, POST_SUB)]

    def rms(v, g_ref):
        return v * lax.rsqrt(jnp.mean(v * v, axis=-1, keepdims=True) + EPS) * g_ref[...]

    x1 = []
    for rs in groups:
        yb = jnp.concatenate([yb_ref[0, hh, rs, :] for hh in range(ATT_HEADS)], axis=-1)
        mixed = ma_ref[0, rs, :].astype(F32) + _sigmoid(sgb_ref[0, rs, :].astype(F32)) * yb.astype(F32)
        x1.append(x_ref[0, rs, :] + _dot(mixed.astype(BF16), wo_ref[...]))
    act = []
    for v in x1:
        h2 = rms(v, fng_ref).astype(BF16)
        gt = _dot(h2, wgu_ref[:, :dff])
        up = _dot(h2, wgu_ref[:, dff:])
        act.append((gt / (1.0 + jnp.exp(-gt)) * up).astype(BF16))
    for rs, v, a in zip(groups, x1, act):
        o_ref[0, rs, :] = rms(v + _dot(a, wd_ref[...]), fin_ref)


def _post(x, ma, yb, sgb, wo, fng, wgu, wd, fin):
    B, S, D = x.shape
    tm = POST_TM
    rowb = pl.BlockSpec((1, tm, D), lambda b, i: (b, i, 0))
    return pl.pallas_call(
        _post_kernel, grid=(B, S // tm),
        in_specs=[rowb, rowb,
                  pl.BlockSpec((1, ATT_HEADS, tm, ATT_HD), lambda b, i: (b, 0, i, 0)),
                  pl.BlockSpec((1, tm, D), lambda b, i: (b, i, 1)),
                  _resident(wo.shape), _resident(fng.shape), _resident(wgu.shape),
                  _resident(wd.shape), _resident(fin.shape)],
        out_specs=rowb,
        out_shape=jax.ShapeDtypeStruct((B, S, D), F32),
        compiler_params=pltpu.CompilerParams(
            dimension_semantics=("parallel", "parallel"), vmem_limit_bytes=VMEM_LIMIT),
        name="post",
    )(x, ma, yb, sgb, wo, fng, wgu, wd, fin)


def _rope_tables(seq_len):
    pos = np.arange(seq_len)
    freqs = ROPE_THETA ** (-np.arange(0, 64, 2, dtype=np.float32) / 64.0)
    ang_r = (pos // GRID_W).astype(np.float32)[:, None] * freqs[None, :]
    ang_c = (pos % GRID_W).astype(np.float32)[:, None] * freqs[None, :]
    zero = np.zeros_like(ang_r)
    rc = np.concatenate([np.cos(ang_r), np.cos(ang_r), np.cos(ang_c), np.cos(ang_c)], axis=1)
    rs1 = np.concatenate([-np.sin(ang_r), zero, -np.sin(ang_c), zero], axis=1)
    rs2 = np.concatenate([zero, np.sin(ang_r), zero, np.sin(ang_c)], axis=1)
    return (jnp.asarray(rc, F32), jnp.asarray(rs1, F32), jnp.asarray(rs2, F32))


def kernel(x, mix_norm, w_in, w_gk_fwd, b_gk_fwd, w_gk_bwd, b_gk_bwd, gla_norm, q_norm, k_norm,
           b_gate, w_o, ffn_norm, w_gate_up, w_down, final_norm):
    B, S, D = x.shape
    assert w_in.shape[0] == 1, "single-layer block"
    assert S % max(IN_TM, GLA_C, POST_TM, ATT_KVB) == 0 and (ATT_GROUP * S) % ATT_TQ == 0

    wt = jnp.swapaxes(w_in[0], 0, 1).astype(BF16)
    nqk = GLA_HEADS * GLA_DK
    wgk = jnp.zeros((LANES, 2 * nqk), F32)
    wgk = wgk.at[:GLA_RANK, :nqk].set(w_gk_fwd[0]).at[GLA_RANK:2 * GLA_RANK, nqk:].set(w_gk_bwd[0])
    wgk = wgk.astype(BF16)
    bgk = jnp.concatenate([b_gk_fwd[0], b_gk_bwd[0]])[None, :]
    rc, rs1, rs2 = _rope_tables(S)

    (pk, gs, aq, ak, avt) = _in_proj(
        x, mix_norm, wt, wgk, bgk, q_norm, k_norm, b_gate[0], rc, rs1, rs2)

    ma = _gla(pk, gs, gla_norm)

    yb, (wo, wgu, wd) = _attn(aq.reshape(B * ATT_KV_HEADS, ATT_GROUP * S, ATT_HD),
                              ak.reshape(B * ATT_KV_HEADS, S, ATT_HD),
                              avt.reshape(B * ATT_KV_HEADS, ATT_HD, S),
                              (w_o, w_gate_up, w_down))
    yb = yb.reshape(B, ATT_HEADS, S, ATT_HD)

    return _post(x, ma, yb, gs, wo, ffn_norm, wgu, wd, final_norm[None, :])
```

```python
import functools
import math

import jax
import jax.numpy as jnp
import numpy as np
from jax import lax
from jax.experimental import pallas as pl
from jax.experimental.pallas import tpu as pltpu

F32 = jnp.float32
BF16 = jnp.bfloat16

EPS = 1e-6
GRID_W = 64
ROPE_THETA = 10000.0
GLA_HEADS = 4
GLA_DK = 128
GLA_DV = 256
GLA_RANK = 16
GLA_GATE_NORM = 16.0
D_MODEL = 1024
PK_Q, PK_K, PK_LA, PK_V, PK_W = 0, 128, 256, 512, 768
ATT_HD = 128
ATT_HEADS = 8
ATT_KV_HEADS = 2
ATT_GROUP = ATT_HEADS // ATT_KV_HEADS

LANES = 128
VMEM_LIMIT = 56 * 1024 * 1024

IN_TM = 512
IN_SUB = 256
GLA_C = 128
GLA_LAG_SCORES = 2
GLA_LAG_OUT = 2
GLA_RING = 7
GLA_UNROLL = 28
ATT_TQ = 4096
ATT_SUB = 1024
ATT_KVB = 1024
POST_TM = 512
POST_SUB = 256


def _dot(a, b):
    return jnp.dot(a, b, preferred_element_type=F32)


def _dot_nt(a, b):
    return lax.dot_general(a, b, (((1,), (1,)), ((), ())), preferred_element_type=F32)


def _sigmoid(x):
    return 1.0 / (1.0 + jnp.exp(-x))


def _resident(shape):
    nd = len(shape)
    return pl.BlockSpec(shape, lambda *_: (0,) * nd, pipeline_mode=pl.Buffered(1))


def _segments():
    sizes = (("gq", GLA_HEADS * GLA_DK), ("gk", GLA_HEADS * GLA_DK), ("gv", GLA_HEADS * GLA_DV),
             ("go", GLA_HEADS * GLA_DV), ("rf", GLA_RANK), ("rb", GLA_RANK),
             ("aq", ATT_HEADS * ATT_HD), ("ak", ATT_KV_HEADS * ATT_HD), ("av", ATT_KV_HEADS * ATT_HD),
             ("ga", D_MODEL), ("gb", D_MODEL))
    at, pos = {}, 0
    for name, width in sizes:
        at[name] = pos
        pos += width
    return {"gq": (at["gq"], at["gk"]), "gk": (at["gk"], at["gv"]), "gv": (at["gv"], at["go"]),
            "go": (at["go"], at["rf"]), "r": (at["rf"], at["rf"] + LANES), "aq": (at["aq"], at["ak"]),
            "akv": (at["ak"], at["ga"]), "ga": (at["ga"], at["gb"]), "gb": (at["gb"], pos)}


SEG = _segments()
ROPE_PAIR = ATT_HD // 4


def _in_proj_kernel(x_ref, mixg_ref, wt_ref, wgk_ref, bgk_ref, qn_ref, kn_ref, bg_ref,
                    rc_ref, rs1_ref, rs2_ref,
                    pk_ref, gs_ref,
                    aq_ref, ak_ref, avt_ref):
    for r0 in range(0, x_ref.shape[1], IN_SUB):
        rs = pl.ds(r0, IN_SUB)
        _in_proj_rows(x_ref.at[:, rs, :], mixg_ref, wt_ref, wgk_ref, bgk_ref, qn_ref, kn_ref, bg_ref,
                      rc_ref.at[rs, :], rs1_ref.at[rs, :], rs2_ref.at[rs, :],
                      pk_ref.at[:, :, rs, :], gs_ref.at[:, rs, :],
                      aq_ref.at[:, :, rs, :], ak_ref.at[:, :, rs, :], avt_ref.at[:, :, :, rs])


def _in_proj_rows(x_ref, mixg_ref, wt_ref, wgk_ref, bgk_ref, qn_ref, kn_ref, bg_ref,
                  rc_ref, rs1_ref, rs2_ref,
                  pk_ref, gs_ref,
                  aq_ref, ak_ref, avt_ref):
    x = x_ref[0]
    ms = jnp.mean(x * x, axis=-1, keepdims=True)
    h = (x * lax.rsqrt(ms + EPS) * mixg_ref[...]).astype(BF16)

    def mm(name):
        lo, hi = SEG[name]
        return _dot_nt(h, wt_ref[lo:hi, :])


    rc, rs1, rs2 = rc_ref[...], rs1_ref[...], rs2_ref[...]

    def rope_tables(g, scale):
        g = jnp.broadcast_to(g, (8, ATT_HD)) * scale
        return (rc * g[0:1], rs1 * pltpu.roll(g, ATT_HD - ROPE_PAIR, 1)[0:1],
                rs2 * pltpu.roll(g, ROPE_PAIR, 1)[0:1])

    def norm_rope(a, tabs):
        r = lax.rsqrt(jnp.mean(a * a, axis=-1, keepdims=True) + EPS)
        return (a * tabs[0] + pltpu.roll(a, ATT_HD - ROPE_PAIR, 1) * tabs[1]
                + pltpu.roll(a, ROPE_PAIR, 1) * tabs[2]) * r

    r = mm("r").astype(BF16)

    aq = mm("aq")
    q_tabs = rope_tables(qn_ref[...], (ATT_HD ** -0.5) * math.log2(math.e))
    for hh in range(ATT_HEADS):
        aq_ref[0, hh] = norm_rope(aq[:, hh * ATT_HD:(hh + 1) * ATT_HD], q_tabs).astype(BF16)

    go = mm("go")
    ga = mm("ga") + bg_ref[0:1, :]
    gs_ref[0, :, :D_MODEL] = (go / ((1.0 + jnp.exp(-go)) * (1.0 + jnp.exp(-ga)))).astype(BF16)

    akv = mm("akv")
    k_tabs = rope_tables(kn_ref[...], 1.0)
    for hh in range(ATT_KV_HEADS):
        ak_ref[0, hh] = norm_rope(akv[:, hh * ATT_HD:(hh + 1) * ATT_HD], k_tabs).astype(BF16)
        v = akv[:, (ATT_KV_HEADS + hh) * ATT_HD:(ATT_KV_HEADS + hh + 1) * ATT_HD]
        avt_ref[0, hh] = v.T.astype(BF16)

    z = _dot(r, wgk_ref[...]) + bgk_ref[...]
    la = (jnp.minimum(z, 0.0) - jnp.log(1.0 + jnp.exp(-jnp.abs(z)))) * (1.0 / GLA_GATE_NORM)
    nqk = GLA_HEADS * GLA_DK
    for hh in range(GLA_HEADS):
        pk_ref[0, hh, :, PK_LA:PK_V] = jnp.concatenate(
            [la[:, hh * GLA_DK:(hh + 1) * GLA_DK], la[:, nqk + hh * GLA_DK:nqk + (hh + 1) * GLA_DK]],
            axis=1).astype(BF16)

    gb = mm("gb") + bg_ref[1:2, :]
    gs_ref[0, :, D_MODEL:] = gb.astype(BF16)

    gv = mm("gv")
    for hh in range(GLA_HEADS):
        vh = gv[:, hh * GLA_DV:(hh + 1) * GLA_DV]
        pk_ref[0, hh, :, PK_V:] = vh.astype(BF16)
    gq = mm("gq") * (GLA_DK ** -0.5)
    gk = mm("gk")
    for hh in range(GLA_HEADS):
        sl = slice(hh * GLA_DK, (hh + 1) * GLA_DK)
        pk_ref[0, hh, :, PK_Q:PK_K] = gq[:, sl].astype(BF16)
        pk_ref[0, hh, :, PK_K:PK_LA] = gk[:, sl].astype(BF16)


def _in_proj(x, mixg, wt, wgk, bgk, qn, kn, bg, rc, rs1, rs2):
    B, S, D = x.shape
    tm = IN_TM
    grid = (B, S // tm)

    def hb(nh, width):
        return pl.BlockSpec((1, nh, tm, width), lambda b, i: (b, 0, i, 0))

    rowb = pl.BlockSpec((1, tm, D), lambda b, i: (b, i, 0))
    tab = pl.BlockSpec((tm, LANES), lambda b, i: (i, 0))
    out_shape = (
        jax.ShapeDtypeStruct((B, GLA_HEADS, S, PK_W), BF16),
        jax.ShapeDtypeStruct((B, S, 2 * D), BF16),
        jax.ShapeDtypeStruct((B, ATT_HEADS, S, ATT_HD), BF16),
        jax.ShapeDtypeStruct((B, ATT_KV_HEADS, S, ATT_HD), BF16),
        jax.ShapeDtypeStruct((B, ATT_KV_HEADS, ATT_HD, S), BF16),
    )
    out_specs = (
        hb(GLA_HEADS, PK_W),
        pl.BlockSpec((1, tm, 2 * D), lambda b, i: (b, i, 0)),
        hb(ATT_HEADS, ATT_HD), hb(ATT_KV_HEADS, ATT_HD),
        pl.BlockSpec((1, ATT_KV_HEADS, ATT_HD, tm), lambda b, i: (b, 0, 0, i)),
    )
    in_specs = [
        rowb, _resident(mixg.shape), _resident(wt.shape), _resident(wgk.shape), _resident(bgk.shape),
        _resident(qn.shape), _resident(kn.shape), _resident(bg.shape), tab, tab, tab,
    ]
    return pl.pallas_call(
        _in_proj_kernel, grid=grid, in_specs=in_specs, out_specs=out_specs, out_shape=out_shape,
        compiler_params=pltpu.CompilerParams(
            dimension_semantics=("parallel", "parallel"), vmem_limit_bytes=VMEM_LIMIT),
        name="in_proj",
    )(x, mixg, wt, wgk, bgk, qn, kn, bg, rc, rs1, rs2)


def _gla_kernel(pk_ref, gg_ref, gn_ref, o_ref,
                osum_ref, qi_ref, qd_ring, kd_ring, ks_ring, s_ring, ut_ref, st_ref, e_ref, carry_ref):
    C, dk = GLA_C, GLA_DK
    S = pk_ref.shape[2]
    n = S // C

    row = lax.broadcasted_iota(jnp.int32, (C, C), 0)
    col = lax.broadcasted_iota(jnp.int32, (C, C), 1)
    tri = (row >= col).astype(BF16)
    fwd_pair = row >= col

    def rows_of(c):
        return pl.ds(c * C if isinstance(c, int) else pl.multiple_of(c * C, C), C)

    def prefix_sums(c):
        la = pk_ref[0, 0, rows_of(c), PK_LA:PK_V]
        return la, _dot(tri, la)

    def decay_factors(c, slot, la, pre):
        rows = rows_of(c)
        tot = pre[C - 1:C, :]
        dist_f = pre[:, :dk]
        dist_b = tot[:, dk:] - pre[:, dk:] + la[:, dk:].astype(F32)
        tot_f, tot_b = tot[:, :dk], tot[:, dk:]
        q = pk_ref[0, 0, rows, PK_Q:PK_K].astype(F32)
        k = pk_ref[0, 0, rows, PK_K:PK_LA].astype(F32)

        def factors(dist, mid, tt):
            qd = q * jnp.exp(dist - mid)
            kd = k * jnp.exp(mid - dist)
            qi = qd * jnp.exp(mid)
            ks = kd * jnp.exp(tt - mid)
            return qd.astype(BF16), kd.astype(BF16), qi.astype(BF16), ks.astype(BF16)

        qd_f, kd_f, qi_f, ks_f = factors(dist_f, dist_f[C // 2 - 1:C // 2, :], tot_f)
        qd_b, kd_b, qi_b, ks_b = factors(dist_b, dist_b[C // 2:C // 2 + 1, :], tot_b)
        qd_ring[slot] = jnp.concatenate([qd_f, qd_b], axis=1)
        kd_ring[slot, :C, :dk] = kd_f
        kd_ring[slot, C:, dk:] = kd_b
        ks_ring[slot] = jnp.concatenate([ks_f, ks_b], axis=1)
        qi_ref[rows, :] = jnp.concatenate([qi_f, qi_b], axis=1)
        e_ref[c] = jnp.broadcast_to(jnp.exp(tot), e_ref.shape[1:])

    def intra_scores(slot):
        s_fb = _dot_nt(qd_ring[slot], kd_ring[slot])
        s_ring[slot] = jnp.where(fwd_pair, s_fb[:, :C], s_fb[:, C:]).astype(BF16)

    def intra_out(c, slot):
        v = pk_ref[0, 0, rows_of(c), PK_V:]
        osum_ref[rows_of(c), :] = _dot(s_ring[slot], v)
        ut_ref[c] = _dot(v.T, ks_ring[slot])

    ring = qd_ring.shape[0]
    lag_s, lag_o = GLA_LAG_SCORES, GLA_LAG_SCORES + GLA_LAG_OUT
    unroll = GLA_UNROLL
    assert ring > lag_o and unroll % ring == 0 and (n - lag_o) % unroll == 0
    kd_ring[...] = jnp.zeros_like(kd_ring)

    def step(i, i_mod_ring):
        static = isinstance(i, int)
        live = lambda c: (not static) or 0 <= c < n
        slot = lambda lag: (i_mod_ring - lag) % ring
        if live(i):
            la, pre = prefix_sums(i)
        if live(i - lag_o):
            intra_out(i - lag_o, slot(lag_o))
        if live(i - lag_s):
            intra_scores(slot(lag_s))
        if live(i):
            decay_factors(i, slot(0), la, pre)

    def phase_a(it, carry):
        base = lag_o + it * unroll
        for u in range(unroll):
            step(base + u, (lag_o + u) % ring)
        return carry

    for i in range(lag_o):
        step(i, i % ring)
    if (n - lag_o) // unroll == 1:
        phase_a(0, 0)
    else:
        lax.fori_loop(0, (n - lag_o) // unroll, phase_a, 0)
    for i in range(n, n + lag_o):
        step(i, i % ring)

    carry_ref[...] = jnp.zeros_like(carry_ref)

    def phase_b(i, carry):
        cf, cb = i, n - 1 - i
        st_f = carry_ref[:, :dk]
        st_b = carry_ref[:, dk:]
        st_ref[cf, :, :dk] = st_f.astype(BF16)
        st_ref[cb, :, dk:] = st_b.astype(BF16)
        carry_ref[:, :dk] = st_f * e_ref[cf, 0:1, :dk] + ut_ref[cf, :, :dk]
        carry_ref[:, dk:] = st_b * e_ref[cb, 0:1, dk:] + ut_ref[cb, :, dk:]
        return carry

    lax.fori_loop(0, n, phase_b, 0, unroll=4)

    def phase_c(c, carry):
        rows = rows_of(c)
        o = osum_ref[rows, :] + _dot_nt(qi_ref[rows, :], st_ref[c])
        ms = jnp.mean(o * o, axis=-1, keepdims=True)
        y = o * lax.rsqrt(ms + EPS) * gn_ref[...] * gg_ref[0, rows, :].astype(F32)
        o_ref[0, rows, :] = y.astype(BF16)
        return carry

    lax.fori_loop(0, n, phase_c, 0, unroll=16)


def _gla(pk, gs, gn):
    B, H, S, _ = pk.shape
    dk, dv = GLA_DK, GLA_DV
    n = S // GLA_C
    hs = lambda w: pl.BlockSpec((1, 1, S, w), lambda b, h: (b, h, 0, 0))
    cs = pl.BlockSpec((1, S, dv), lambda b, h: (b, 0, h))
    return pl.pallas_call(
        _gla_kernel, grid=(B, H),
        in_specs=[hs(PK_W), cs, _resident(gn.shape)],
        out_specs=cs,
        out_shape=jax.ShapeDtypeStruct((B, S, H * dv), BF16),
        scratch_shapes=[
            pltpu.VMEM((S, dv), F32),
            pltpu.VMEM((S, 2 * dk), BF16),
            pltpu.VMEM((GLA_RING, GLA_C, 2 * dk), BF16),
            pltpu.VMEM((GLA_RING, 2 * GLA_C, 2 * dk), BF16),
            pltpu.VMEM((GLA_RING, GLA_C, 2 * dk), BF16),
            pltpu.VMEM((GLA_RING, GLA_C, GLA_C), BF16),
            pltpu.VMEM((n, dv, 2 * dk), F32),
            pltpu.VMEM((n, dv, 2 * dk), BF16),
            pltpu.VMEM((n, 8, 2 * dk), F32),
            pltpu.VMEM((dv, 2 * dk), F32),
        ],
        compiler_params=pltpu.CompilerParams(
            dimension_semantics=("parallel", "parallel"), vmem_limit_bytes=VMEM_LIMIT),
        name="gla",
    )(pk, gs, gn)


def _attn_kernel(q_ref, k_ref, vt_ref, *refs, n_cast):
    for w_ref, wb_ref in zip(refs[:n_cast], refs[n_cast + 1:]):
        wb_ref[...] = w_ref[0].astype(BF16)
    o_ref = refs[n_cast]
    _attn_tile(q_ref, k_ref, vt_ref, o_ref)


def _attn_tile(q_ref, k_ref, vt_ref, o_ref):
    skv = k_ref.shape[1]
    nblk = skv // ATT_KVB
    nsub = q_ref.shape[1] // ATT_SUB

    def sub_rows(t):
        return pl.ds(t * ATT_SUB if isinstance(t, int) else pl.multiple_of(t * ATT_SUB, ATT_SUB), ATT_SUB)

    def scores(q, j):
        return _dot_nt(k_ref[0, j * ATT_KVB:(j + 1) * ATT_KVB, :], q)

    def emit(t, acc, den):
        o_ref[0, sub_rows(t), :] = (acc / den).T.astype(BF16)

    in_range = None
    for t in range(nsub):
        q = q_ref[0, sub_rows(t), :]
        acc = jnp.zeros((ATT_HD, ATT_SUB), F32)
        den = jnp.zeros((1, ATT_SUB), F32)
        s_next = scores(q, 0)
        for j in range(nblk):
            s = s_next
            if j + 1 < nblk:
                s_next = scores(q, j + 1)
            p = jnp.exp2(s)
            den = den + jnp.sum(p, axis=0, keepdims=True)
            acc = acc + _dot(vt_ref[0, :, j * ATT_KVB:(j + 1) * ATT_KVB], p.astype(BF16))
        emit(t, acc, den)
        ok = jnp.logical_and(jnp.min(den) >= 2.0 ** -100, jnp.max(den) <= 2.0 ** 100)
        ok = jnp.logical_and(ok, jnp.max(jnp.abs(acc)) < jnp.inf)
        in_range = ok if in_range is None else jnp.logical_and(in_range, ok)

    @pl.when(jnp.logical_not(in_range))
    def _():
        def redo(t, carry):
            q = q_ref[0, sub_rows(t), :]
            m = jnp.full((1, ATT_SUB), -jnp.inf, F32)
            acc = jnp.zeros((ATT_HD, ATT_SUB), F32)
            den = jnp.zeros((1, ATT_SUB), F32)
            for j in range(nblk):
                s = scores(q, j)
                m_new = jnp.maximum(m, jnp.max(s, axis=0, keepdims=True))
                alpha = jnp.exp2(m - m_new)
                p = jnp.exp2(s - m_new)
                den = den * alpha + jnp.sum(p, axis=0, keepdims=True)
                acc = acc * alpha + _dot(vt_ref[0, :, j * ATT_KVB:(j + 1) * ATT_KVB], p.astype(BF16))
                m = m_new
            emit(t, acc, den)
            return carry

        lax.fori_loop(0, nsub, redo, 0)


def _attn(q, k, vt, weights_f32):
    G, NQ, hd = q.shape
    S = k.shape[1]
    ni = NQ // ATT_TQ
    nsteps = G * ni
    w_in_specs, w_out_specs, w_out_shapes = [], [], []
    for w in weights_f32:
        _, rows, cols = w.shape
        share = next(s for s in (1, 2, 4) if rows % (nsteps // s) == 0 and (rows // (nsteps // s)) % 16 == 0)
        slab = rows // (nsteps // share)
        w_in_specs.append(pl.BlockSpec((1, slab, cols), lambda g, i, share=share: (0, (g * ni + i) // share, 0)))
        w_out_specs.append(pl.BlockSpec((slab, cols), lambda g, i, share=share: ((g * ni + i) // share, 0)))
        w_out_shapes.append(jax.ShapeDtypeStruct((rows, cols), BF16))
    out = pl.pallas_call(
        functools.partial(_attn_kernel, n_cast=len(weights_f32)), grid=(G, ni),
        in_specs=[pl.BlockSpec((1, ATT_TQ, hd), lambda g, i: (g, i, 0)),
                  pl.BlockSpec((1, S, hd), lambda g, i: (g, 0, 0)),
                  pl.BlockSpec((1, vt.shape[1], S), lambda g, i: (g, 0, 0))] + w_in_specs,
        out_specs=[pl.BlockSpec((1, ATT_TQ, hd), lambda g, i: (g, i, 0))] + w_out_specs,
        out_shape=[jax.ShapeDtypeStruct((G, NQ, hd), BF16)] + w_out_shapes,
        compiler_params=pltpu.CompilerParams(
            dimension_semantics=("arbitrary", "arbitrary"), vmem_limit_bytes=VMEM_LIMIT),
        name="attn",
    )(q, k, vt, *weights_f32)
    return out[0], out[1:]


def _post_kernel(x_ref, ma_ref, yb_ref, sgb_ref, wo_ref, fng_ref, wgu_ref, wd_ref, fin_ref, o_ref):
    dff = wd_ref.shape[0]
    tm = x_ref.shape[1]
    groups = [slice(r, r + POST_SUB) for r in range(0, tm, POST_SUB)]

    def rms(v, g_ref):
        return v * lax.rsqrt(jnp.mean(v * v, axis=-1, keepdims=True) + EPS) * g_ref[...]

    x1 = []
    for rs in groups:
        yb = jnp.concatenate([yb_ref[0, hh, rs, :] for hh in range(ATT_HEADS)], axis=-1)
        mixed = ma_ref[0, rs, :].astype(F32) + _sigmoid(sgb_ref[0, rs, :].astype(F32)) * yb.astype(F32)
        x1.append(x_ref[0, rs, :] + _dot(mixed.astype(BF16), wo_ref[...]))
    act = []
    for v in x1:
        h2 = rms(v, fng_ref).astype(BF16)
        gt = _dot(h2, wgu_ref[:, :dff])
        up = _dot(h2, wgu_ref[:, dff:])
        act.append((gt / (1.0 + jnp.exp(-gt)) * up).astype(BF16))
    for rs, v, a in zip(groups, x1, act):
        o_ref[0, rs, :] = rms(v + _dot(a, wd_ref[...]), fin_ref)


def _post(x, ma, yb, sgb, wo, fng, wgu, wd, fin):
    B, S, D = x.shape
    tm = POST_TM
    rowb = pl.BlockSpec((1, tm, D), lambda b, i: (b, i, 0))
    return pl.pallas_call(
        _post_kernel, grid=(B, S // tm),
        in_specs=[rowb, rowb,
                  pl.BlockSpec((1, ATT_HEADS, tm, ATT_HD), lambda b, i: (b, 0, i, 0)),
                  pl.BlockSpec((1, tm, D), lambda b, i: (b, i, 1)),
                  _resident(wo.shape), _resident(fng.shape), _resident(wgu.shape),
                  _resident(wd.shape), _resident(fin.shape)],
        out_specs=rowb,
        out_shape=jax.ShapeDtypeStruct((B, S, D), F32),
        compiler_params=pltpu.CompilerParams(
            dimension_semantics=("parallel", "parallel"), vmem_limit_bytes=VMEM_LIMIT),
        name="post",
    )(x, ma, yb, sgb, wo, fng, wgu, wd, fin)


def _rope_tables(seq_len):
    pos = np.arange(seq_len)
    freqs = ROPE_THETA ** (-np.arange(0, 64, 2, dtype=np.float32) / 64.0)
    ang_r = (pos // GRID_W).astype(np.float32)[:, None] * freqs[None, :]
    ang_c = (pos % GRID_W).astype(np.float32)[:, None] * freqs[None, :]
    zero = np.zeros_like(ang_r)
    rc = np.concatenate([np.cos(ang_r), np.cos(ang_r), np.cos(ang_c), np.cos(ang_c)], axis=1)
    rs1 = np.concatenate([-np.sin(ang_r), zero, -np.sin(ang_c), zero], axis=1)
    rs2 = np.concatenate([zero, np.sin(ang_r), zero, np.sin(ang_c)], axis=1)
    return (jnp.asarray(rc, F32), jnp.asarray(rs1, F32), jnp.asarray(rs2, F32))


def kernel(x, mix_norm, w_in, w_gk_fwd, b_gk_fwd, w_gk_bwd, b_gk_bwd, gla_norm, q_norm, k_norm,
           b_gate, w_o, ffn_norm, w_gate_up, w_down, final_norm):
    B, S, D = x.shape
    assert w_in.shape[0] == 1, "single-layer block"
    assert S % max(IN_TM, GLA_C, POST_TM, ATT_KVB) == 0 and (ATT_GROUP * S) % ATT_TQ == 0

    wt = jnp.swapaxes(w_in[0], 0, 1).astype(BF16)
    nqk = GLA_HEADS * GLA_DK
    wgk = jnp.zeros((LANES, 2 * nqk), F32)
    wgk = wgk.at[:GLA_RANK, :nqk].set(w_gk_fwd[0]).at[GLA_RANK:2 * GLA_RANK, nqk:].set(w_gk_bwd[0])
    wgk = wgk.astype(BF16)
    bgk = jnp.concatenate([b_gk_fwd[0], b_gk_bwd[0]])[None, :]
    rc, rs1, rs2 = _rope_tables(S)

    (pk, gs, aq, ak, avt) = _in_proj(
        x, mix_norm, wt, wgk, bgk, q_norm, k_norm, b_gate[0], rc, rs1, rs2)

    ma = _gla(pk, gs, gla_norm)

    yb, (wo, wgu, wd) = _attn(aq.reshape(B * ATT_KV_HEADS, ATT_GROUP * S, ATT_HD),
                              ak.reshape(B * ATT_KV_HEADS, S, ATT_HD),
                              avt.reshape(B * ATT_KV_HEADS, ATT_HD, S),
                              (w_o, w_gate_up, w_down))
    yb = yb.reshape(B, ATT_HEADS, S, ATT_HD)

    return _post(x, ma, yb, gs, wo, ffn_norm, wgu, wd, final_norm[None, :])
```
